```python
import jax, jax.numpy as jnp
from jax import lax
import numpy as np

D_MODEL = 1024
BATCH = 8
SEQ = 8192
DEPTH = 2

CHUNK = 64
PLE_DIM = 256
D_FF = 2816
CONV_W = 4
RET_HEADS = 4
RET_DK = 64
RET_DV = 64
RET_W = RET_HEADS * RET_DV
LRU_W = 384
LRU_BLOCKS = 6
LRU_BLOCK = LRU_W // LRU_BLOCKS
LRU_C = 8.0
GDN_HEADS = 6
GDN_DK = 64
GDN_DV = 64
GDN_W = GDN_HEADS * GDN_DV
D_MIX = RET_W + LRU_W + GDN_W
IN_WIDTHS = (RET_W, RET_W, RET_W, RET_W, LRU_W, LRU_W, GDN_W, GDN_W, GDN_W, GDN_W, GDN_HEADS, GDN_HEADS)
D_IN = sum(IN_WIDTHS)
ROPE_THETA = 10000.0
ALPHA = (2 * DEPTH) ** 0.25
BETA_INIT = (8 * DEPTH) ** -0.25
LN_EPS = 1e-5

kernel_name = 'hybrid_retention_rglru_gdn_macaron_deepnorm'


def layer_norm(x, g, b):
    xf = x.astype(jnp.float32)
    mu = jnp.mean(xf, -1, keepdims=True)
    var = jnp.mean(jnp.square(xf - mu), -1, keepdims=True)
    return ((xf - mu) * lax.rsqrt(var + LN_EPS) * g + b).astype(x.dtype)


def head_norm(t, eps=1e-5):
    mu = jnp.mean(t, -1, keepdims=True)
    return (t - mu) * lax.rsqrt(jnp.var(t, -1, keepdims=True) + eps)


def rms_norm(t, g, eps=1e-6):
    return t * lax.rsqrt(jnp.mean(t * t, -1, keepdims=True) + eps) * g


def l2_normalize(t, eps=1e-6):
    return t * lax.rsqrt(jnp.sum(t * t, -1, keepdims=True) + eps)


def swiglu(x, w_gate, w_up, w_down):
    return (jax.nn.silu(x @ w_gate) * (x @ w_up)) @ w_down


def causal_depthwise_conv(x, w, b=None):
    K = w.shape[0]
    S = x.shape[1]
    xp = jnp.pad(x, ((0, 0), (K - 1, 0), (0, 0)))
    y = sum(w[k] * xp[:, k:k + S] for k in range(K))
    return y if b is None else y + b


def rotary(t, positions):
    half = t.shape[-1] // 2
    inv_freq = ROPE_THETA ** (-jnp.arange(half, dtype=jnp.float32) / half)
    ang = positions.astype(jnp.float32)[..., None] * inv_freq
    cos = jnp.cos(ang)[:, :, None, :]
    sin = jnp.sin(ang)[:, :, None, :]
    t1, t2 = t[..., :half], t[..., half:]
    return jnp.concatenate([t1 * cos - t2 * sin, t2 * cos + t1 * sin], -1)


def retention_chunked(q, k, v, positions):
    B, S, H, _ = q.shape
    N = S // CHUNK
    q = rotary(q, positions) * RET_DK ** -0.5
    k = rotary(k, positions)
    log_gamma = jnp.log1p(-jnp.exp2(-5.0 - jnp.arange(H, dtype=jnp.float32)))
    idx = jnp.arange(CHUNK, dtype=jnp.float32)
    intra = jnp.exp(jnp.abs(idx[:, None] - idx[None, :])[None] * log_gamma[:, None, None])
    cross = jnp.exp((idx + 1.0)[None] * log_gamma[:, None])
    tail = jnp.exp((CHUNK - 1.0 - idx)[None] * log_gamma[:, None])
    chunk_decay = jnp.exp(CHUNK * log_gamma)
    qc = q.reshape(B, N, CHUNK, H, RET_DK)
    kc = k.reshape(B, N, CHUNK, H, RET_DK)
    vc = v.reshape(B, N, CHUNK, H, RET_DV)
    scores = jnp.einsum('bnihd,bnjhd->bnhij', qc, kc) * intra
    o_intra = jnp.einsum('bnhij,bnjhe->bnihe', scores, vc)
    kv = jnp.einsum('bnjhd,hj,bnjhe->nbhde', kc, tail, vc)

    def step(state, kv_n):
        return state * chunk_decay[None, :, None, None] + kv_n, state

    _, states = lax.scan(step, jnp.zeros((B, H, RET_DK, RET_DV), jnp.float32), kv)
    o_cross = jnp.einsum('bnihd,hi,nbhde->bnihe', qc, cross, states)
    return (o_intra + o_cross).reshape(B, S, H, RET_DV)


def rg_lru(x, conv_w, conv_b, w_a, b_a, w_x, b_x, lam):
    B, S, _ = x.shape
    x = causal_depthwise_conv(x, conv_w, conv_b)
    xb = x.reshape(B, S, LRU_BLOCKS, LRU_BLOCK)
    r = jax.nn.sigmoid(jnp.einsum('bsgi,gij->bsgj', xb, w_a).reshape(B, S, LRU_W) + b_a)
    i = jax.nn.sigmoid(jnp.einsum('bsgi,gij->bsgj', xb, w_x).reshape(B, S, LRU_W) + b_x)
    log_a = LRU_C * r * jax.nn.log_sigmoid(lam)
    a = jnp.exp(log_a)
    b = jnp.sqrt(-jnp.expm1(2.0 * log_a)) * (i * x)

    def combine(left, right):
        a1, b1 = left
        a2, b2 = right
        return a1 * a2, a2 * b1 + b2

    _, h = lax.associative_scan(combine, (a, b), axis=1)
    return h


def gated_delta_chunked(q, k, v, g, beta):
    B, S, H, DK = q.shape
    DV = v.shape[-1]
    N = S // CHUNK

    def to_chunks(t):
        return jnp.moveaxis(t.reshape(B, N, CHUNK, H, *t.shape[3:]), 3, 2)

    q = to_chunks(q) * DK ** -0.5
    k = to_chunks(k)
    v = to_chunks(v)
    g = to_chunks(g)
    beta = to_chunks(beta)
    gc = jnp.cumsum(g, axis=-1)
    incl = jnp.tril(jnp.ones((CHUNK, CHUNK), bool))
    strict = jnp.tril(jnp.ones((CHUNK, CHUNK), bool), -1)
    decay = jnp.where(incl, jnp.exp(jnp.minimum(gc[..., :, None] - gc[..., None, :], 0.0)), 0.0)
    kk = jnp.einsum('bnhid,bnhjd->bnhij', k, k)
    a_mat = jnp.where(strict, beta[..., :, None] * kk * decay, 0.0) + jnp.eye(CHUNK, dtype=jnp.float32)
    rhs = jnp.concatenate([v * beta[..., None], k * (beta * jnp.exp(gc))[..., None]], -1)
    sol = lax.linalg.triangular_solve(a_mat, rhs, left_side=True, lower=True, unit_diagonal=True)
    u, w = sol[..., :DV], sol[..., DV:]
    qk = jnp.einsum('bnhid,bnhjd->bnhij', q, k) * decay
    q_dec = q * jnp.exp(gc)[..., None]
    g_last = gc[..., -1]
    k_tail = k * jnp.exp(g_last[..., None] - gc)[..., None]
    xs = (jnp.moveaxis(u, 1, 0), jnp.moveaxis(w, 1, 0), jnp.moveaxis(qk, 1, 0),
          jnp.moveaxis(q_dec, 1, 0), jnp.moveaxis(k_tail, 1, 0), jnp.moveaxis(g_last, 1, 0))

    def step(state, inp):
        u_n, w_n, qk_n, qd_n, kt_n, gl_n = inp
        v_new = u_n - jnp.einsum('bhcd,bhde->bhce', w_n, state)
        o = jnp.einsum('bhcd,bhde->bhce', qd_n, state) + jnp.einsum('bhij,bhje->bhie', qk_n, v_new)
        state = state * jnp.exp(gl_n)[..., None, None] + jnp.einsum('bhcd,bhce->bhde', kt_n, v_new)
        return state, o

    _, o = lax.scan(step, jnp.zeros((B, H, DK, DV), jnp.float32), xs)
    return jnp.transpose(o, (1, 0, 3, 2, 4)).reshape(B, S, H, DV)


def hybrid_mixer(x, positions, w_in, ret_norm_g, lru_conv_w, lru_conv_b, lru_w_a, lru_b_a,
                 lru_w_x, lru_b_x, lru_lambda, gdn_conv_w, gdn_a_log, gdn_dt_bias, gdn_norm_g, w_out):
    B, S, _ = x.shape
    h = (x @ w_in).astype(jnp.float32)
    splits = np.cumsum(IN_WIDTHS)[:-1].tolist()
    q_r, k_r, v_r, g_r, x_l, gate_l, q_g, k_g, v_g, z_g, a_g, b_g = jnp.split(h, splits, axis=-1)

    o_r = retention_chunked(q_r.reshape(B, S, RET_HEADS, RET_DK), k_r.reshape(B, S, RET_HEADS, RET_DK),
                            v_r.reshape(B, S, RET_HEADS, RET_DV), positions)
    o_r = head_norm(o_r).reshape(B, S, RET_W) * ret_norm_g * jax.nn.silu(g_r)

    o_l = rg_lru(x_l, lru_conv_w, lru_conv_b, lru_w_a, lru_b_a, lru_w_x, lru_b_x, lru_lambda) * jax.nn.gelu(gate_l)

    qkv = jax.nn.silu(causal_depthwise_conv(jnp.concatenate([q_g, k_g, v_g], -1), gdn_conv_w))
    q_g, k_g, v_g = jnp.split(qkv, 3, axis=-1)
    q_g = l2_normalize(q_g.reshape(B, S, GDN_HEADS, GDN_DK))
    k_g = l2_normalize(k_g.reshape(B, S, GDN_HEADS, GDN_DK))
    v_g = v_g.reshape(B, S, GDN_HEADS, GDN_DV)
    beta = jax.nn.sigmoid(b_g)
    g = -jnp.exp(gdn_a_log) * jax.nn.softplus(a_g + gdn_dt_bias)
    o_g = gated_delta_chunked(q_g, k_g, v_g, g, beta)
    o_g = (rms_norm(o_g, gdn_norm_g) * jax.nn.silu(z_g.reshape(B, S, GDN_HEADS, GDN_DV))).reshape(B, S, GDN_W)

    o = jnp.concatenate([o_r, o_l, o_g], -1).astype(x.dtype)
    return o @ w_out


def setup_inputs(seed: int = 0) -> dict:
    key = jax.random.key(seed)
    counter = [0]

    def nk():
        counter[0] += 1
        return jax.random.fold_in(key, counter[0])

    f32 = jnp.float32
    L = DEPTH

    def nrm(shape, fan_in, scale=1.0):
        return jax.random.normal(nk(), shape, f32) * (scale * fan_in ** -0.5)

    def gain(shape):
        return 1.0 + 0.02 * jax.random.normal(nk(), shape, f32)

    def bias(shape):
        return 0.02 * jax.random.normal(nk(), shape, f32)

    x = jax.random.normal(nk(), (BATCH, SEQ, D_MODEL), f32)
    p = jax.random.normal(nk(), (DEPTH, BATCH, SEQ, PLE_DIM), f32)
    start = jax.random.randint(nk(), (BATCH, 1), 0, 4096, jnp.int32)
    positions = start + jnp.arange(SEQ, dtype=jnp.int32)[None, :]
    a0 = jax.random.uniform(nk(), (L, LRU_W), f32, 0.9, 0.999)
    lru_lambda = jnp.log(a0) - jnp.log1p(-a0)
    gdn_a_log = jnp.log(jax.random.uniform(nk(), (L, GDN_HEADS), f32, 1.0, 16.0))
    dt = jnp.exp(jax.random.uniform(nk(), (L, GDN_HEADS), f32, np.log(1e-3), np.log(1e-1)))
    gdn_dt_bias = dt + jnp.log(-jnp.expm1(-dt))
    return {
        'x': x,
        'p': p,
        'positions': positions,
        'ln_ffn1_g': gain((L, D_MODEL)),
        'ln_ffn1_b': bias((L, D_MODEL)),
        'ffn1_w_gate': nrm((L, D_MODEL, D_FF), D_MODEL),
        'ffn1_w_up': nrm((L, D_MODEL, D_FF), D_MODEL),
        'ffn1_w_down': nrm((L, D_FF, D_MODEL), D_FF, BETA_INIT),
        'w_in': nrm((L, D_MODEL, D_IN), D_MODEL),
        'ret_norm_g': gain((L, RET_W)),
        'lru_conv_w': nrm((L, CONV_W, LRU_W), CONV_W),
        'lru_conv_b': bias((L, LRU_W)),
        'lru_w_a': nrm((L, LRU_BLOCKS, LRU_BLOCK, LRU_BLOCK), LRU_BLOCK),
        'lru_b_a': bias((L, LRU_W)),
        'lru_w_x': nrm((L, LRU_BLOCKS, LRU_BLOCK, LRU_BLOCK), LRU_BLOCK),
        'lru_b_x': bias((L, LRU_W)),
        'lru_lambda': lru_lambda,
        'gdn_conv_w': nrm((L, CONV_W, 3 * GDN_W), CONV_W),
        'gdn_a_log': gdn_a_log,
        'gdn_dt_bias': gdn_dt_bias,
        'gdn_norm_g': gain((L, GDN_DV)),
        'w_out': nrm((L, D_MIX, D_MODEL), D_MIX, BETA_INIT),
        'ln_mix_g': gain((L, D_MODEL)),
        'ln_mix_b': bias((L, D_MODEL)),
        'ffn2_w_gate': nrm((L, D_MODEL, D_FF), D_MODEL),
        'ffn2_w_up': nrm((L, D_MODEL, D_FF), D_MODEL),
        'ffn2_w_down': nrm((L, D_FF, D_MODEL), D_FF, BETA_INIT),
        'ple_w_gate': nrm((L, D_MODEL, D_MODEL), D_MODEL),
        'ple_w_proj': nrm((L, PLE_DIM, D_MODEL), PLE_DIM, BETA_INIT),
        'ln_ffn2_g': gain((L, D_MODEL)),
        'ln_ffn2_b': bias((L, D_MODEL)),
    }


def reference(x, p, positions, ln_ffn1_g, ln_ffn1_b, ffn1_w_gate, ffn1_w_up, ffn1_w_down,
              w_in, ret_norm_g, lru_conv_w, lru_conv_b, lru_w_a, lru_b_a, lru_w_x, lru_b_x, lru_lambda,
              gdn_conv_w, gdn_a_log, gdn_dt_bias, gdn_norm_g, w_out, ln_mix_g, ln_mix_b,
              ffn2_w_gate, ffn2_w_up, ffn2_w_down, ple_w_gate, ple_w_proj, ln_ffn2_g, ln_ffn2_b):
    for i in range(DEPTH):
        x = layer_norm(ALPHA * x + 0.5 * swiglu(x, ffn1_w_gate[i], ffn1_w_up[i], ffn1_w_down[i]),
                       ln_ffn1_g[i], ln_ffn1_b[i])
        mix = hybrid_mixer(x, positions, w_in[i], ret_norm_g[i], lru_conv_w[i], lru_conv_b[i],
                           lru_w_a[i], lru_b_a[i], lru_w_x[i], lru_b_x[i], lru_lambda[i],
                           gdn_conv_w[i], gdn_a_log[i], gdn_dt_bias[i], gdn_norm_g[i], w_out[i])
        x = layer_norm(ALPHA * x + mix, ln_mix_g[i], ln_mix_b[i])
        ffn = 0.5 * swiglu(x, ffn2_w_gate[i], ffn2_w_up[i], ffn2_w_down[i])
        ple = jax.nn.sigmoid(x @ ple_w_gate[i]) * (p[i] @ ple_w_proj[i])
        x = layer_norm(ALPHA * x + ffn + ple, ln_ffn2_g[i], ln_ffn2_b[i])
    return x
```

```python
import functools

import numpy as np
import jax
import jax.numpy as jnp
from jax import lax
from jax.experimental import pallas as pl
from jax.experimental.pallas import tpu as pltpu

F32 = jnp.float32
BF16 = jnp.bfloat16

D_MODEL = 1024
DEPTH = 2
CHUNK = 64
PLE_DIM = 256
D_FF = 2816
CONV_W = 4
RET_HEADS = 4
RET_DK = 64
RET_W = 256
LRU_W = 384
LRU_BLOCK = 64
LRU_C = 8.0
GDN_HEADS = 6
GDN_DK = 64
GDN_W = 384
IN_WIDTHS = (RET_W, RET_W, RET_W, RET_W, LRU_W, LRU_W, GDN_W, GDN_W, GDN_W, GDN_W, GDN_HEADS, GDN_HEADS)
D_IN = sum(IN_WIDTHS)
ROPE_THETA = 10000.0
ALPHA = (2 * DEPTH) ** 0.25
LN_EPS = 1e-5

LANES = 128
SUBLANES = 8
VMEM_LIMIT_BYTES = 56 * 1024 * 1024

MIX_BLOCK = 256
SUPER = 128
FFN_ROWS = 512
FFN_COLS = 256

C_RQ, C_RK, C_RV, C_RG = 0, 256, 512, 768
C_LX, C_LG = 1024, 1408
C_GQ, C_GK, C_GV, C_GZ = 1792, 2176, 2560, 2944
C_AB = 3328
D_IN_PAD = 3456


def _mm(a, b):
    return jnp.dot(a.astype(BF16), b.astype(BF16), preferred_element_type=F32)


def _mm_nt(a, b):
    return lax.dot_general(a.astype(BF16), b.astype(BF16), (((1,), (1,)), ((), ())),
                           preferred_element_type=F32)


def _mm_tn(a, b):
    return lax.dot_general(a.astype(BF16), b.astype(BF16), (((0,), (0,)), ((), ())),
                           preferred_element_type=F32)


def _split(x, parts):
    out = []
    r = x
    for _ in range(parts):
        t = r.astype(BF16)
        out.append(t)
        r = r - t.astype(F32)
    return out


def _mm_sel(sel, x, parts):
    acc = None
    for t in _split(x, parts):
        y = jnp.dot(sel, t, preferred_element_type=F32)
        acc = y if acc is None else acc + y
    return acc


def _mm_xsel(x, sel, parts):
    acc = None
    for t in _split(x, parts):
        y = jnp.dot(t, sel, preferred_element_type=F32)
        acc = y if acc is None else acc + y
    return acc


def _sigmoid(x):
    return 1.0 / (1.0 + jnp.exp(-x))


def _silu(x):
    return x * _sigmoid(x)


def _softplus(x):
    return jnp.maximum(x, 0.0) + jnp.log1p(jnp.exp(-jnp.abs(x)))


def _gelu_tanh(x):
    c = np.float32(np.sqrt(2.0 / np.pi))
    return 0.5 * x * (1.0 + jnp.tanh(c * (x + 0.044715 * (x * x * x))))


def _layer_norm(r, g, b):
    mu = jnp.mean(r, axis=-1, keepdims=True)
    d = r - mu
    var = jnp.mean(d * d, axis=-1, keepdims=True)
    return d * lax.rsqrt(var + LN_EPS) * g + b


def _rope_body(pos_ref, invf_ref, cos_ref, sin_ref):
    ang = pos_ref[...] * invf_ref[...]
    cos_ref[...] = jnp.cos(ang)
    sin_ref[...] = jnp.sin(ang)


def _rope_tables(positions):
    B, S = positions.shape
    half = RET_DK // 2
    per_row = LANES // half
    T = B * S
    inv_freq = ROPE_THETA ** (-jnp.arange(half, dtype=F32) / half)
    pos_rep = jnp.repeat(positions.astype(F32).reshape(T // per_row, per_row), half, axis=1)
    invf = jnp.tile(inv_freq, per_row).reshape(1, LANES)
    rows = T // per_row
    blk = min(rows, 2048)
    cos, sin = pl.pallas_call(
        _rope_body,
        grid=(rows // blk,),
        in_specs=[pl.BlockSpec((blk, LANES), lambda i: (i, 0)),
                  pl.BlockSpec((1, LANES), lambda i: (0, 0))],
        out_specs=[pl.BlockSpec((blk, LANES), lambda i: (i, 0)),
                   pl.BlockSpec((blk, LANES), lambda i: (i, 0))],
        out_shape=[jax.ShapeDtypeStruct((rows, LANES), F32)] * 2,
        name="rope_tables",
    )(pos_rep, invf)
    cos = jnp.tile(cos.reshape(B, S, half), (1, 1, per_row))
    sin = jnp.tile(sin.reshape(B, S, half), (1, 1, per_row))
    return cos, sin


def _ffn_body(has_ple, *refs):
    if has_ple:
        (x_ref, wg_ref, wu_ref, wd_ref, g_ref, b_ref, p_ref, pwg_ref, pwp_ref, o_ref, h_scr) = refs
    else:
        (x_ref, wg_ref, wu_ref, wd_ref, g_ref, b_ref, o_ref, h_scr) = refs
    x = x_ref[...]
    xb = x.astype(BF16)
    for c in range(0, D_FF, FFN_COLS):
        gate = jnp.dot(xb, wg_ref[:, c:c + FFN_COLS], preferred_element_type=F32)
        up = jnp.dot(xb, wu_ref[:, c:c + FFN_COLS], preferred_element_type=F32)
        h_scr[:, c:c + FFN_COLS] = (_silu(gate) * up).astype(BF16)
    y = jnp.dot(h_scr[...], wd_ref[...], preferred_element_type=F32)
    r = ALPHA * x + 0.5 * y
    if has_ple:
        gate = _sigmoid(jnp.dot(xb, pwg_ref[...], preferred_element_type=F32))
        proj = jnp.dot(p_ref[...].astype(BF16), pwp_ref[...], preferred_element_type=F32)
        r = r + gate * proj
    o_ref[...] = _layer_norm(r, g_ref[...], b_ref[...])


def _const_spec(shape):
    nd = len(shape)
    return pl.BlockSpec(shape, lambda *_: (0,) * nd)


def _ffn(x2, wg, wu, wd, ln_g, ln_b, ple=None):
    T = x2.shape[0]
    rows = min(FFN_ROWS, T)
    tok = lambda n: pl.BlockSpec((rows, n), lambda i: (i, 0))
    args = [x2, wg, wu, wd, ln_g, ln_b]
    specs = [tok(D_MODEL), _const_spec(wg.shape), _const_spec(wu.shape), _const_spec(wd.shape),
             _const_spec(ln_g.shape), _const_spec(ln_b.shape)]
    if ple is not None:
        p2, pwg, pwp = ple
        args += [p2, pwg, pwp]
        specs += [tok(PLE_DIM), _const_spec(pwg.shape), _const_spec(pwp.shape)]
    return pl.pallas_call(
        functools.partial(_ffn_body, ple is not None),
        grid=(T // rows,),
        in_specs=specs,
        out_specs=tok(D_MODEL),
        out_shape=jax.ShapeDtypeStruct((T, D_MODEL), F32),
        scratch_shapes=[pltpu.VMEM((rows, D_FF), BF16)],
        compiler_params=pltpu.CompilerParams(dimension_semantics=("arbitrary",),
                                             vmem_limit_bytes=VMEM_LIMIT_BYTES),
        name="ffn_ple" if ple is not None else "ffn",
    )(*args)


def _iota2(shape, dim):
    return lax.broadcasted_iota(jnp.int32, shape, dim)


def _retention(q_raw, k_raw, v, gate, cos, sin, consts, norm_g, state_ref):
    mk_ref, mv_ref, decay_ref, cross_ref, tail_ref, sdec_ref, bd_ref, avg_ref = consts

    def rot(t):
        t1, t2 = t[:, :LANES], t[:, LANES:]
        return jnp.concatenate([t1 * cos - t2 * sin, t2 * cos + t1 * sin], axis=1)

    q = rot(q_raw) * (RET_DK ** -0.5)
    k = rot(k_raw)
    state = state_ref[...]
    o = _mm(q * cross_ref[...], state)
    kb = k.astype(BF16)
    for h in range(RET_HEADS):
        s = _mm_nt(q * mk_ref[h:h + 1, :], kb) * decay_ref[h]
        o = o + _mm(s, v * mv_ref[h:h + 1, :])
    state_ref[...] = state * sdec_ref[...] + _mm_tn(k * tail_ref[...], v) * bd_ref[...]
    avg = avg_ref[...]
    mu = _mm_xsel(o, avg, 2)
    d = o - mu
    var = _mm_xsel(d * d, avg, 2)
    return d * lax.rsqrt(var + 1e-5) * norm_g * _silu(gate)


def _conv4(buf_ref, cur, w_ref, L):
    buf_ref[SUBLANES:SUBLANES + L, :] = cur
    y = cur * w_ref[CONV_W - 1:CONV_W, :]
    for k in range(CONV_W - 1):
        off = SUBLANES - (CONV_W - 1) + k
        y = y + buf_ref[off:off + L, :] * w_ref[k:k + 1, :]
    buf_ref[0:SUBLANES, :] = buf_ref[L:L + SUBLANES, :]
    return y


def _rg_lru(x_raw, gate, conv_w_ref, conv_b, wax_ref, b_a, b_x, lam, buf_ref, h_ref, L):
    xc = _conv4(buf_ref, x_raw, conv_w_ref, L) + conv_b
    pre_r, pre_i = [], []
    for p in range(LRU_W // LANES):
        y = _mm(xc[:, p * LANES:(p + 1) * LANES], wax_ref[p])
        pre_r.append(y[:, :LANES])
        pre_i.append(y[:, LANES:])
    r = _sigmoid(jnp.concatenate(pre_r, axis=1) + b_a)
    i = _sigmoid(jnp.concatenate(pre_i, axis=1) + b_x)
    log_a = (LRU_C * (-_softplus(-lam))) * r
    a = jnp.exp(log_a)
    t = jnp.tanh(log_a)
    b = jnp.sqrt(-2.0 * t / (1.0 - t)) * (i * xc)
    G = L // SUBLANES
    a3 = a.reshape(G, SUBLANES, LRU_W)
    b3 = b.reshape(G, SUBLANES, LRU_W)
    sub = _iota2((G, SUBLANES, LRU_W), 1)
    for d in (1, 2, 4):
        ok = sub >= d
        a_s = pltpu.roll(a3, d, 1)
        b_s = pltpu.roll(b3, d, 1)
        b3 = b3 + jnp.where(ok, a3 * b_s, 0.0)
        a3 = jnp.where(ok, a3 * a_s, a3)
    a_last = jnp.broadcast_to(a3[:, SUBLANES - 1:SUBLANES, :], a3.shape)
    b_last = jnp.broadcast_to(b3[:, SUBLANES - 1:SUBLANES, :], b3.shape)
    carry = h_ref[...]
    hs = []
    for g in range(G):
        hs.append(b3[g] + a3[g] * carry)
        carry = b_last[g] + a_last[g] * carry
    h_ref[...] = carry
    h = jnp.concatenate(hs, axis=0)
    return h * _gelu_tanh(gate)


def _gated_delta(qkv_raw, z, ab, conv_w_ref, a_log_row, dt_row, norm_g, consts, buf_ref, state_ref, L):
    tri_ref, e_gc_ref, e_beta_ref, bd2_ref, avg2_ref = consts
    W = GDN_W
    qkv = _silu(_conv4(buf_ref, qkv_raw, conv_w_ref, L))
    bd2 = bd2_ref[...]
    avg2 = avg2_ref[...]

    def l2n(t):
        cols = []
        for p in range(W // LANES):
            tp = t[:, p * LANES:(p + 1) * LANES]
            ss = _mm_xsel(tp * tp, bd2, 2)
            cols.append(tp * lax.rsqrt(ss + 1e-6))
        return cols

    q_p = [t * (GDN_DK ** -0.5) for t in l2n(qkv[:, 0:W])]
    k_p = l2n(qkv[:, W:2 * W])
    v_all = qkv[:, 2 * W:3 * W]

    g_n = -jnp.exp(a_log_row) * _softplus(ab + dt_row)
    beta_n = _sigmoid(ab)
    gc_n = _mm_sel(tri_ref[...], g_n, 3)
    gc_h = _mm_xsel(gc_n, e_gc_ref[...], 3)
    beta_pair = _mm_xsel(beta_n, e_beta_ref[...], 2)

    lane = _iota2((SUPER, LANES), 1)
    row = _iota2((SUPER, LANES), 0)
    chunk_bits = CHUNK.bit_length() - 1
    same_chunk = jnp.right_shift(row, chunk_bits) == jnp.right_shift(lane, chunk_bits)
    incl = jnp.logical_and(same_chunk, row >= lane)
    strict = jnp.logical_and(same_chunk, row > lane)
    eye = row == lane
    first_half = lane < CHUNK
    half_masks = (jnp.where(first_half, 1.0, 0.0).astype(F32)[0:1, :],
                  jnp.where(first_half, 0.0, 1.0).astype(F32)[0:1, :])

    o_cols = [[] for _ in range(W // LANES)]
    for sc in range(L // SUPER):
        r0 = sc * SUPER
        rs = slice(r0, r0 + SUPER)
        for p in range(W // LANES):
            kp = k_p[p][rs]
            qp = q_p[p][rs]
            vp = v_all[rs, p * LANES:(p + 1) * LANES]
            bp = beta_pair[rs, p * LANES:(p + 1) * LANES]
            gcA = gc_h[rs, (2 * p) * LANES:(2 * p + 1) * LANES]
            gcB = gc_h[rs, (2 * p + 1) * LANES:(2 * p + 2) * LANES]
            gc_pair = jnp.where(first_half, gcA, gcB)
            eg_pair = jnp.exp(gc_pair)
            kpb = kp.astype(BF16)
            kb_beta = kp * bp
            u_p = None
            w_p = None
            qkd = []
            for hh, gch in ((0, gcA), (1, gcB)):
                m = half_masks[hh]
                rowterm = jnp.sum(jnp.where(eye, gch, 0.0), axis=0, keepdims=True)
                dec = jnp.exp(jnp.minimum(gch - rowterm, 0.0))
                n_mat = jnp.where(strict, _mm_nt(kb_beta * m, kpb) * dec, 0.0)
                qkd.append(jnp.where(incl, _mm_nt(qp * m, kpb) * dec, 0.0))
                x_inv = jnp.where(eye, 1.0, 0.0) - n_mat
                pw = n_mat
                for _ in range(5):
                    pw = _mm(pw, pw)
                    x_inv = x_inv + _mm(x_inv, pw)
                rhs = jnp.concatenate([vp * bp * m, kb_beta * eg_pair * m], axis=1)
                uw = _mm(x_inv, rhs)
                u_p = uw[:, :LANES] if u_p is None else u_p + uw[:, :LANES]
                w_p = uw[:, LANES:] if w_p is None else w_p + uw[:, LANES:]
            q_dec = qp * eg_pair
            for c in range(SUPER // CHUNK):
                cs = slice(c * CHUNK, (c + 1) * CHUNK)
                last = c * CHUNK + CHUNK - 1
                gl_row = gc_pair[last:last + 1, :]
                k_tail = kp[cs] * jnp.exp(gl_row - gc_pair[cs])
                state = state_ref[p]
                v_new = u_p[cs] - _mm(w_p[cs], state)
                o_c = _mm(q_dec[cs], state)
                zeros = jnp.zeros((CHUNK, LANES), F32)
                for hh in range(2):
                    vm = v_new * half_masks[hh]
                    v_full = jnp.concatenate([vm, zeros] if c == 0 else [zeros, vm], axis=0)
                    o_c = o_c + _mm(qkd[hh][cs], v_full)
                state_ref[p] = state * jnp.exp(gl_row) + _mm_tn(k_tail, v_new) * bd2
                o_cols[p].append(o_c)
    o = jnp.concatenate([jnp.concatenate(col, axis=0) for col in o_cols], axis=1)
    outs = []
    for p in range(W // LANES):
        op = o[:, p * LANES:(p + 1) * LANES]
        ms = _mm_xsel(op * op, avg2, 2)
        outs.append(op * lax.rsqrt(ms + 1e-6))
    o = jnp.concatenate(outs, axis=1)
    return o * norm_g * _silu(z)


def _mixer_body(L, x_ref, cos_ref, sin_ref, win_ref, wout_ref,
                mk_ref, mv_ref, decay_ref, cross_ref, tail_ref, sdec_ref, bdr_ref, avg_ref, rng_ref,
                lcw_ref, lcb_ref, wax_ref, lba_ref, lbx_ref, lam_ref,
                gcw_ref, galog_ref, gdt_ref, gng_ref, tri_ref, egc_ref, ebeta_ref, bd2_ref, avg2_ref,
                lng_ref, lnb_ref,
                o_ref, ret_state, gdn_state, lru_h, lru_buf, gdn_buf):
    @pl.when(pl.program_id(1) == 0)
    def _reset():
        ret_state[...] = jnp.zeros_like(ret_state)
        gdn_state[...] = jnp.zeros_like(gdn_state)
        lru_h[...] = jnp.zeros_like(lru_h)
        lru_buf[...] = jnp.zeros_like(lru_buf)
        gdn_buf[...] = jnp.zeros_like(gdn_buf)

    x = x_ref[...]
    xb = x.astype(BF16)

    def proj(lo, hi):
        return jnp.dot(xb, win_ref[:, lo:hi], preferred_element_type=F32)

    o_r = _retention(proj(C_RQ, C_RK), proj(C_RK, C_RV), proj(C_RV, C_RG), proj(C_RG, C_LX),
                     cos_ref[...], sin_ref[...],
                     (mk_ref, mv_ref, decay_ref, cross_ref, tail_ref, sdec_ref, bdr_ref, avg_ref),
                     rng_ref[...], ret_state)
    o_l = _rg_lru(proj(C_LX, C_LG), proj(C_LG, C_GQ), lcw_ref, lcb_ref[...], wax_ref,
                  lba_ref[...], lbx_ref[...], lam_ref[...], lru_buf, lru_h, L)
    o_g = _gated_delta(proj(C_GQ, C_GZ), proj(C_GZ, C_AB), proj(C_AB, D_IN_PAD),
                       gcw_ref, galog_ref[...], gdt_ref[...], gng_ref[...],
                       (tri_ref, egc_ref, ebeta_ref, bd2_ref, avg2_ref), gdn_buf, gdn_state, L)
    mix = (_mm(o_r, wout_ref[0:RET_W, :]) + _mm(o_l, wout_ref[RET_W:RET_W + LRU_W, :])
           + _mm(o_g, wout_ref[RET_W + LRU_W:, :]))
    o_ref[...] = _layer_norm(ALPHA * x + mix, lng_ref[...], lnb_ref[...])


def _mixer_constants(L):
    lg = jnp.log1p(-jnp.exp2(-5.0 - jnp.arange(RET_HEADS, dtype=F32)))
    lane = np.arange(RET_W)
    khead = (lane % LANES) // (RET_DK // 2)
    vhead = lane // RET_DK
    mk = jnp.asarray(np.pad((khead[None, :] == np.arange(RET_HEADS)[:, None]).astype(np.float32), ((0, 4), (0, 0))))
    mv = jnp.asarray(np.pad((vhead[None, :] == np.arange(RET_HEADS)[:, None]).astype(np.float32), ((0, 4), (0, 0))))
    idx = np.arange(L)
    diff = jnp.asarray((idx[:, None] - idx[None, :]).astype(np.float32))
    same = jnp.asarray(idx[:, None] // CHUNK == idx[None, :] // CHUNK)
    earlier = jnp.asarray(idx[None, :] // CHUNK < idx[:, None] // CHUNK)
    decay = jnp.where(same[None], jnp.exp(jnp.abs(diff)[None] * lg[:, None, None]),
                      jnp.where(earlier[None], jnp.exp(diff[None] * lg[:, None, None]), 0.0))
    lg_lane = lg[jnp.asarray(khead)]
    fidx = jnp.asarray(idx.astype(np.float32))
    cross = jnp.exp((fidx + 1.0)[:, None] * lg_lane[None, :])
    tail = jnp.exp((L - 1.0 - fidx)[:, None] * lg_lane[None, :])
    sdec = jnp.broadcast_to(jnp.exp(float(L) * lg_lane)[:, None], (RET_W, RET_W))
    bdr = jnp.asarray((khead[:, None] == vhead[None, :]).astype(np.float32))
    avg = jnp.asarray((vhead[:, None] == vhead[None, :]).astype(np.float32) / RET_DK).astype(BF16)
    tri = jnp.asarray(np.logical_and(idx[:, None] // CHUNK == idx[None, :] // CHUNK,
                                     idx[:, None] >= idx[None, :]).astype(np.float32)).astype(BF16)
    e_gc = np.zeros((LANES, GDN_HEADS * LANES), np.float32)
    e_beta = np.zeros((LANES, GDN_W), np.float32)
    for h in range(GDN_HEADS):
        e_gc[h, h * LANES:(h + 1) * LANES] = 1.0
        e_beta[GDN_HEADS + h, h * GDN_DK:(h + 1) * GDN_DK] = 1.0
    l2 = np.arange(LANES) // GDN_DK
    bd2 = (l2[:, None] == l2[None, :]).astype(np.float32)
    return dict(mk=mk, mv=mv, decay=decay, cross=cross, tail=tail, sdec=sdec, bdr=bdr, avg=avg, tri=tri,
                e_gc=jnp.asarray(e_gc).astype(BF16), e_beta=jnp.asarray(e_beta).astype(BF16),
                bd2=jnp.asarray(bd2), bd2b=jnp.asarray(bd2).astype(BF16),
                avg2=jnp.asarray(bd2 / GDN_DK).astype(BF16))


def _mixer(x, cos, sin, w, consts):
    B, S, _ = x.shape
    L = min(MIX_BLOCK, S)
    tok = lambda n: pl.BlockSpec((None, L, n), lambda b, j: (b, j, 0))
    c = consts
    operands = [
        (x, tok(D_MODEL)), (cos, tok(LANES)), (sin, tok(LANES)),
        (w["w_in"], None), (w["w_out"], None),
        (c["mk"], None), (c["mv"], None), (c["decay"], None), (c["cross"], None), (c["tail"], None),
        (c["sdec"], None), (c["bdr"], None), (c["avg"], None), (w["ret_norm_g"], None),
        (w["lru_conv_w"], None), (w["lru_conv_b"], None), (w["lru_wax"], None), (w["lru_b_a"], None),
        (w["lru_b_x"], None), (w["lru_lambda"], None),
        (w["gdn_conv_w"], None), (w["gdn_a_log"], None), (w["gdn_dt_bias"], None), (w["gdn_norm_g"], None),
        (c["tri"], None), (c["e_gc"], None), (c["e_beta"], None), (c["bd2"], None), (c["avg2"], None),
        (w["ln_g"], None), (w["ln_b"], None),
    ]
    args = [a for a, _ in operands]
    specs = [s if s is not None else _const_spec(a.shape) for a, s in operands]
    return pl.pallas_call(
        functools.partial(_mixer_body, L),
        grid=(B, S // L),
        in_specs=specs,
        out_specs=tok(D_MODEL),
        out_shape=jax.ShapeDtypeStruct((B, S, D_MODEL), F32),
        scratch_shapes=[
            pltpu.VMEM((RET_W, RET_W), F32),
            pltpu.VMEM((GDN_W // LANES, LANES, LANES), F32),
            pltpu.VMEM((SUBLANES, LRU_W), F32),
            pltpu.VMEM((L + SUBLANES, LRU_W), F32),
            pltpu.VMEM((L + SUBLANES, 3 * GDN_W), F32),
        ],
        compiler_params=pltpu.CompilerParams(dimension_semantics=("arbitrary", "arbitrary"),
                                             vmem_limit_bytes=VMEM_LIMIT_BYTES),
        name="mixer",
    )(*args)


def _w_in_columns():
    off = np.concatenate([[0], np.cumsum(IN_WIDTHS)])
    cols = np.full((D_IN_PAD,), D_IN, np.int64)
    lane = np.arange(RET_W)
    rot = ((lane % LANES) // (RET_DK // 2)) * RET_DK + (lane // LANES) * (RET_DK // 2) + lane % (RET_DK // 2)
    cols[C_RQ:C_RQ + RET_W] = off[0] + rot
    cols[C_RK:C_RK + RET_W] = off[1] + rot
    for dst, src, n in ((C_RV, 2, RET_W), (C_RG, 3, RET_W), (C_LX, 4, LRU_W), (C_LG, 5, LRU_W),
                        (C_GQ, 6, GDN_W), (C_GK, 7, GDN_W), (C_GV, 8, GDN_W), (C_GZ, 9, GDN_W)):
        cols[dst:dst + n] = off[src] + np.arange(n)
    cols[C_AB:C_AB + GDN_HEADS] = off[10] + np.arange(GDN_HEADS)
    cols[C_AB + GDN_HEADS:C_AB + 2 * GDN_HEADS] = off[11] + np.arange(GDN_HEADS)
    return cols


def _pair_block_diag(w):
    z = jnp.zeros((LRU_BLOCK, LRU_BLOCK), w.dtype)
    slabs = []
    for p in range(LRU_W // LANES):
        top = jnp.concatenate([w[2 * p], z], axis=1)
        bot = jnp.concatenate([z, w[2 * p + 1]], axis=1)
        slabs.append(jnp.concatenate([top, bot], axis=0))
    return jnp.stack(slabs)


def _pad_rows(a, rows):
    return jnp.pad(a, ((0, rows - a.shape[0]), (0, 0)))


def _head_row(v):
    return jnp.pad(v, (0, LANES - v.shape[0])).reshape(1, LANES)


def kernel(x, p, positions, ln_ffn1_g, ln_ffn1_b, ffn1_w_gate, ffn1_w_up, ffn1_w_down, w_in, ret_norm_g,
           lru_conv_w, lru_conv_b, lru_w_a, lru_b_a, lru_w_x, lru_b_x, lru_lambda, gdn_conv_w, gdn_a_log,
           gdn_dt_bias, gdn_norm_g, w_out, ln_mix_g, ln_mix_b, ffn2_w_gate, ffn2_w_up, ffn2_w_down,
           ple_w_gate, ple_w_proj, ln_ffn2_g, ln_ffn2_b):
    B, S, _ = x.shape
    T = B * S
    row = lambda v: v.reshape(1, -1)
    cos, sin = _rope_tables(positions)
    consts = _mixer_constants(min(MIX_BLOCK, S))
    cols = jnp.asarray(_w_in_columns())
    for i in range(DEPTH):
        x2 = _ffn(x.reshape(T, D_MODEL), ffn1_w_gate[i].astype(BF16), ffn1_w_up[i].astype(BF16),
                  ffn1_w_down[i].astype(BF16), row(ln_ffn1_g[i]), row(ln_ffn1_b[i]))
        w_in_pad = jnp.pad(w_in[i], ((0, 0), (0, 1)))
        mix_w = dict(
            w_in=jnp.take(w_in_pad, cols, axis=1).astype(BF16),
            w_out=w_out[i].astype(BF16),
            ret_norm_g=row(ret_norm_g[i]),
            lru_conv_w=_pad_rows(lru_conv_w[i], SUBLANES), lru_conv_b=row(lru_conv_b[i]),
            lru_wax=jnp.concatenate([_pair_block_diag(lru_w_a[i]), _pair_block_diag(lru_w_x[i])], axis=2).astype(BF16),
            lru_b_a=row(lru_b_a[i]), lru_b_x=row(lru_b_x[i]), lru_lambda=row(lru_lambda[i]),
            gdn_conv_w=_pad_rows(gdn_conv_w[i], SUBLANES),
            gdn_a_log=_head_row(gdn_a_log[i]), gdn_dt_bias=_head_row(gdn_dt_bias[i]),
            gdn_norm_g=row(jnp.tile(gdn_norm_g[i], GDN_HEADS)),
            ln_g=row(ln_mix_g[i]), ln_b=row(ln_mix_b[i]),
        )
        x3 = _mixer(x2.reshape(B, S, D_MODEL), cos, sin, mix_w, consts)
        x = _ffn(x3.reshape(T, D_MODEL), ffn2_w_gate[i].astype(BF16), ffn2_w_up[i].astype(BF16),
                 ffn2_w_down[i].astype(BF16), row(ln_ffn2_g[i]), row(ln_ffn2_b[i]),
                 ple=(p[i].reshape(T, PLE_DIM), ple_w_gate[i].astype(BF16), ple_w_proj[i].astype(BF16)))
        x = x.reshape(B, S, D_MODEL)
    return x
```

```python
import functools

import numpy as np
import jax
import jax.numpy as jnp
from jax import lax
from jax.experimental import pallas as pl
from jax.experimental.pallas import tpu as pltpu

F32 = jnp.float32
BF16 = jnp.bfloat16

D_MODEL = 1024
DEPTH = 2
CHUNK = 64
PLE_DIM = 256
D_FF = 2816
CONV_W = 4
RET_HEADS = 4
RET_DK = 64
RET_W = 256
LRU_W = 384
LRU_BLOCK = 64
LRU_C = 8.0
GDN_HEADS = 6
GDN_DK = 64
GDN_W = 384
IN_WIDTHS = (RET_W, RET_W, RET_W, RET_W, LRU_W, LRU_W, GDN_W, GDN_W, GDN_W, GDN_W, GDN_HEADS, GDN_HEADS)
D_IN = sum(IN_WIDTHS)
ROPE_THETA = 10000.0
ALPHA = (2 * DEPTH) ** 0.25
LN_EPS = 1e-5

LANES = 128
SUBLANES = 8
VMEM_LIMIT_BYTES = 56 * 1024 * 1024

MIX_BLOCK = 256
SUPER = 128
FFN_ROWS = 512
FFN_COLS = 256

C_RQ, C_RK, C_RV, C_RG = 0, 256, 512, 768
C_LX, C_LG = 1024, 1408
C_GQ, C_GK, C_GV, C_GZ = 1792, 2176, 2560, 2944
C_AB = 3328
D_IN_PAD = 3456


def _mm(a, b):
    return jnp.dot(a.astype(BF16), b.astype(BF16), preferred_element_type=F32)


def _mm_nt(a, b):
    return lax.dot_general(a.astype(BF16), b.astype(BF16), (((1,), (1,)), ((), ())),
                           preferred_element_type=F32)


def _mm_tn(a, b):
    return lax.dot_general(a.astype(BF16), b.astype(BF16), (((0,), (0,)), ((), ())),
                           preferred_element_type=F32)


def _bmm(a, b):
    return lax.dot_general(a.astype(BF16), b.astype(BF16), (((2,), (1,)), ((0,), (0,))),
                           preferred_element_type=F32)


def _split(x, parts):
    out = []
    r = x
    for _ in range(parts):
        t = r.astype(BF16)
        out.append(t)
        r = r - t.astype(F32)
    return out


def _mm_sel(sel, x, parts):
    acc = None
    for t in _split(x, parts):
        y = jnp.dot(sel, t, preferred_element_type=F32)
        acc = y if acc is None else acc + y
    return acc


def _mm_xsel(x, sel, parts):
    acc = None
    for t in _split(x, parts):
        y = jnp.dot(t, sel, preferred_element_type=F32)
        acc = y if acc is None else acc + y
    return acc


def _sigmoid(x):
    return 1.0 / (1.0 + jnp.exp(-x))


def _silu(x):
    return x * _sigmoid(x)


def _softplus(x):
    return jnp.maximum(x, 0.0) + jnp.log1p(jnp.exp(-jnp.abs(x)))


def _gelu_tanh(x):
    c = np.float32(np.sqrt(2.0 / np.pi))
    return 0.5 * x * (1.0 + jnp.tanh(c * (x + 0.044715 * (x * x * x))))


def _layer_norm(r, g, b):
    mu = jnp.mean(r, axis=-1, keepdims=True)
    d = r - mu
    var = jnp.mean(d * d, axis=-1, keepdims=True)
    return d * lax.rsqrt(var + LN_EPS) * g + b


def _rope_body(pos_ref, invf_ref, cos_ref, sin_ref):
    ang = pos_ref[...] * invf_ref[...]
    cos_ref[...] = jnp.cos(ang)
    sin_ref[...] = jnp.sin(ang)


def _rope_tables(positions):
    B, S = positions.shape
    half = RET_DK // 2
    per_row = LANES // half
    T = B * S
    inv_freq = ROPE_THETA ** (-jnp.arange(half, dtype=F32) / half)
    pos_rep = jnp.repeat(positions.astype(F32).reshape(T // per_row, per_row), half, axis=1)
    invf = jnp.tile(inv_freq, per_row).reshape(1, LANES)
    rows = T // per_row
    blk = min(rows, 2048)
    cos, sin = pl.pallas_call(
        _rope_body,
        grid=(rows // blk,),
        in_specs=[pl.BlockSpec((blk, LANES), lambda i: (i, 0)),
                  pl.BlockSpec((1, LANES), lambda i: (0, 0))],
        out_specs=[pl.BlockSpec((blk, LANES), lambda i: (i, 0)),
                   pl.BlockSpec((blk, LANES), lambda i: (i, 0))],
        out_shape=[jax.ShapeDtypeStruct((rows, LANES), F32)] * 2,
        name="rope_tables",
    )(pos_rep, invf)
    cos = jnp.tile(cos.reshape(B, S, half), (1, 1, per_row))
    sin = jnp.tile(sin.reshape(B, S, half), (1, 1, per_row))
    return cos, sin


def _ffn_body(has_ple, *refs):
    if has_ple:
        (x_ref, wg_ref, wu_ref, wd_ref, g_ref, b_ref, p_ref, pwg_ref, pwp_ref, o_ref, h_scr) = refs
    else:
        (x_ref, wg_ref, wu_ref, wd_ref, g_ref, b_ref, o_ref, h_scr) = refs
    x = x_ref[...]
    xb = x.astype(BF16)
    for c in range(0, D_FF, FFN_COLS):
        gate = jnp.dot(xb, wg_ref[:, c:c + FFN_COLS], preferred_element_type=F32)
        up = jnp.dot(xb, wu_ref[:, c:c + FFN_COLS], preferred_element_type=F32)
        h_scr[:, c:c + FFN_COLS] = (_silu(gate) * up).astype(BF16)
    y = jnp.dot(h_scr[...], wd_ref[...], preferred_element_type=F32)
    r = ALPHA * x + 0.5 * y
    if has_ple:
        gate = _sigmoid(jnp.dot(xb, pwg_ref[...], preferred_element_type=F32))
        proj = jnp.dot(p_ref[...].astype(BF16), pwp_ref[...], preferred_element_type=F32)
        r = r + gate * proj
    o_ref[...] = _layer_norm(r, g_ref[...], b_ref[...])


def _const_spec(shape):
    nd = len(shape)
    return pl.BlockSpec(shape, lambda *_: (0,) * nd)


def _ffn(x2, wg, wu, wd, ln_g, ln_b, ple=None):
    T = x2.shape[0]
    rows = min(FFN_ROWS, T)
    tok = lambda n: pl.BlockSpec((rows, n), lambda i: (i, 0))
    args = [x2, wg, wu, wd, ln_g, ln_b]
    specs = [tok(D_MODEL), _const_spec(wg.shape), _const_spec(wu.shape), _const_spec(wd.shape),
             _const_spec(ln_g.shape), _const_spec(ln_b.shape)]
    if ple is not None:
        p2, pwg, pwp = ple
        args += [p2, pwg, pwp]
        specs += [tok(PLE_DIM), _const_spec(pwg.shape), _const_spec(pwp.shape)]
    return pl.pallas_call(
        functools.partial(_ffn_body, ple is not None),
        grid=(T // rows,),
        in_specs=specs,
        out_specs=tok(D_MODEL),
        out_shape=jax.ShapeDtypeStruct((T, D_MODEL), F32),
        scratch_shapes=[pltpu.VMEM((rows, D_FF), BF16)],
        compiler_params=pltpu.CompilerParams(dimension_semantics=("arbitrary",),
                                             vmem_limit_bytes=VMEM_LIMIT_BYTES),
        name="ffn_ple" if ple is not None else "ffn",
    )(*args)


def _iota2(shape, dim):
    return lax.broadcasted_iota(jnp.int32, shape, dim)


def _drain(gen):
    for _ in gen:
        pass


def _interleave(main, filler, per_step):
    for _ in main:
        for _ in range(per_step):
            next(filler, None)
    _drain(filler)


def _retention_steps(proj, cos, sin, consts, norm_g, state_ref, out):
    mk_ref, mv_ref, decay_ref, cross_ref, tail_ref, sdec_ref, bd_ref, avg_ref = consts

    def rot(t):
        t1, t2 = t[:, :LANES], t[:, LANES:]
        return jnp.concatenate([t1 * cos - t2 * sin, t2 * cos + t1 * sin], axis=1)

    q = rot(proj(C_RQ, C_RK)) * (RET_DK ** -0.5)
    k = rot(proj(C_RK, C_RV))
    yield
    v = proj(C_RV, C_RG)
    gate = proj(C_RG, C_LX)
    yield
    state = state_ref[...]
    o = _mm(q * cross_ref[...], state)
    state_ref[...] = state * sdec_ref[...] + _mm_tn(k * tail_ref[...], v) * bd_ref[...]
    yield
    kb = k.astype(BF16)
    for h in range(RET_HEADS):
        s = _mm_nt(q * mk_ref[h:h + 1, :], kb) * decay_ref[h]
        o = o + _mm(s, v * mv_ref[h:h + 1, :])
        yield
    avg = avg_ref[...]
    d = o - _mm(o, avg)
    var = _mm(d * d, avg)
    out["ret"] = d * lax.rsqrt(var + 1e-5) * norm_g * _silu(gate)
    yield


def _conv4(buf_ref, cur, w_ref, L):
    buf_ref[SUBLANES:SUBLANES + L, :] = cur
    y = cur * w_ref[CONV_W - 1:CONV_W, :]
    for k in range(CONV_W - 1):
        off = SUBLANES - (CONV_W - 1) + k
        y = y + buf_ref[off:off + L, :] * w_ref[k:k + 1, :]
    buf_ref[0:SUBLANES, :] = buf_ref[L:L + SUBLANES, :]
    return y


def _rg_lru_steps(proj, conv_w_ref, conv_b, wax_ref, b_a, b_x, lam, buf_ref, h_ref, L, out):
    xc = _conv4(buf_ref, proj(C_LX, C_LG), conv_w_ref, L) + conv_b
    gate = proj(C_LG, C_GQ)
    yield
    pre_r, pre_i = [], []
    for p in range(LRU_W // LANES):
        y = _mm(xc[:, p * LANES:(p + 1) * LANES], wax_ref[p])
        pre_r.append(y[:, :LANES])
        pre_i.append(y[:, LANES:])
    yield
    r = _sigmoid(jnp.concatenate(pre_r, axis=1) + b_a)
    i = _sigmoid(jnp.concatenate(pre_i, axis=1) + b_x)
    log_a = (LRU_C * (-_softplus(-lam))) * r
    a = jnp.exp(log_a)
    t = jnp.tanh(log_a)
    b = jnp.sqrt(-2.0 * t / (1.0 - t)) * (i * xc)
    G = L // SUBLANES
    a3 = a.reshape(G, SUBLANES, LRU_W)
    b3 = b.reshape(G, SUBLANES, LRU_W)
    sub = _iota2((G, SUBLANES, LRU_W), 1)
    for d in (1, 2, 4):
        ok = sub >= d
        a_s = pltpu.roll(a3, d, 1)
        b_s = pltpu.roll(b3, d, 1)
        b3 = b3 + jnp.where(ok, a3 * b_s, 0.0)
        a3 = jnp.where(ok, a3 * a_s, a3)
    a_last = jnp.broadcast_to(a3[:, SUBLANES - 1:SUBLANES, :], a3.shape)
    b_last = jnp.broadcast_to(b3[:, SUBLANES - 1:SUBLANES, :], b3.shape)
    carry = h_ref[...]
    hs = []
    for g in range(G):
        hs.append(b3[g] + a3[g] * carry)
        carry = b_last[g] + a_last[g] * carry
    h_ref[...] = carry
    out["lru"] = jnp.concatenate(hs, axis=0) * _gelu_tanh(gate)
    yield


def _gdn_prepare(proj, conv_w_ref, a_log_row, dt_row, consts, buf_ref, L):
    tri_ref, e_gc_ref, e_beta_ref, bd2_ref, avg2_ref = consts
    W = GDN_W
    qkv = _silu(_conv4(buf_ref, proj(C_GQ, C_GZ), conv_w_ref, L))
    ab = proj(C_AB, D_IN_PAD)
    bd2b = bd2_ref[...].astype(BF16)

    def l2n(t):
        cols = []
        for p in range(W // LANES):
            tp = t[:, p * LANES:(p + 1) * LANES]
            cols.append(tp * lax.rsqrt(_mm(tp * tp, bd2b) + 1e-6))
        return cols

    q_p = [t * (GDN_DK ** -0.5) for t in l2n(qkv[:, 0:W])]
    k_p = l2n(qkv[:, W:2 * W])
    v_p = [qkv[:, 2 * W + p * LANES:2 * W + (p + 1) * LANES] for p in range(W // LANES)]
    g_n = -jnp.exp(a_log_row) * _softplus(ab + dt_row)
    gc_n = _mm_sel(tri_ref[...], g_n, 3)
    gc_h = _mm_xsel(gc_n, e_gc_ref[...], 3)
    beta_pair = _mm_xsel(_sigmoid(ab), e_beta_ref[...], 2)
    lane = _iota2((SUPER, LANES), 1)
    row = _iota2((SUPER, LANES), 0)
    chunk_bits = CHUNK.bit_length() - 1
    same_chunk = jnp.right_shift(row, chunk_bits) == jnp.right_shift(lane, chunk_bits)
    first_half = lane < CHUNK
    return dict(
        q=q_p, k=k_p, v=v_p, gc_h=gc_h, beta=beta_pair, bd2=bd2_ref[...],
        incl=jnp.logical_and(same_chunk, row >= lane), strict=jnp.logical_and(same_chunk, row > lane),
        eye=row == lane, first_half=first_half,
        masks=(jnp.where(first_half, 1.0, 0.0).astype(F32)[0:1, :], jnp.where(first_half, 0.0, 1.0).astype(F32)[0:1, :]))


def _gdn_parallel_steps(ctx, sc, out):
    rs = slice(sc * SUPER, (sc + 1) * SUPER)
    n_pairs = GDN_W // LANES
    m_a, m_b = ctx["masks"]
    eye, strict, incl = ctx["eye"], ctx["strict"], ctx["incl"]
    gcs = jnp.stack([ctx["gc_h"][rs, h * LANES:(h + 1) * LANES] for h in range(GDN_HEADS)])
    rowterm = jnp.sum(jnp.where(eye[None], gcs, 0.0), axis=1, keepdims=True)
    dec = jnp.exp(jnp.minimum(gcs - rowterm, 0.0))
    kk, qk, rhs, gc_pair, q_dec, k_rows = [], [], [], [], [], []
    for p in range(n_pairs):
        kp, qp, vp = ctx["k"][p][rs], ctx["q"][p][rs], ctx["v"][p][rs]
        bp = ctx["beta"][rs, p * LANES:(p + 1) * LANES]
        gcp = jnp.where(ctx["first_half"], gcs[2 * p], gcs[2 * p + 1])
        eg = jnp.exp(gcp)
        kbb = kp * bp
        y = _mm_nt(jnp.concatenate([kbb * m_a, kbb * m_b, qp * m_a, qp * m_b], axis=0), kp)
        kk += [y[0:SUPER], y[SUPER:2 * SUPER]]
        qk += [y[2 * SUPER:3 * SUPER], y[3 * SUPER:4 * SUPER]]
        r_p = jnp.concatenate([vp * bp, kbb * eg], axis=1)
        rhs += [r_p * jnp.concatenate([m_a, m_a], axis=1), r_p * jnp.concatenate([m_b, m_b], axis=1)]
        gc_pair.append(gcp)
        q_dec.append(qp * eg)
        k_rows.append(kp)
    yield
    n_mat = jnp.where(strict[None], jnp.stack(kk) * dec, 0.0)
    qkd = jnp.where(incl[None], jnp.stack(qk) * dec, 0.0)
    x_inv = jnp.where(eye, 1.0, 0.0)[None] - n_mat
    pw = _bmm(n_mat, n_mat)
    yield
    for level in range(5):
        if level < 4:
            y = _bmm(jnp.concatenate([pw, x_inv], axis=1), pw)
            pw = y[:, :SUPER]
            x_inv = x_inv + y[:, SUPER:]
        else:
            x_inv = x_inv + _bmm(x_inv, pw)
        yield
    uw = _bmm(x_inv, jnp.stack(rhs))
    out[sc] = dict(
        u=[uw[2 * p, :, :LANES] + uw[2 * p + 1, :, :LANES] for p in range(n_pairs)],
        w=[uw[2 * p, :, LANES:] + uw[2 * p + 1, :, LANES:] for p in range(n_pairs)],
        qkd=qkd, gc_pair=gc_pair, q_dec=q_dec, k=k_rows)
    yield


def _gdn_recurrence_steps(ctx, par, state_ref, o_rows):
    n_pairs = GDN_W // LANES
    m_a, m_b = ctx["masks"]
    bd2 = ctx["bd2"]
    zeros = jnp.zeros((CHUNK, LANES), F32)
    for c in range(SUPER // CHUNK):
        cs = slice(c * CHUNK, (c + 1) * CHUNK)
        last = c * CHUNK + CHUNK - 1
        ys = [_mm(jnp.concatenate([par["w"][p][cs], par["q_dec"][p][cs]], axis=0), state_ref[p])
              for p in range(n_pairs)]
        yield
        for p in range(n_pairs):
            gcp = par["gc_pair"][p]
            gl_row = gcp[last:last + 1, :]
            v_new = par["u"][p][cs] - ys[p][:CHUNK]
            k_tail = par["k"][p][cs] * jnp.exp(gl_row - gcp[cs])
            vm_a, vm_b = v_new * m_a, v_new * m_b
            v_cat = jnp.concatenate([vm_a, zeros, vm_b, zeros] if c == 0 else [zeros, vm_a, zeros, vm_b], axis=0)
            q_cat = jnp.concatenate([par["qkd"][2 * p][cs], par["qkd"][2 * p + 1][cs]], axis=1)
            o_rows[p].append(ys[p][CHUNK:] + _mm(q_cat, v_cat))
            state_ref[p] = state_ref[p] * jnp.exp(gl_row) + _mm_tn(k_tail, v_new) * bd2
        yield


def _mixer_body(L, x_ref, cos_ref, sin_ref, win_ref, wout_ref,
                mk_ref, mv_ref, decay_ref, cross_ref, tail_ref, sdec_ref, bdr_ref, avg_ref, rng_ref,
                lcw_ref, lcb_ref, wax_ref, lba_ref, lbx_ref, lam_ref,
                gcw_ref, galog_ref, gdt_ref, gng_ref, tri_ref, egc_ref, ebeta_ref, bd2_ref, avg2_ref,
                lng_ref, lnb_ref,
                o_ref, ret_state, gdn_state, lru_h, lru_buf, gdn_buf):
    @pl.when(pl.program_id(1) == 0)
    def _reset():
        ret_state[...] = jnp.zeros_like(ret_state)
        gdn_state[...] = jnp.zeros_like(gdn_state)
        lru_h[...] = jnp.zeros_like(lru_h)
        lru_buf[...] = jnp.zeros_like(lru_buf)
        gdn_buf[...] = jnp.zeros_like(gdn_buf)

    x = x_ref[...]
    xb = x.astype(BF16)

    def proj(lo, hi):
        return jnp.dot(xb, win_ref[:, lo:hi], preferred_element_type=F32)

    n_pairs = GDN_W // LANES
    n_super = L // SUPER
    ctx = _gdn_prepare(proj, gcw_ref, galog_ref[...], gdt_ref[...],
                       (tri_ref, egc_ref, ebeta_ref, bd2_ref, avg2_ref), gdn_buf, L)
    par = {}
    heads = {}
    o_rows = [[] for _ in range(n_pairs)]

    def other_heads():
        yield from _retention_steps(proj, cos_ref[...], sin_ref[...],
                                    (mk_ref, mv_ref, decay_ref, cross_ref, tail_ref, sdec_ref, bdr_ref, avg_ref),
                                    rng_ref[...], ret_state, heads)
        yield from _rg_lru_steps(proj, lcw_ref, lcb_ref[...], wax_ref, lba_ref[...], lbx_ref[...], lam_ref[...],
                                 lru_buf, lru_h, L, heads)
        heads["mix"] = _mm(heads["ret"], wout_ref[0:RET_W, :]) + _mm(heads["lru"], wout_ref[RET_W:RET_W + LRU_W, :])
        yield

    _drain(_gdn_parallel_steps(ctx, 0, par))
    for sc in range(n_super):
        last = sc == n_super - 1
        filler = other_heads() if last else _gdn_parallel_steps(ctx, sc + 1, par)
        _interleave(_gdn_recurrence_steps(ctx, par[sc], gdn_state, o_rows), filler, 3)

    avg2 = avg2_ref[...]
    z = proj(C_GZ, C_AB)
    outs = []
    for p in range(n_pairs):
        op = jnp.concatenate(o_rows[p], axis=0)
        outs.append(op * lax.rsqrt(_mm(op * op, avg2) + 1e-6))
    o_g = jnp.concatenate(outs, axis=1) * gng_ref[...] * _silu(z)
    mix = heads["mix"] + _mm(o_g, wout_ref[RET_W + LRU_W:, :])
    o_ref[...] = _layer_norm(ALPHA * x + mix, lng_ref[...], lnb_ref[...])


def _mixer_constants(L):
    lg = jnp.log1p(-jnp.exp2(-5.0 - jnp.arange(RET_HEADS, dtype=F32)))
    lane = np.arange(RET_W)
    khead = (lane % LANES) // (RET_DK // 2)
    vhead = lane // RET_DK
    mk = jnp.asarray(np.pad((khead[None, :] == np.arange(RET_HEADS)[:, None]).astype(np.float32), ((0, 4), (0, 0))))
    mv = jnp.asarray(np.pad((vhead[None, :] == np.arange(RET_HEADS)[:, None]).astype(np.float32), ((0, 4), (0, 0))))
    idx = np.arange(L)
    diff = jnp.asarray((idx[:, None] - idx[None, :]).astype(np.float32))
    same = jnp.asarray(idx[:, None] // CHUNK == idx[None, :] // CHUNK)
    earlier = jnp.asarray(idx[None, :] // CHUNK < idx[:, None] // CHUNK)
    decay = jnp.where(same[None], jnp.exp(jnp.abs(diff)[None] * lg[:, None, None]),
                      jnp.where(earlier[None], jnp.exp(diff[None] * lg[:, None, None]), 0.0))
    lg_lane = lg[jnp.asarray(khead)]
    fidx = jnp.asarray(idx.astype(np.float32))
    cross = jnp.exp((fidx + 1.0)[:, None] * lg_lane[None, :])
    tail = jnp.exp((L - 1.0 - fidx)[:, None] * lg_lane[None, :])
    sdec = jnp.broadcast_to(jnp.exp(float(L) * lg_lane)[:, None], (RET_W, RET_W))
    bdr = jnp.asarray((khead[:, None] == vhead[None, :]).astype(np.float32))
    avg = jnp.asarray((vhead[:, None] == vhead[None, :]).astype(np.float32) / RET_DK).astype(BF16)
    tri = jnp.asarray(np.logical_and(idx[:, None] // CHUNK == idx[None, :] // CHUNK,
                                     idx[:, None] >= idx[None, :]).astype(np.float32)).astype(BF16)
    e_gc = np.zeros((LANES, GDN_HEADS * LANES), np.float32)
    e_beta = np.zeros((LANES, GDN_W), np.float32)
    for h in range(GDN_HEADS):
        e_gc[h, h * LANES:(h + 1) * LANES] = 1.0
        e_beta[GDN_HEADS + h, h * GDN_DK:(h + 1) * GDN_DK] = 1.0
    l2 = np.arange(LANES) // GDN_DK
    bd2 = (l2[:, None] == l2[None, :]).astype(np.float32)
    return dict(mk=mk, mv=mv, decay=decay, cross=cross, tail=tail, sdec=sdec, bdr=bdr, avg=avg, tri=tri,
                e_gc=jnp.asarray(e_gc).astype(BF16), e_beta=jnp.asarray(e_beta).astype(BF16),
                bd2=jnp.asarray(bd2), avg2=jnp.asarray(bd2 / GDN_DK).astype(BF16))


def _mixer(x, cos, sin, w, consts):
    B, S, _ = x.shape
    L = min(MIX_BLOCK, S)
    tok = lambda n: pl.BlockSpec((None, L, n), lambda b, j: (b, j, 0))
    c = consts
    operands = [
        (x, tok(D_MODEL)), (cos, tok(LANES)), (sin, tok(LANES)),
        (w["w_in"], None), (w["w_out"], None),
        (c["mk"], None), (c["mv"], None), (c["decay"], None), (c["cross"], None), (c["tail"], None),
        (c["sdec"], None), (c["bdr"], None), (c["avg"], None), (w["ret_norm_g"], None),
        (w["lru_conv_w"], None), (w["lru_conv_b"], None), (w["lru_wax"], None), (w["lru_b_a"], None),
        (w["lru_b_x"], None), (w["lru_lambda"], None),
        (w["gdn_conv_w"], None), (w["gdn_a_log"], None), (w["gdn_dt_bias"], None), (w["gdn_norm_g"], None),
        (c["tri"], None), (c["e_gc"], None), (c["e_beta"], None), (c["bd2"], None), (c["avg2"], None),
        (w["ln_g"], None), (w["ln_b"], None),
    ]
    args = [a for a, _ in operands]
    specs = [s if s is not None else _const_spec(a.shape) for a, s in operands]
    return pl.pallas_call(
        functools.partial(_mixer_body, L),
        grid=(B, S // L),
        in_specs=specs,
        out_specs=tok(D_MODEL),
        out_shape=jax.ShapeDtypeStruct((B, S, D_MODEL), F32),
        scratch_shapes=[
            pltpu.VMEM((RET_W, RET_W), F32),
            pltpu.VMEM((GDN_W // LANES, LANES, LANES), F32),
            pltpu.VMEM((SUBLANES, LRU_W), F32),
            pltpu.VMEM((L + SUBLANES, LRU_W), F32),
            pltpu.VMEM((L + SUBLANES, 3 * GDN_W), F32),
        ],
        compiler_params=pltpu.CompilerParams(dimension_semantics=("arbitrary", "arbitrary"),
                                             vmem_limit_bytes=VMEM_LIMIT_BYTES),
        name="mixer",
    )(*args)


def _w_in_columns():
    off = np.concatenate([[0], np.cumsum(IN_WIDTHS)])
    cols = np.full((D_IN_PAD,), D_IN, np.int64)
    lane = np.arange(RET_W)
    rot = ((lane % LANES) // (RET_DK // 2)) * RET_DK + (lane // LANES) * (RET_DK // 2) + lane % (RET_DK // 2)
    cols[C_RQ:C_RQ + RET_W] = off[0] + rot
    cols[C_RK:C_RK + RET_W] = off[1] + rot
    for dst, src, n in ((C_RV, 2, RET_W), (C_RG, 3, RET_W), (C_LX, 4, LRU_W), (C_LG, 5, LRU_W),
                        (C_GQ, 6, GDN_W), (C_GK, 7, GDN_W), (C_GV, 8, GDN_W), (C_GZ, 9, GDN_W)):
        cols[dst:dst + n] = off[src] + np.arange(n)
    cols[C_AB:C_AB + GDN_HEADS] = off[10] + np.arange(GDN_HEADS)
    cols[C_AB + GDN_HEADS:C_AB + 2 * GDN_HEADS] = off[11] + np.arange(GDN_HEADS)
    return cols


def _pair_block_diag(w):
    z = jnp.zeros((LRU_BLOCK, LRU_BLOCK), w.dtype)
    slabs = []
    for p in range(LRU_W // LANES):
        top = jnp.concatenate([w[2 * p], z], axis=1)
        bot = jnp.concatenate([z, w[2 * p + 1]], axis=1)
        slabs.append(jnp.concatenate([top, bot], axis=0))
    return jnp.stack(slabs)


def _pad_rows(a, rows):
    return jnp.pad(a, ((0, rows - a.shape[0]), (0, 0)))


def _head_row(v):
    return jnp.pad(v, (0, LANES - v.shape[0])).reshape(1, LANES)


def kernel(x, p, positions, ln_ffn1_g, ln_ffn1_b, ffn1_w_gate, ffn1_w_up, ffn1_w_down, w_in, ret_norm_g,
           lru_conv_w, lru_conv_b, lru_w_a, lru_b_a, lru_w_x, lru_b_x, lru_lambda, gdn_conv_w, gdn_a_log,
           gdn_dt_bias, gdn_norm_g, w_out, ln_mix_g, ln_mix_b, ffn2_w_gate, ffn2_w_up, ffn2_w_down,
           ple_w_gate, ple_w_proj, ln_ffn2_g, ln_ffn2_b):
    B, S, _ = x.shape
    T = B * S
    row = lambda v: v.reshape(1, -1)
    cos, sin = _rope_tables(positions)
    consts = _mixer_constants(min(MIX_BLOCK, S))
    cols = jnp.asarray(_w_in_columns())
    for i in range(DEPTH):
        x2 = _ffn(x.reshape(T, D_MODEL), ffn1_w_gate[i].astype(BF16), ffn1_w_up[i].astype(BF16),
                  ffn1_w_down[i].astype(BF16), row(ln_ffn1_g[i]), row(ln_ffn1_b[i]))
        w_in_pad = jnp.pad(w_in[i], ((0, 0), (0, 1)))
        mix_w = dict(
            w_in=jnp.take(w_in_pad, cols, axis=1).astype(BF16),
            w_out=w_out[i].astype(BF16),
            ret_norm_g=row(ret_norm_g[i]),
            lru_conv_w=_pad_rows(lru_conv_w[i], SUBLANES), lru_conv_b=row(lru_conv_b[i]),
            lru_wax=jnp.concatenate([_pair_block_diag(lru_w_a[i]), _pair_block_diag(lru_w_x[i])], axis=2).astype(BF16),
            lru_b_a=row(lru_b_a[i]), lru_b_x=row(lru_b_x[i]), lru_lambda=row(lru_lambda[i]),
            gdn_conv_w=_pad_rows(gdn_conv_w[i], SUBLANES),
            gdn_a_log=_head_row(gdn_a_log[i]), gdn_dt_bias=_head_row(gdn_dt_bias[i]),
            gdn_norm_g=row(jnp.tile(gdn_norm_g[i], GDN_HEADS)),
            ln_g=row(ln_mix_g[i]), ln_b=row(ln_mix_b[i]),
        )
        x3 = _mixer(x2.reshape(B, S, D_MODEL), cos, sin, mix_w, consts)
        x = _ffn(x3.reshape(T, D_MODEL), ffn2_w_gate[i].astype(BF16), ffn2_w_up[i].astype(BF16),
                 ffn2_w_down[i].astype(BF16), row(ln_ffn2_g[i]), row(ln_ffn2_b[i]),
                 ple=(p[i].reshape(T, PLE_DIM), ple_w_gate[i].astype(BF16), ple_w_proj[i].astype(BF16)))
        x = x.reshape(B, S, D_MODEL)
    return x
```

```python
import functools

import numpy as np
import jax
import jax.numpy as jnp
from jax import lax
from jax.experimental import pallas as pl
from jax.experimental.pallas import tpu as pltpu

F32 = jnp.float32
BF16 = jnp.bfloat16

D_MODEL = 1024
DEPTH = 2
CHUNK = 64
PLE_DIM = 256
D_FF = 2816
CONV_W = 4
RET_HEADS = 4
RET_DK = 64
RET_W = 256
LRU_W = 384
LRU_BLOCK = 64
LRU_C = 8.0
GDN_HEADS = 6
GDN_DK = 64
GDN_W = 384
IN_WIDTHS = (RET_W, RET_W, RET_W, RET_W, LRU_W, LRU_W, GDN_W, GDN_W, GDN_W, GDN_W, GDN_HEADS, GDN_HEADS)
D_IN = sum(IN_WIDTHS)
ROPE_THETA = 10000.0
ALPHA = (2 * DEPTH) ** 0.25
LN_EPS = 1e-5

LANES = 128
SUBLANES = 8
VMEM_LIMIT_BYTES = 56 * 1024 * 1024

MIX_BLOCK = 512
RET_BLOCK = 256
SUPER = 128
FFN_ROWS = 1024
FFN_COLS = 256

C_RQ, C_RK, C_RV, C_RG = 0, 256, 512, 768
C_LX, C_LG = 1024, 1408
C_GQ, C_GK, C_GV, C_GZ = 1792, 2176, 2560, 2944
C_AB = 3328
D_IN_PAD = 3456


def _mm(a, b):
    return jnp.dot(a.astype(BF16), b.astype(BF16), preferred_element_type=F32)


def _mm_nt(a, b):
    return lax.dot_general(a.astype(BF16), b.astype(BF16), (((1,), (1,)), ((), ())),
                           preferred_element_type=F32)


def _mm_tn(a, b):
    return lax.dot_general(a.astype(BF16), b.astype(BF16), (((0,), (0,)), ((), ())),
                           preferred_element_type=F32)


def _bmm(a, b):
    return lax.dot_general(a.astype(BF16), b.astype(BF16), (((2,), (1,)), ((0,), (0,))),
                           preferred_element_type=F32)


def _split(x, parts):
    out = []
    r = x
    for _ in range(parts):
        t = r.astype(BF16)
        out.append(t)
        r = r - t.astype(F32)
    return out


def _mm_sel(sel, x, parts):
    acc = None
    for t in _split(x, parts):
        y = jnp.dot(sel, t, preferred_element_type=F32)
        acc = y if acc is None else acc + y
    return acc


def _mm_xsel(x, sel, parts):
    acc = None
    for t in _split(x, parts):
        y = jnp.dot(t, sel, preferred_element_type=F32)
        acc = y if acc is None else acc + y
    return acc


def _sigmoid(x):
    return 1.0 / (1.0 + jnp.exp(-x))


def _silu(x):
    return x * _sigmoid(x)


def _softplus(x):
    return jnp.maximum(x, 0.0) + jnp.log1p(jnp.exp(-jnp.abs(x)))


def _gelu_tanh(x):
    c = np.float32(np.sqrt(2.0 / np.pi))
    return 0.5 * x * (1.0 + jnp.tanh(c * (x + 0.044715 * (x * x * x))))


def _layer_norm(r, g, b):
    mu = jnp.mean(r, axis=-1, keepdims=True)
    d = r - mu
    var = jnp.mean(d * d, axis=-1, keepdims=True)
    return d * lax.rsqrt(var + LN_EPS) * g + b


def _rope_body(pos_ref, invf_ref, cos_ref, sin_ref):
    ang = pos_ref[...] * invf_ref[...]
    cos_ref[...] = jnp.cos(ang)
    sin_ref[...] = jnp.sin(ang)


def _rope_tables(positions):
    B, S = positions.shape
    half = RET_DK // 2
    per_row = LANES // half
    T = B * S
    inv_freq = ROPE_THETA ** (-jnp.arange(half, dtype=F32) / half)
    pos_rep = jnp.repeat(positions.astype(F32).reshape(T // per_row, per_row), half, axis=1)
    invf = jnp.tile(inv_freq, per_row).reshape(1, LANES)
    rows = T // per_row
    blk = min(rows, 2048)
    cos, sin = pl.pallas_call(
        _rope_body,
        grid=(rows // blk,),
        in_specs=[pl.BlockSpec((blk, LANES), lambda i: (i, 0)),
                  pl.BlockSpec((1, LANES), lambda i: (0, 0))],
        out_specs=[pl.BlockSpec((blk, LANES), lambda i: (i, 0)),
                   pl.BlockSpec((blk, LANES), lambda i: (i, 0))],
        out_shape=[jax.ShapeDtypeStruct((rows, LANES), F32)] * 2,
        name="rope_tables",
    )(pos_rep, invf)
    cos = jnp.tile(cos.reshape(B, S, half), (1, 1, per_row))
    sin = jnp.tile(sin.reshape(B, S, half), (1, 1, per_row))
    return cos, sin


def _ffn_body(has_ple, *refs):
    if has_ple:
        (x_ref, wg_ref, wu_ref, wd_ref, g_ref, b_ref, p_ref, pwg_ref, pwp_ref, o_ref, h_scr) = refs
    else:
        (x_ref, wg_ref, wu_ref, wd_ref, g_ref, b_ref, o_ref, h_scr) = refs
    x = x_ref[...]
    xb = x.astype(BF16)
    for c in range(0, D_FF, FFN_COLS):
        gate = jnp.dot(xb, wg_ref[:, c:c + FFN_COLS], preferred_element_type=F32)
        up = jnp.dot(xb, wu_ref[:, c:c + FFN_COLS], preferred_element_type=F32)
        h_scr[:, c:c + FFN_COLS] = (_silu(gate) * up).astype(BF16)
    y = jnp.dot(h_scr[...], wd_ref[...], preferred_element_type=F32)
    r = ALPHA * x + 0.5 * y
    if has_ple:
        gate = _sigmoid(jnp.dot(xb, pwg_ref[...], preferred_element_type=F32))
        proj = jnp.dot(p_ref[...].astype(BF16), pwp_ref[...], preferred_element_type=F32)
        r = r + gate * proj
    o_ref[...] = _layer_norm(r, g_ref[...], b_ref[...])


def _const_spec(shape):
    nd = len(shape)
    return pl.BlockSpec(shape, lambda *_: (0,) * nd, pipeline_mode=pl.Buffered(1))


def _ffn(x2, wg, wu, wd, ln_g, ln_b, ple=None):
    T = x2.shape[0]
    rows = min(FFN_ROWS, T)
    tok = lambda n: pl.BlockSpec((rows, n), lambda i: (i, 0))
    args = [x2, wg, wu, wd, ln_g, ln_b]
    specs = [tok(D_MODEL), _const_spec(wg.shape), _const_spec(wu.shape), _const_spec(wd.shape),
             _const_spec(ln_g.shape), _const_spec(ln_b.shape)]
    if ple is not None:
        p2, pwg, pwp = ple
        args += [p2, pwg, pwp]
        specs += [tok(PLE_DIM), _const_spec(pwg.shape), _const_spec(pwp.shape)]
    return pl.pallas_call(
        functools.partial(_ffn_body, ple is not None),
        grid=(T // rows,),
        in_specs=specs,
        out_specs=tok(D_MODEL),
        out_shape=jax.ShapeDtypeStruct((T, D_MODEL), F32),
        scratch_shapes=[pltpu.VMEM((rows, D_FF), BF16)],
        compiler_params=pltpu.CompilerParams(dimension_semantics=("arbitrary",),
                                             vmem_limit_bytes=VMEM_LIMIT_BYTES),
        name="ffn_ple" if ple is not None else "ffn",
    )(*args)


def _iota2(shape, dim):
    return lax.broadcasted_iota(jnp.int32, shape, dim)


def _drain(gen):
    for _ in gen:
        pass


def _retention_steps(proj, cos, sin, consts, norm_g, state_ref, out):
    mk_ref, mv_ref, decay_ref, cross_ref, tail_ref, sdec_ref, bd_ref, avg_ref = consts

    def rot(t):
        t1, t2 = t[:, :LANES], t[:, LANES:]
        return jnp.concatenate([t1 * cos - t2 * sin, t2 * cos + t1 * sin], axis=1)

    q_all = rot(proj(C_RQ, C_RK)) * (RET_DK ** -0.5)
    k_all = rot(proj(C_RK, C_RV))
    yield
    v_all = proj(C_RV, C_RG)
    gate = proj(C_RG, C_LX)
    yield
    os = []
    for sb in range(q_all.shape[0] // RET_BLOCK):
        rows = slice(sb * RET_BLOCK, (sb + 1) * RET_BLOCK)
        q, k, v = q_all[rows], k_all[rows], v_all[rows]
        state = state_ref[...]
        o = _mm(q * cross_ref[...], state)
        state_ref[...] = state * sdec_ref[...] + _mm_tn(k * tail_ref[...], v) * bd_ref[...]
        yield
        kb = k.astype(BF16)
        for h in range(RET_HEADS):
            s = _mm_nt(q * mk_ref[h:h + 1, :], kb) * decay_ref[h]
            o = o + _mm(s, v * mv_ref[h:h + 1, :])
            yield
        os.append(o)
    o = jnp.concatenate(os, axis=0)
    avg = avg_ref[...]
    d = o - _mm(o, avg)
    var = _mm(d * d, avg)
    out["ret"] = d * lax.rsqrt(var + 1e-5) * norm_g * _silu(gate)
    yield


def _conv4(buf_ref, cur, w_ref, L):
    buf_ref[SUBLANES:SUBLANES + L, :] = cur
    y = cur * w_ref[CONV_W - 1:CONV_W, :]
    for k in range(CONV_W - 1):
        off = SUBLANES - (CONV_W - 1) + k
        y = y + buf_ref[off:off + L, :] * w_ref[k:k + 1, :]
    buf_ref[0:SUBLANES, :] = buf_ref[L:L + SUBLANES, :]
    return y


def _rg_lru_steps(proj, conv_w_ref, conv_b, wax_ref, b_a, b_x, lam, buf_ref, h_ref, L, out):
    xc = _conv4(buf_ref, proj(C_LX, C_LG), conv_w_ref, L) + conv_b
    gate = proj(C_LG, C_GQ)
    yield
    pre_r, pre_i = [], []
    for p in range(LRU_W // LANES):
        y = _mm(xc[:, p * LANES:(p + 1) * LANES], wax_ref[p])
        pre_r.append(y[:, :LANES])
        pre_i.append(y[:, LANES:])
    yield
    r = _sigmoid(jnp.concatenate(pre_r, axis=1) + b_a)
    i = _sigmoid(jnp.concatenate(pre_i, axis=1) + b_x)
    log_a = (LRU_C * (-_softplus(-lam))) * r
    a = jnp.exp(log_a)
    t = jnp.tanh(log_a)
    b = jnp.sqrt(-2.0 * t / (1.0 - t)) * (i * xc)
    G = L // SUBLANES
    a3 = a.reshape(G, SUBLANES, LRU_W)
    b3 = b.reshape(G, SUBLANES, LRU_W)
    sub = _iota2((G, SUBLANES, LRU_W), 1)
    for d in (1, 2, 4):
        ok = sub >= d
        a_s = pltpu.roll(a3, d, 1)
        b_s = pltpu.roll(b3, d, 1)
        b3 = b3 + jnp.where(ok, a3 * b_s, 0.0)
        a3 = jnp.where(ok, a3 * a_s, a3)
    a_last = jnp.broadcast_to(a3[:, SUBLANES - 1:SUBLANES, :], a3.shape)
    b_last = jnp.broadcast_to(b3[:, SUBLANES - 1:SUBLANES, :], b3.shape)
    carry = h_ref[...]
    hs = []
    for g in range(G):
        hs.append(b3[g] + a3[g] * carry)
        carry = b_last[g] + a_last[g] * carry
    h_ref[...] = carry
    out["lru"] = jnp.concatenate(hs, axis=0) * _gelu_tanh(gate)
    yield


def _gdn_prepare(proj, conv_w_ref, a_log_row, dt_row, consts, buf_ref, L):
    tri_ref, e_gc_ref, e_beta_ref, bd2_ref, avg2_ref = consts
    W = GDN_W
    qkv = _silu(_conv4(buf_ref, proj(C_GQ, C_GZ), conv_w_ref, L))
    ab = proj(C_AB, D_IN_PAD)
    bd2b = bd2_ref[...].astype(BF16)

    def l2n(t):
        cols = []
        for p in range(W // LANES):
            tp = t[:, p * LANES:(p + 1) * LANES]
            cols.append(tp * lax.rsqrt(_mm(tp * tp, bd2b) + 1e-6))
        return cols

    q_p = [t * (GDN_DK ** -0.5) for t in l2n(qkv[:, 0:W])]
    k_p = l2n(qkv[:, W:2 * W])
    v_p = [qkv[:, 2 * W + p * LANES:2 * W + (p + 1) * LANES] for p in range(W // LANES)]
    g_n = -jnp.exp(a_log_row) * _softplus(ab + dt_row)
    tri = tri_ref[...]
    gc_n = jnp.concatenate([_mm_sel(tri, g_n[s * SUPER:(s + 1) * SUPER], 3) for s in range(L // SUPER)], axis=0)
    gc_h = _mm_xsel(gc_n, e_gc_ref[...], 3)
    beta_pair = _mm_xsel(_sigmoid(ab), e_beta_ref[...], 2)
    lane = _iota2((SUPER, LANES), 1)
    row = _iota2((SUPER, LANES), 0)
    chunk_bits = CHUNK.bit_length() - 1
    same_chunk = jnp.right_shift(row, chunk_bits) == jnp.right_shift(lane, chunk_bits)
    first_half = lane < CHUNK
    return dict(
        q=q_p, k=k_p, v=v_p, gc_h=gc_h, beta=beta_pair, bd2=bd2_ref[...],
        incl=jnp.logical_and(same_chunk, row >= lane), strict=jnp.logical_and(same_chunk, row > lane),
        eye=row == lane, first_half=first_half,
        masks=(jnp.where(first_half, 1.0, 0.0).astype(F32)[0:1, :], jnp.where(first_half, 0.0, 1.0).astype(F32)[0:1, :]))


def _gdn_parallel_steps(ctx, sc, out):
    rs = slice(sc * SUPER, (sc + 1) * SUPER)
    n_pairs = GDN_W // LANES
    m_a, m_b = ctx["masks"]
    eye, strict, incl = ctx["eye"], ctx["strict"], ctx["incl"]
    gcs = jnp.stack([ctx["gc_h"][rs, h * LANES:(h + 1) * LANES] for h in range(GDN_HEADS)])
    rowterm = jnp.sum(jnp.where(eye[None], gcs, 0.0), axis=1, keepdims=True)
    dec = jnp.exp(jnp.minimum(gcs - rowterm, 0.0))
    kk, qk, rhs, gc_pair, q_dec, k_rows = [], [], [], [], [], []
    for p in range(n_pairs):
        kp, qp, vp = ctx["k"][p][rs], ctx["q"][p][rs], ctx["v"][p][rs]
        bp = ctx["beta"][rs, p * LANES:(p + 1) * LANES]
        gcp = jnp.where(ctx["first_half"], gcs[2 * p], gcs[2 * p + 1])
        eg = jnp.exp(gcp)
        kbb = kp * bp
        y = _mm_nt(jnp.concatenate([kbb * m_a, kbb * m_b, qp * m_a, qp * m_b], axis=0), kp)
        kk += [y[0:SUPER], y[SUPER:2 * SUPER]]
        qk += [y[2 * SUPER:3 * SUPER], y[3 * SUPER:4 * SUPER]]
        r_p = jnp.concatenate([vp * bp, kbb * eg], axis=1)
        rhs += [r_p * jnp.concatenate([m_a, m_a], axis=1), r_p * jnp.concatenate([m_b, m_b], axis=1)]
        gc_pair.append(gcp)
        q_dec.append(qp * eg)
        k_rows.append(kp)
    yield
    n_mat = jnp.where(strict[None], jnp.stack(kk) * dec, 0.0)
    qkd = jnp.where(incl[None], jnp.stack(qk) * dec, 0.0)
    x_inv = jnp.where(eye, 1.0, 0.0)[None] - n_mat
    pw = _bmm(n_mat, n_mat)
    yield
    for level in range(5):
        if level < 4:
            y = _bmm(jnp.concatenate([pw, x_inv], axis=1), pw)
            pw = y[:, :SUPER]
            x_inv = x_inv + y[:, SUPER:]
        else:
            x_inv = x_inv + _bmm(x_inv, pw)
        yield
    uw = _bmm(x_inv, jnp.stack(rhs))
    out[sc] = dict(
        u=[uw[2 * p, :, :LANES] + uw[2 * p + 1, :, :LANES] for p in range(n_pairs)],
        w=[uw[2 * p, :, LANES:] + uw[2 * p + 1, :, LANES:] for p in range(n_pairs)],
        qkd=qkd, gc_pair=gc_pair, q_dec=q_dec, k=k_rows)
    yield


def _gdn_recurrence_steps(ctx, par, state_ref, o_rows):
    n_pairs = GDN_W // LANES
    m_a, m_b = ctx["masks"]
    bd2 = ctx["bd2"]
    zeros = jnp.zeros((CHUNK, LANES), F32)
    for c in range(SUPER // CHUNK):
        cs = slice(c * CHUNK, (c + 1) * CHUNK)
        last = c * CHUNK + CHUNK - 1
        ys = [_mm(jnp.concatenate([par["w"][p][cs], par["q_dec"][p][cs]], axis=0), state_ref[p])
              for p in range(n_pairs)]
        yield
        for p in range(n_pairs):
            gcp = par["gc_pair"][p]
            gl_row = gcp[last:last + 1, :]
            v_new = par["u"][p][cs] - ys[p][:CHUNK]
            k_tail = par["k"][p][cs] * jnp.exp(gl_row - gcp[cs])
            vm_a, vm_b = v_new * m_a, v_new * m_b
            v_cat = jnp.concatenate([vm_a, zeros, vm_b, zeros] if c == 0 else [zeros, vm_a, zeros, vm_b], axis=0)
            q_cat = jnp.concatenate([par["qkd"][2 * p][cs], par["qkd"][2 * p + 1][cs]], axis=1)
            o_rows[p].append(ys[p][CHUNK:] + _mm(q_cat, v_cat))
            state_ref[p] = state_ref[p] * jnp.exp(gl_row) + _mm_tn(k_tail, v_new) * bd2
        yield


def _mixer_body(L, x_ref, cos_ref, sin_ref, win_ref, wout_ref,
                mk_ref, mv_ref, decay_ref, cross_ref, tail_ref, sdec_ref, bdr_ref, avg_ref, rng_ref,
                lcw_ref, lcb_ref, wax_ref, lba_ref, lbx_ref, lam_ref,
                gcw_ref, galog_ref, gdt_ref, gng_ref, tri_ref, egc_ref, ebeta_ref, bd2_ref, avg2_ref,
                lng_ref, lnb_ref,
                o_ref, ret_state, gdn_state, lru_h, lru_buf, gdn_buf):
    @pl.when(pl.program_id(1) == 0)
    def _reset():
        ret_state[...] = jnp.zeros_like(ret_state)
        gdn_state[...] = jnp.zeros_like(gdn_state)
        lru_h[...] = jnp.zeros_like(lru_h)
        lru_buf[...] = jnp.zeros_like(lru_buf)
        gdn_buf[...] = jnp.zeros_like(gdn_buf)

    x = x_ref[...]
    xb = x.astype(BF16)

    def proj(lo, hi):
        return jnp.dot(xb, win_ref[:, lo:hi], preferred_element_type=F32)

    n_pairs = GDN_W // LANES
    n_super = L // SUPER
    ctx = _gdn_prepare(proj, gcw_ref, galog_ref[...], gdt_ref[...],
                       (tri_ref, egc_ref, ebeta_ref, bd2_ref, avg2_ref), gdn_buf, L)
    par = {}
    heads = {}
    o_rows = [[] for _ in range(n_pairs)]

    def other_heads():
        yield from _retention_steps(proj, cos_ref[...], sin_ref[...],
                                    (mk_ref, mv_ref, decay_ref, cross_ref, tail_ref, sdec_ref, bdr_ref, avg_ref),
                                    rng_ref[...], ret_state, heads)
        yield from _rg_lru_steps(proj, lcw_ref, lcb_ref[...], wax_ref, lba_ref[...], lbx_ref[...], lam_ref[...],
                                 lru_buf, lru_h, L, heads)
        heads["mix"] = _mm(heads["ret"], wout_ref[0:RET_W, :]) + _mm(heads["lru"], wout_ref[RET_W:RET_W + LRU_W, :])
        yield

    others = other_heads()
    _drain(_gdn_parallel_steps(ctx, 0, par))
    for sc in range(n_super):
        nxt = _gdn_parallel_steps(ctx, sc + 1, par) if sc + 1 < n_super else iter(())
        for _ in _gdn_recurrence_steps(ctx, par[sc], gdn_state, o_rows):
            next(nxt, None)
            next(nxt, None)
            next(others, None)
        _drain(nxt)
    _drain(others)

    avg2 = avg2_ref[...]
    z = proj(C_GZ, C_AB)
    outs = []
    for p in range(n_pairs):
        op = jnp.concatenate(o_rows[p], axis=0)
        outs.append(op * lax.rsqrt(_mm(op * op, avg2) + 1e-6))
    o_g = jnp.concatenate(outs, axis=1) * gng_ref[...] * _silu(z)
    mix = heads["mix"] + _mm(o_g, wout_ref[RET_W + LRU_W:, :])
    o_ref[...] = _layer_norm(ALPHA * x + mix, lng_ref[...], lnb_ref[...])


def _mixer_constants():
    L = RET_BLOCK
    lg = jnp.log1p(-jnp.exp2(-5.0 - jnp.arange(RET_HEADS, dtype=F32)))
    lane = np.arange(RET_W)
    khead = (lane % LANES) // (RET_DK // 2)
    vhead = lane // RET_DK
    mk = jnp.asarray(np.pad((khead[None, :] == np.arange(RET_HEADS)[:, None]).astype(np.float32), ((0, 4), (0, 0))))
    mv = jnp.asarray(np.pad((vhead[None, :] == np.arange(RET_HEADS)[:, None]).astype(np.float32), ((0, 4), (0, 0))))
    idx = np.arange(L)
    diff = jnp.asarray((idx[:, None] - idx[None, :]).astype(np.float32))
    same = jnp.asarray(idx[:, None] // CHUNK == idx[None, :] // CHUNK)
    earlier = jnp.asarray(idx[None, :] // CHUNK < idx[:, None] // CHUNK)
    decay = jnp.where(same[None], jnp.exp(jnp.abs(diff)[None] * lg[:, None, None]),
                      jnp.where(earlier[None], jnp.exp(diff[None] * lg[:, None, None]), 0.0))
    lg_lane = lg[jnp.asarray(khead)]
    fidx = jnp.asarray(idx.astype(np.float32))
    cross = jnp.exp((fidx + 1.0)[:, None] * lg_lane[None, :])
    tail = jnp.exp((L - 1.0 - fidx)[:, None] * lg_lane[None, :])
    sdec = jnp.broadcast_to(jnp.exp(float(L) * lg_lane)[:, None], (RET_W, RET_W))
    bdr = jnp.asarray((khead[:, None] == vhead[None, :]).astype(np.float32))
    avg = jnp.asarray((vhead[:, None] == vhead[None, :]).astype(np.float32) / RET_DK).astype(BF16)
    sidx = np.arange(SUPER)
    tri = jnp.asarray(np.logical_and(sidx[:, None] // CHUNK == sidx[None, :] // CHUNK,
                                     sidx[:, None] >= sidx[None, :]).astype(np.float32)).astype(BF16)
    e_gc = np.zeros((LANES, GDN_HEADS * LANES), np.float32)
    e_beta = np.zeros((LANES, GDN_W), np.float32)
    for h in range(GDN_HEADS):
        e_gc[h, h * LANES:(h + 1) * LANES] = 1.0
        e_beta[GDN_HEADS + h, h * GDN_DK:(h + 1) * GDN_DK] = 1.0
    l2 = np.arange(LANES) // GDN_DK
    bd2 = (l2[:, None] == l2[None, :]).astype(np.float32)
    return dict(mk=mk, mv=mv, decay=decay, cross=cross, tail=tail, sdec=sdec, bdr=bdr, avg=avg, tri=tri,
                e_gc=jnp.asarray(e_gc).astype(BF16), e_beta=jnp.asarray(e_beta).astype(BF16),
                bd2=jnp.asarray(bd2), avg2=jnp.asarray(bd2 / GDN_DK).astype(BF16))


def _mixer(x, cos, sin, w, consts):
    B, S, _ = x.shape
    L = min(MIX_BLOCK, S)
    tok = lambda n: pl.BlockSpec((None, L, n), lambda b, j: (b, j, 0))
    c = consts
    operands = [
        (x, tok(D_MODEL)), (cos, tok(LANES)), (sin, tok(LANES)),
        (w["w_in"], None), (w["w_out"], None),
        (c["mk"], None), (c["mv"], None), (c["decay"], None), (c["cross"], None), (c["tail"], None),
        (c["sdec"], None), (c["bdr"], None), (c["avg"], None), (w["ret_norm_g"], None),
        (w["lru_conv_w"], None), (w["lru_conv_b"], None), (w["lru_wax"], None), (w["lru_b_a"], None),
        (w["lru_b_x"], None), (w["lru_lambda"], None),
        (w["gdn_conv_w"], None), (w["gdn_a_log"], None), (w["gdn_dt_bias"], None), (w["gdn_norm_g"], None),
        (c["tri"], None), (c["e_gc"], None), (c["e_beta"], None), (c["bd2"], None), (c["avg2"], None),
        (w["ln_g"], None), (w["ln_b"], None),
    ]
    args = [a for a, _ in operands]
    specs = [s if s is not None else _const_spec(a.shape) for a, s in operands]
    return pl.pallas_call(
        functools.partial(_mixer_body, L),
        grid=(B, S // L),
        in_specs=specs,
        out_specs=tok(D_MODEL),
        out_shape=jax.ShapeDtypeStruct((B, S, D_MODEL), F32),
        scratch_shapes=[
            pltpu.VMEM((RET_W, RET_W), F32),
            pltpu.VMEM((GDN_W // LANES, LANES, LANES), F32),
            pltpu.VMEM((SUBLANES, LRU_W), F32),
            pltpu.VMEM((L + SUBLANES, LRU_W), F32),
            pltpu.VMEM((L + SUBLANES, 3 * GDN_W), F32),
        ],
        compiler_params=pltpu.CompilerParams(dimension_semantics=("arbitrary", "arbitrary"),
                                             vmem_limit_bytes=VMEM_LIMIT_BYTES),
        name="mixer",
    )(*args)


def _w_in_columns():
    off = np.concatenate([[0], np.cumsum(IN_WIDTHS)])
    cols = np.full((D_IN_PAD,), D_IN, np.int64)
    lane = np.arange(RET_W)
    rot = ((lane % LANES) // (RET_DK // 2)) * RET_DK + (lane // LANES) * (RET_DK // 2) + lane % (RET_DK // 2)
    cols[C_RQ:C_RQ + RET_W] = off[0] + rot
    cols[C_RK:C_RK + RET_W] = off[1] + rot
    for dst, src, n in ((C_RV, 2, RET_W), (C_RG, 3, RET_W), (C_LX, 4, LRU_W), (C_LG, 5, LRU_W),
                        (C_GQ, 6, GDN_W), (C_GK, 7, GDN_W), (C_GV, 8, GDN_W), (C_GZ, 9, GDN_W)):
        cols[dst:dst + n] = off[src] + np.arange(n)
    cols[C_AB:C_AB + GDN_HEADS] = off[10] + np.arange(GDN_HEADS)
    cols[C_AB + GDN_HEADS:C_AB + 2 * GDN_HEADS] = off[11] + np.arange(GDN_HEADS)
    return cols


def _pair_block_diag(w):
    z = jnp.zeros((LRU_BLOCK, LRU_BLOCK), w.dtype)
    slabs = []
    for p in range(LRU_W // LANES):
        top = jnp.concatenate([w[2 * p], z], axis=1)
        bot = jnp.concatenate([z, w[2 * p + 1]], axis=1)
        slabs.append(jnp.concatenate([top, bot], axis=0))
    return jnp.stack(slabs)


def _pad_rows(a, rows):
    return jnp.pad(a, ((0, rows - a.shape[0]), (0, 0)))


def _head_row(v):
    return jnp.pad(v, (0, LANES - v.shape[0])).reshape(1, LANES)


def kernel(x, p, positions, ln_ffn1_g, ln_ffn1_b, ffn1_w_gate, ffn1_w_up, ffn1_w_down, w_in, ret_norm_g,
           lru_conv_w, lru_conv_b, lru_w_a, lru_b_a, lru_w_x, lru_b_x, lru_lambda, gdn_conv_w, gdn_a_log,
           gdn_dt_bias, gdn_norm_g, w_out, ln_mix_g, ln_mix_b, ffn2_w_gate, ffn2_w_up, ffn2_w_down,
           ple_w_gate, ple_w_proj, ln_ffn2_g, ln_ffn2_b):
    B, S, _ = x.shape
    T = B * S
    row = lambda v: v.reshape(1, -1)
    cos, sin = _rope_tables(positions)
    consts = _mixer_constants()
    cols = jnp.asarray(_w_in_columns())
    for i in range(DEPTH):
        x2 = _ffn(x.reshape(T, D_MODEL), ffn1_w_gate[i].astype(BF16), ffn1_w_up[i].astype(BF16),
                  ffn1_w_down[i].astype(BF16), row(ln_ffn1_g[i]), row(ln_ffn1_b[i]))
        w_in_pad = jnp.pad(w_in[i], ((0, 0), (0, 1)))
        mix_w = dict(
            w_in=jnp.take(w_in_pad, cols, axis=1).astype(BF16),
            w_out=w_out[i].astype(BF16),
            ret_norm_g=row(ret_norm_g[i]),
            lru_conv_w=_pad_rows(lru_conv_w[i], SUBLANES), lru_conv_b=row(lru_conv_b[i]),
            lru_wax=jnp.concatenate([_pair_block_diag(lru_w_a[i]), _pair_block_diag(lru_w_x[i])], axis=2).astype(BF16),
            lru_b_a=row(lru_b_a[i]), lru_b_x=row(lru_b_x[i]), lru_lambda=row(lru_lambda[i]),
            gdn_conv_w=_pad_rows(gdn_conv_w[i], SUBLANES),
            gdn_a_log=_head_row(gdn_a_log[i]), gdn_dt_bias=_head_row(gdn_dt_bias[i]),
            gdn_norm_g=row(jnp.tile(gdn_norm_g[i], GDN_HEADS)),
            ln_g=row(ln_mix_g[i]), ln_b=row(ln_mix_b[i]),
        )
        x3 = _mixer(x2.reshape(B, S, D_MODEL), cos, sin, mix_w, consts)
        x = _ffn(x3.reshape(T, D_MODEL), ffn2_w_gate[i].astype(BF16), ffn2_w_up[i].astype(BF16),
                 ffn2_w_down[i].astype(BF16), row(ln_ffn2_g[i]), row(ln_ffn2_b[i]),
                 ple=(p[i].reshape(T, PLE_DIM), ple_w_gate[i].astype(BF16), ple_w_proj[i].astype(BF16)))
        x = x.reshape(B, S, D_MODEL)
    return x
```

```python
import functools

import numpy as np
import jax
import jax.numpy as jnp
from jax import lax
from jax.experimental import pallas as pl
from jax.experimental.pallas import tpu as pltpu

F32 = jnp.float32
BF16 = jnp.bfloat16

D_MODEL = 1024
DEPTH = 2
CHUNK = 64
PLE_DIM = 256
D_FF = 2816
CONV_W = 4
RET_HEADS = 4
RET_DK = 64
RET_W = 256
LRU_W = 384
LRU_BLOCK = 64
LRU_C = 8.0
GDN_HEADS = 6
GDN_DK = 64
GDN_W = 384
IN_WIDTHS = (RET_W, RET_W, RET_W, RET_W, LRU_W, LRU_W, GDN_W, GDN_W, GDN_W, GDN_W, GDN_HEADS, GDN_HEADS)
D_IN = sum(IN_WIDTHS)
ROPE_THETA = 10000.0
ALPHA = (2 * DEPTH) ** 0.25
LN_EPS = 1e-5

LANES = 128
SUBLANES = 8
VMEM_LIMIT_BYTES = 56 * 1024 * 1024

MIX_BLOCK = 512
RET_BLOCK = 256
SUPER = 128
FFN_ROWS = 1024
FFN_COLS = 256
ROPE_ROWS = 1024

C_RQ, C_RK, C_RV, C_RG = 0, 256, 512, 768
C_LX, C_LG = 1024, 1408
C_GQ, C_GK, C_GV, C_GZ = 1792, 2176, 2560, 2944
C_AB = 3328
D_IN_PAD = 3456


def _mm(a, b):
    return jnp.dot(a.astype(BF16), b.astype(BF16), preferred_element_type=F32)


def _mm_nt(a, b):
    return lax.dot_general(a.astype(BF16), b.astype(BF16), (((1,), (1,)), ((), ())),
                           preferred_element_type=F32)


def _mm_tn(a, b):
    return lax.dot_general(a.astype(BF16), b.astype(BF16), (((0,), (0,)), ((), ())),
                           preferred_element_type=F32)


def _bmm(a, b):
    return lax.dot_general(a.astype(BF16), b.astype(BF16), (((2,), (1,)), ((0,), (0,))),
                           preferred_element_type=F32)


def _split(x, parts):
    out = []
    r = x
    for _ in range(parts):
        t = r.astype(BF16)
        out.append(t)
        r = r - t.astype(F32)
    return out


def _mm_sel(sel, x, parts):
    acc = None
    for t in _split(x, parts):
        y = jnp.dot(sel, t, preferred_element_type=F32)
        acc = y if acc is None else acc + y
    return acc


def _mm_xsel(x, sel, parts):
    acc = None
    for t in _split(x, parts):
        y = jnp.dot(t, sel, preferred_element_type=F32)
        acc = y if acc is None else acc + y
    return acc


def _sigmoid(x):
    return 1.0 / (1.0 + jnp.exp(-x))


def _silu(x):
    return x * _sigmoid(x)


def _softplus(x):
    return jnp.maximum(x, 0.0) + jnp.log(1.0 + jnp.exp(-jnp.abs(x)))


def _sqrt_nonneg(y):
    return jnp.where(y > 0.0, y * lax.rsqrt(y), 0.0)


def _gelu_tanh(x):
    c = np.float32(np.sqrt(2.0 / np.pi))
    return 0.5 * x * (1.0 + jnp.tanh(c * (x + 0.044715 * (x * x * x))))


def _layer_norm(r, g, b):
    mu = jnp.mean(r, axis=-1, keepdims=True)
    d = r - mu
    var = jnp.mean(d * d, axis=-1, keepdims=True)
    return d * lax.rsqrt(var + LN_EPS) * g + b


def _rope_body(pos_ref, invf_ref, cos_ref, sin_ref):
    half = RET_DK // 2
    per_row = LANES // half
    ang = pos_ref[...] * invf_ref[...]
    rows = ang.shape[0]
    group = jnp.right_shift(lax.broadcasted_iota(jnp.int32, (rows, LANES), 1), half.bit_length() - 1)
    for table, out_ref in ((jnp.cos(ang), cos_ref), (jnp.sin(ang), sin_ref)):
        for j in range(per_row):
            own = jnp.where(group == j, table, 0.0)
            spread = own
            for k in range(1, per_row):
                spread = spread + pltpu.roll(own, k * half, 1)
            out_ref[pl.ds(j, rows, stride=per_row), :] = spread


def _rope_tables(positions):
    B, S = positions.shape
    half = RET_DK // 2
    per_row = LANES // half
    T = B * S
    rows = T // per_row
    inv_freq = ROPE_THETA ** (-jnp.arange(half, dtype=F32) / half)
    pos_rep = jnp.broadcast_to(positions.astype(F32).reshape(rows, per_row, 1), (rows, per_row, half))
    invf = jnp.tile(inv_freq, per_row).reshape(1, LANES)
    blk = min(rows, ROPE_ROWS)
    cos, sin = pl.pallas_call(
        _rope_body,
        grid=(rows // blk,),
        in_specs=[pl.BlockSpec((blk, LANES), lambda i: (i, 0)),
                  pl.BlockSpec((1, LANES), lambda i: (0, 0))],
        out_specs=[pl.BlockSpec((per_row * blk, LANES), lambda i: (i, 0)),
                   pl.BlockSpec((per_row * blk, LANES), lambda i: (i, 0))],
        out_shape=[jax.ShapeDtypeStruct((T, LANES), F32)] * 2,
        name="rope_tables",
    )(pos_rep.reshape(rows, LANES), invf)
    return cos.reshape(B, S, LANES), sin.reshape(B, S, LANES)


def _ffn_body(has_ple, *refs):
    if has_ple:
        (x_ref, wg_ref, wu_ref, wd_ref, g_ref, b_ref, p_ref, pwg_ref, pwp_ref, o_ref, h_scr) = refs
    else:
        (x_ref, wg_ref, wu_ref, wd_ref, g_ref, b_ref, o_ref, h_scr) = refs
    x = x_ref[...]
    xb = x.astype(BF16)
    for c in range(0, D_FF, FFN_COLS):
        gate = jnp.dot(xb, wg_ref[:, c:c + FFN_COLS], preferred_element_type=F32)
        up = jnp.dot(xb, wu_ref[:, c:c + FFN_COLS], preferred_element_type=F32)
        h_scr[:, c:c + FFN_COLS] = (_silu(gate) * up).astype(BF16)
    y = jnp.dot(h_scr[...], wd_ref[...], preferred_element_type=F32)
    r = ALPHA * x + 0.5 * y
    if has_ple:
        gate = _sigmoid(jnp.dot(xb, pwg_ref[...], preferred_element_type=F32))
        proj = jnp.dot(p_ref[...].astype(BF16), pwp_ref[...], preferred_element_type=F32)
        r = r + gate * proj
    o_ref[...] = _layer_norm(r, g_ref[...], b_ref[...])


def _const_spec(shape):
    nd = len(shape)
    return pl.BlockSpec(shape, lambda *_: (0,) * nd, pipeline_mode=pl.Buffered(1))


def _ffn(x2, wg, wu, wd, ln_g, ln_b, ple=None):
    T = x2.shape[0]
    rows = min(FFN_ROWS, T)
    tok = lambda n: pl.BlockSpec((rows, n), lambda i: (i, 0))
    args = [x2, wg, wu, wd, ln_g, ln_b]
    specs = [tok(D_MODEL), _const_spec(wg.shape), _const_spec(wu.shape), _const_spec(wd.shape),
             _const_spec(ln_g.shape), _const_spec(ln_b.shape)]
    if ple is not None:
        p2, pwg, pwp = ple
        args += [p2, pwg, pwp]
        specs += [tok(PLE_DIM), _const_spec(pwg.shape), _const_spec(pwp.shape)]
    return pl.pallas_call(
        functools.partial(_ffn_body, ple is not None),
        grid=(T // rows,),
        in_specs=specs,
        out_specs=tok(D_MODEL),
        out_shape=jax.ShapeDtypeStruct((T, D_MODEL), F32),
        scratch_shapes=[pltpu.VMEM((rows, D_FF), BF16)],
        compiler_params=pltpu.CompilerParams(dimension_semantics=("arbitrary",),
                                             vmem_limit_bytes=VMEM_LIMIT_BYTES),
        name="ffn_ple" if ple is not None else "ffn",
    )(*args)


def _iota2(shape, dim):
    return lax.broadcasted_iota(jnp.int32, shape, dim)


def _drain(gen):
    for _ in gen:
        pass


def _retention_steps(proj, cos, sin, consts, norm_g, state_ref, out):
    mk_ref, mv_ref, decay_ref, cross_ref, tail_ref, sdec_ref, bd_ref, avg_ref = consts

    def rot(t):
        t1, t2 = t[:, :LANES], t[:, LANES:]
        return jnp.concatenate([t1 * cos - t2 * sin, t2 * cos + t1 * sin], axis=1)

    q_all = rot(proj(C_RQ, C_RK)) * (RET_DK ** -0.5)
    k_all = rot(proj(C_RK, C_RV))
    yield
    v_all = proj(C_RV, C_RG)
    gate = proj(C_RG, C_LX)
    yield
    os = []
    for sb in range(q_all.shape[0] // RET_BLOCK):
        rows = slice(sb * RET_BLOCK, (sb + 1) * RET_BLOCK)
        q, k, v = q_all[rows], k_all[rows], v_all[rows]
        state = state_ref[...]
        o = _mm(q * cross_ref[...], state)
        state_ref[...] = state * sdec_ref[...] + _mm_tn(k * tail_ref[...], v) * bd_ref[...]
        yield
        kb = k.astype(BF16)
        for h in range(RET_HEADS):
            s = _mm_nt(q * mk_ref[h:h + 1, :], kb) * decay_ref[h]
            o = o + _mm(s, v * mv_ref[h:h + 1, :])
            yield
        os.append(o)
    o = jnp.concatenate(os, axis=0)
    avg = avg_ref[...]
    d = o - _mm(o, avg)
    var = _mm(d * d, avg)
    out["ret"] = d * lax.rsqrt(var + 1e-5) * norm_g * _silu(gate)
    yield


def _conv_taps(buf_ref, w_ref, r0, n, c0, c1):
    base = SUBLANES + r0
    y = buf_ref[base:base + n, c0:c1] * w_ref[CONV_W - 1:CONV_W, c0:c1]
    for k in range(CONV_W - 1):
        off = base - (CONV_W - 1) + k
        y = y + buf_ref[off:off + n, c0:c1] * w_ref[k:k + 1, c0:c1]
    return y


def _rg_lru_gates(buf_ref, conv_w_ref, conv_b, wax_ref, b_a, b_x, lam, L):
    coeffs = []
    for p in range(LRU_W // LANES):
        c0, c1 = p * LANES, (p + 1) * LANES
        xc = _conv_taps(buf_ref, conv_w_ref, 0, L, c0, c1) + conv_b[:, c0:c1]
        y = _mm(xc, wax_ref[p])
        coeffs.append((xc, y, c0, c1))
    out = []
    for xc, y, c0, c1 in coeffs:
        r = _sigmoid(y[:, :LANES] + b_a[:, c0:c1])
        i = _sigmoid(y[:, LANES:] + b_x[:, c0:c1])
        log_a = (LRU_C * (-_softplus(-lam[:, c0:c1]))) * r
        t = jnp.tanh(log_a)
        out.append((jnp.exp(log_a), _sqrt_nonneg(-2.0 * t / (1.0 - t)) * (i * xc)))
    return out


def _rg_lru_scan(a, b, h_ref, c0, c1, L):
    G = L // SUBLANES
    a3 = a.reshape(G, SUBLANES, LANES)
    b3 = b.reshape(G, SUBLANES, LANES)
    sub = _iota2((G, SUBLANES, LANES), 1)
    for d in (1, 2, 4):
        ok = sub >= d
        a_s = pltpu.roll(a3, d, 1)
        b_s = pltpu.roll(b3, d, 1)
        b3 = b3 + jnp.where(ok, a3 * b_s, 0.0)
        a3 = jnp.where(ok, a3 * a_s, a3)
    a_last = jnp.broadcast_to(a3[:, SUBLANES - 1:SUBLANES, :], a3.shape)
    b_last = jnp.broadcast_to(b3[:, SUBLANES - 1:SUBLANES, :], b3.shape)
    carry = h_ref[:, c0:c1]
    hs = []
    for g in range(G):
        hs.append(b3[g] + a3[g] * carry)
        carry = b_last[g] + a_last[g] * carry
    h_ref[:, c0:c1] = carry
    return jnp.concatenate(hs, axis=0)


def _gdn_gates(ab, a_log_row, dt_row, tri_ref, e_gc_ref, e_beta_ref, L):
    g_n = -jnp.exp(a_log_row) * _softplus(ab + dt_row)
    tri = tri_ref[...]
    gc_n = jnp.concatenate([_mm_sel(tri, g_n[s * SUPER:(s + 1) * SUPER], 2) for s in range(L // SUPER)], axis=0)
    gc_h = _mm_xsel(gc_n, e_gc_ref[...], 2)
    beta_pair = _mm_xsel(_sigmoid(ab), e_beta_ref[...], 2)
    return gc_h, beta_pair


def _gdn_masks():
    lane = _iota2((SUPER, LANES), 1)
    row = _iota2((SUPER, LANES), 0)
    chunk_bits = CHUNK.bit_length() - 1
    same_chunk = jnp.right_shift(row, chunk_bits) == jnp.right_shift(lane, chunk_bits)
    first_half = lane < CHUNK
    return dict(
        incl=jnp.logical_and(same_chunk, row >= lane), strict=jnp.logical_and(same_chunk, row > lane),
        eye=row == lane, first_half=first_half,
        masks=(jnp.where(first_half, 1.0, 0.0).astype(F32)[0:1, :], jnp.where(first_half, 0.0, 1.0).astype(F32)[0:1, :]))


def _gdn_prep(ctx, sc, buf_ref, conv_w_ref):
    W = GDN_W
    qkv = _silu(_conv_taps(buf_ref, conv_w_ref, sc * SUPER, SUPER, 0, 3 * W))
    bd2b = ctx["bd2"].astype(BF16)
    q, k, v = [], [], []
    for p in range(W // LANES):
        qk = jnp.concatenate([qkv[:, p * LANES:(p + 1) * LANES], qkv[:, W + p * LANES:W + (p + 1) * LANES]], axis=0)
        qk = qk * lax.rsqrt(_mm(qk * qk, bd2b) + 1e-6)
        q.append(qk[:SUPER] * (GDN_DK ** -0.5))
        k.append(qk[SUPER:])
        v.append(qkv[:, 2 * W + p * LANES:2 * W + (p + 1) * LANES])
    ctx["qkv"][sc] = (q, k, v)


def _gdn_parallel_steps(ctx, sc, out):
    rs = slice(sc * SUPER, (sc + 1) * SUPER)
    n_pairs = GDN_W // LANES
    m_a, m_b = ctx["masks"]
    eye, strict, incl = ctx["eye"], ctx["strict"], ctx["incl"]
    q_p, k_p, v_p = ctx["qkv"][sc]
    gcs = jnp.stack([ctx["gc_h"][rs, h * LANES:(h + 1) * LANES] for h in range(GDN_HEADS)])
    rowterm = jnp.sum(jnp.where(eye[None], gcs, 0.0), axis=1, keepdims=True)
    dec = jnp.exp(jnp.minimum(gcs - rowterm, 0.0))
    kk, qk, rhs, gc_pair, q_dec = [], [], [], [], []
    for p in range(n_pairs):
        kp, qp, vp = k_p[p], q_p[p], v_p[p]
        bp = ctx["beta"][rs, p * LANES:(p + 1) * LANES]
        gcp = jnp.where(ctx["first_half"], gcs[2 * p], gcs[2 * p + 1])
        eg = jnp.exp(gcp)
        kbb = kp * bp
        y = _mm_nt(jnp.concatenate([kbb * m_a, kbb * m_b, qp * m_a, qp * m_b], axis=0), kp)
        kk += [y[0:SUPER], y[SUPER:2 * SUPER]]
        qk += [y[2 * SUPER:3 * SUPER], y[3 * SUPER:4 * SUPER]]
        r_p = jnp.concatenate([vp * bp, kbb * eg], axis=1)
        rhs += [r_p * jnp.concatenate([m_a, m_a], axis=1), r_p * jnp.concatenate([m_b, m_b], axis=1)]
        gc_pair.append(gcp)
        q_dec.append(qp * eg)
    yield
    n_mat = jnp.where(strict[None], jnp.stack(kk) * dec, 0.0)
    qkd = jnp.where(incl[None], jnp.stack(qk) * dec, 0.0)
    x_inv = jnp.where(eye, 1.0, 0.0)[None] - n_mat
    pw = _bmm(n_mat, n_mat)
    yield
    for level in range(5):
        if level < 4:
            y = _bmm(jnp.concatenate([pw, x_inv], axis=1), pw)
            pw = y[:, :SUPER]
            x_inv = x_inv + y[:, SUPER:]
        else:
            x_inv = x_inv + _bmm(x_inv, pw)
        yield
    uw = _bmm(x_inv, jnp.stack(rhs))
    out[sc] = dict(
        u=[uw[2 * p, :, :LANES] + uw[2 * p + 1, :, :LANES] for p in range(n_pairs)],
        w=[uw[2 * p, :, LANES:] + uw[2 * p + 1, :, LANES:] for p in range(n_pairs)],
        qkd=qkd, gc_pair=gc_pair, q_dec=q_dec, k=k_p)
    yield


def _gdn_recurrence_steps(ctx, par, state_ref, o_rows):
    n_pairs = GDN_W // LANES
    m_a, m_b = ctx["masks"]
    bd2 = ctx["bd2"]
    zeros = jnp.zeros((CHUNK, LANES), F32)
    for c in range(SUPER // CHUNK):
        cs = slice(c * CHUNK, (c + 1) * CHUNK)
        last = c * CHUNK + CHUNK - 1
        ys = [_mm(jnp.concatenate([par["w"][p][cs], par["q_dec"][p][cs]], axis=0), state_ref[p])
              for p in range(n_pairs)]
        yield
        for p in range(n_pairs):
            gcp = par["gc_pair"][p]
            gl_row = gcp[last:last + 1, :]
            v_new = par["u"][p][cs] - ys[p][:CHUNK]
            k_tail = par["k"][p][cs] * jnp.exp(gl_row - gcp[cs])
            vm_a, vm_b = v_new * m_a, v_new * m_b
            v_cat = jnp.concatenate([vm_a, zeros, vm_b, zeros] if c == 0 else [zeros, vm_a, zeros, vm_b], axis=0)
            q_cat = jnp.concatenate([par["qkd"][2 * p][cs], par["qkd"][2 * p + 1][cs]], axis=1)
            o_rows[p].append(ys[p][CHUNK:] + _mm(q_cat, v_cat))
            state_ref[p] = state_ref[p] * jnp.exp(gl_row) + _mm_tn(k_tail, v_new) * bd2
        yield


def _mixer_body(L, x_ref, cos_ref, sin_ref, win_ref, wout_ref,
                mk_ref, mv_ref, decay_ref, cross_ref, tail_ref, sdec_ref, bdr_ref, avg_ref, rng_ref,
                lcw_ref, lcb_ref, wax_ref, lba_ref, lbx_ref, lam_ref,
                gcw_ref, galog_ref, gdt_ref, gng_ref, tri_ref, egc_ref, ebeta_ref, bd2_ref, avg2_ref,
                lng_ref, lnb_ref,
                o_ref, ret_state, gdn_state, lru_h, lru_buf, gdn_buf):
    @pl.when(pl.program_id(1) == 0)
    def _reset():
        ret_state[...] = jnp.zeros_like(ret_state)
        gdn_state[...] = jnp.zeros_like(gdn_state)
        lru_h[...] = jnp.zeros_like(lru_h)
        lru_buf[...] = jnp.zeros_like(lru_buf)
        gdn_buf[...] = jnp.zeros_like(gdn_buf)

    x = x_ref[...]
    xb = x.astype(BF16)

    def proj(lo, hi):
        return jnp.dot(xb, win_ref[:, lo:hi], preferred_element_type=F32)

    n_pairs = GDN_W // LANES
    n_lru = LRU_W // LANES
    n_super = L // SUPER
    gdn_buf[SUBLANES:SUBLANES + L, :] = proj(C_GQ, C_GZ)
    gc_h, beta_pair = _gdn_gates(proj(C_AB, D_IN_PAD), galog_ref[...], gdt_ref[...], tri_ref, egc_ref, ebeta_ref, L)
    ctx = dict(gc_h=gc_h, beta=beta_pair, bd2=bd2_ref[...], qkv={}, **_gdn_masks())
    _gdn_prep(ctx, 0, gdn_buf, gcw_ref)
    lru_buf[SUBLANES:SUBLANES + L, :] = proj(C_LX, C_LG)
    lru_gate = proj(C_LG, C_GQ)
    heads = {}
    ret_gen = _retention_steps(proj, cos_ref[...], sin_ref[...],
                               (mk_ref, mv_ref, decay_ref, cross_ref, tail_ref, sdec_ref, bdr_ref, avg_ref),
                               rng_ref[...], ret_state, heads)
    next(ret_gen)
    next(ret_gen)
    if n_super > 1:
        _gdn_prep(ctx, 1, gdn_buf, gcw_ref)
    lru_ab = _rg_lru_gates(lru_buf, lcw_ref, lcb_ref[...], wax_ref, lba_ref[...], lbx_ref[...], lam_ref[...], L)
    par = {}
    o_rows = [[] for _ in range(n_pairs)]
    lru_cols = []
    _drain(_gdn_parallel_steps(ctx, 0, par))
    for sc in range(n_super):
        if sc + 2 < n_super:
            _gdn_prep(ctx, sc + 2, gdn_buf, gcw_ref)
        if len(lru_cols) < n_lru:
            t = len(lru_cols)
            lru_cols.append(_rg_lru_scan(lru_ab[t][0], lru_ab[t][1], lru_h, t * LANES, (t + 1) * LANES, L))
        nxt = _gdn_parallel_steps(ctx, sc + 1, par) if sc + 1 < n_super else iter(())
        ret_steps = 0 if sc + 2 < n_super else (1 if sc + 1 < n_super else 2)
        for _ in _gdn_recurrence_steps(ctx, par[sc], gdn_state, o_rows):
            next(nxt, None)
            next(nxt, None)
            for _ in range(ret_steps):
                next(ret_gen, None)
        _drain(nxt)
    while len(lru_cols) < n_lru:
        t = len(lru_cols)
        lru_cols.append(_rg_lru_scan(lru_ab[t][0], lru_ab[t][1], lru_h, t * LANES, (t + 1) * LANES, L))
    _drain(ret_gen)
    gdn_buf[0:SUBLANES, :] = gdn_buf[L:L + SUBLANES, :]
    lru_buf[0:SUBLANES, :] = lru_buf[L:L + SUBLANES, :]

    o_l = jnp.concatenate(lru_cols, axis=1) * _gelu_tanh(lru_gate)
    mix = _mm(heads["ret"], wout_ref[0:RET_W, :]) + _mm(o_l, wout_ref[RET_W:RET_W + LRU_W, :])
    avg2 = avg2_ref[...]
    z = proj(C_GZ, C_AB)
    outs = []
    for p in range(n_pairs):
        op = jnp.concatenate(o_rows[p], axis=0)
        outs.append(op * lax.rsqrt(_mm(op * op, avg2) + 1e-6))
    o_g = jnp.concatenate(outs, axis=1) * gng_ref[...] * _silu(z)
    mix = mix + _mm(o_g, wout_ref[RET_W + LRU_W:, :])
    o_ref[...] = _layer_norm(ALPHA * x + mix, lng_ref[...], lnb_ref[...])


def _mixer_constants():
    L = RET_BLOCK
    lg = jnp.log1p(-jnp.exp2(-5.0 - jnp.arange(RET_HEADS, dtype=F32)))
    lane = np.arange(RET_W)
    khead = (lane % LANES) // (RET_DK // 2)
    vhead = lane // RET_DK
    mk = jnp.asarray(np.pad((khead[None, :] == np.arange(RET_HEADS)[:, None]).astype(np.float32), ((0, 4), (0, 0))))
    mv = jnp.asarray(np.pad((vhead[None, :] == np.arange(RET_HEADS)[:, None]).astype(np.float32), ((0, 4), (0, 0))))
    idx = np.arange(L)
    diff = jnp.asarray((idx[:, None] - idx[None, :]).astype(np.float32))
    same = jnp.asarray(idx[:, None] // CHUNK == idx[None, :] // CHUNK)
    earlier = jnp.asarray(idx[None, :] // CHUNK < idx[:, None] // CHUNK)
    decay = jnp.where(same[None], jnp.exp(jnp.abs(diff)[None] * lg[:, None, None]),
                      jnp.where(earlier[None], jnp.exp(diff[None] * lg[:, None, None]), 0.0))
    lg_lane = lg[jnp.asarray(khead)]
    fidx = jnp.asarray(idx.astype(np.float32))
    cross = jnp.exp((fidx + 1.0)[:, None] * lg_lane[None, :])
    tail = jnp.exp((L - 1.0 - fidx)[:, None] * lg_lane[None, :])
    sdec = jnp.broadcast_to(jnp.exp(float(L) * lg_lane)[:, None], (RET_W, RET_W))
    bdr = jnp.asarray((khead[:, None] == vhead[None, :]).astype(np.float32))
    avg = jnp.asarray((vhead[:, None] == vhead[None, :]).astype(np.float32) / RET_DK).astype(BF16)
    sidx = np.arange(SUPER)
    tri = jnp.asarray(np.logical_and(sidx[:, None] // CHUNK == sidx[None, :] // CHUNK,
                                     sidx[:, None] >= sidx[None, :]).astype(np.float32)).astype(BF16)
    e_gc = np.zeros((LANES, GDN_HEADS * LANES), np.float32)
    e_beta = np.zeros((LANES, GDN_W), np.float32)
    for h in range(GDN_HEADS):
        e_gc[h, h * LANES:(h + 1) * LANES] = 1.0
        e_beta[GDN_HEADS + h, h * GDN_DK:(h + 1) * GDN_DK] = 1.0
    l2 = np.arange(LANES) // GDN_DK
    bd2 = (l2[:, None] == l2[None, :]).astype(np.float32)
    return dict(mk=mk, mv=mv, decay=decay, cross=cross, tail=tail, sdec=sdec, bdr=bdr, avg=avg, tri=tri,
                e_gc=jnp.asarray(e_gc).astype(BF16), e_beta=jnp.asarray(e_beta).astype(BF16),
                bd2=jnp.asarray(bd2), avg2=jnp.asarray(bd2 / GDN_DK).astype(BF16))


def _mixer(x, cos, sin, w, consts):
    B, S, _ = x.shape
    L = min(MIX_BLOCK, S)
    tok = lambda n: pl.BlockSpec((None, L, n), lambda b, j: (b, j, 0))
    c = consts
    operands = [
        (x, tok(D_MODEL)), (cos, tok(LANES)), (sin, tok(LANES)),
        (w["w_in"], None), (w["w_out"], None),
        (c["mk"], None), (c["mv"], None), (c["decay"], None), (c["cross"], None), (c["tail"], None),
        (c["sdec"], None), (c["bdr"], None), (c["avg"], None), (w["ret_norm_g"], None),
        (w["lru_conv_w"], None), (w["lru_conv_b"], None), (w["lru_wax"], None), (w["lru_b_a"], None),
        (w["lru_b_x"], None), (w["lru_lambda"], None),
        (w["gdn_conv_w"], None), (w["gdn_a_log"], None), (w["gdn_dt_bias"], None), (w["gdn_norm_g"], None),
        (c["tri"], None), (c["e_gc"], None), (c["e_beta"], None), (c["bd2"], None), (c["avg2"], None),
        (w["ln_g"], None), (w["ln_b"], None),
    ]
    args = [a for a, _ in operands]
    specs = [s if s is not None else _const_spec(a.shape) for a, s in operands]
    return pl.pallas_call(
        functools.partial(_mixer_body, L),
        grid=(B, S // L),
        in_specs=specs,
        out_specs=tok(D_MODEL),
        out_shape=jax.ShapeDtypeStruct((B, S, D_MODEL), F32),
        scratch_shapes=[
            pltpu.VMEM((RET_W, RET_W), F32),
            pltpu.VMEM((GDN_W // LANES, LANES, LANES), F32),
            pltpu.VMEM((SUBLANES, LRU_W), F32),
            pltpu.VMEM((L + SUBLANES, LRU_W), F32),
            pltpu.VMEM((L + SUBLANES, 3 * GDN_W), F32),
        ],
        compiler_params=pltpu.CompilerParams(dimension_semantics=("arbitrary", "arbitrary"),
                                             vmem_limit_bytes=VMEM_LIMIT_BYTES),
        name="mixer",
    )(*args)


def _w_in_columns():
    off = np.concatenate([[0], np.cumsum(IN_WIDTHS)])
    cols = np.full((D_IN_PAD,), D_IN, np.int64)
    lane = np.arange(RET_W)
    rot = ((lane % LANES) // (RET_DK // 2)) * RET_DK + (lane // LANES) * (RET_DK // 2) + lane % (RET_DK // 2)
    cols[C_RQ:C_RQ + RET_W] = off[0] + rot
    cols[C_RK:C_RK + RET_W] = off[1] + rot
    for dst, src, n in ((C_RV, 2, RET_W), (C_RG, 3, RET_W), (C_LX, 4, LRU_W), (C_LG, 5, LRU_W),
                        (C_GQ, 6, GDN_W), (C_GK, 7, GDN_W), (C_GV, 8, GDN_W), (C_GZ, 9, GDN_W)):
        cols[dst:dst + n] = off[src] + np.arange(n)
    cols[C_AB:C_AB + GDN_HEADS] = off[10] + np.arange(GDN_HEADS)
    cols[C_AB + GDN_HEADS:C_AB + 2 * GDN_HEADS] = off[11] + np.arange(GDN_HEADS)
    return cols


def _pair_block_diag(w):
    z = jnp.zeros((LRU_BLOCK, LRU_BLOCK), w.dtype)
    slabs = []
    for p in range(LRU_W // LANES):
        top = jnp.concatenate([w[2 * p], z], axis=1)
        bot = jnp.concatenate([z, w[2 * p + 1]], axis=1)
        slabs.append(jnp.concatenate([top, bot], axis=0))
    return jnp.stack(slabs)


def _pad_rows(a, rows):
    return jnp.pad(a, ((0, rows - a.shape[0]), (0, 0)))


def _head_row(v):
    return jnp.pad(v, (0, LANES - v.shape[0])).reshape(1, LANES)


def kernel(x, p, positions, ln_ffn1_g, ln_ffn1_b, ffn1_w_gate, ffn1_w_up, ffn1_w_down, w_in, ret_norm_g,
           lru_conv_w, lru_conv_b, lru_w_a, lru_b_a, lru_w_x, lru_b_x, lru_lambda, gdn_conv_w, gdn_a_log,
           gdn_dt_bias, gdn_norm_g, w_out, ln_mix_g, ln_mix_b, ffn2_w_gate, ffn2_w_up, ffn2_w_down,
           ple_w_gate, ple_w_proj, ln_ffn2_g, ln_ffn2_b):
    B, S, _ = x.shape
    T = B * S
    row = lambda v: v.reshape(1, -1)
    cos, sin = _rope_tables(positions)
    consts = _mixer_constants()
    cols = jnp.asarray(_w_in_columns())
    for i in range(DEPTH):
        x2 = _ffn(x.reshape(T, D_MODEL), ffn1_w_gate[i].astype(BF16), ffn1_w_up[i].astype(BF16),
                  ffn1_w_down[i].astype(BF16), row(ln_ffn1_g[i]), row(ln_ffn1_b[i]))
        w_in_pad = jnp.pad(w_in[i], ((0, 0), (0, 1)))
        mix_w = dict(
            w_in=jnp.take(w_in_pad, cols, axis=1).astype(BF16),
            w_out=w_out[i].astype(BF16),
            ret_norm_g=row(ret_norm_g[i]),
            lru_conv_w=_pad_rows(lru_conv_w[i], SUBLANES), lru_conv_b=row(lru_conv_b[i]),
            lru_wax=jnp.concatenate([_pair_block_diag(lru_w_a[i]), _pair_block_diag(lru_w_x[i])], axis=2).astype(BF16),
            lru_b_a=row(lru_b_a[i]), lru_b_x=row(lru_b_x[i]), lru_lambda=row(lru_lambda[i]),
            gdn_conv_w=_pad_rows(gdn_conv_w[i], SUBLANES),
            gdn_a_log=_head_row(gdn_a_log[i]), gdn_dt_bias=_head_row(gdn_dt_bias[i]),
            gdn_norm_g=row(jnp.tile(gdn_norm_g[i], GDN_HEADS)),
            ln_g=row(ln_mix_g[i]), ln_b=row(ln_mix_b[i]),
        )
        x3 = _mixer(x2.reshape(B, S, D_MODEL), cos, sin, mix_w, consts)
        x = _ffn(x3.reshape(T, D_MODEL), ffn2_w_gate[i].astype(BF16), ffn2_w_up[i].astype(BF16),
                 ffn2_w_down[i].astype(BF16), row(ln_ffn2_g[i]), row(ln_ffn2_b[i]),
                 ple=(p[i].reshape(T, PLE_DIM), ple_w_gate[i].astype(BF16), ple_w_proj[i].astype(BF16)))
        x = x.reshape(B, S, D_MODEL)
    return x
```

```python
import functools

import numpy as np
import jax
import jax.numpy as jnp
from jax import lax
from jax.experimental import pallas as pl
from jax.experimental.pallas import tpu as pltpu

F32 = jnp.float32
BF16 = jnp.bfloat16

D_MODEL = 1024
DEPTH = 2
CHUNK = 64
PLE_DIM = 256
D_FF = 2816
CONV_W = 4
RET_HEADS = 4
RET_DK = 64
RET_W = 256
LRU_W = 384
LRU_BLOCK = 64
LRU_C = 8.0
GDN_HEADS = 6
GDN_DK = 64
GDN_W = 384
IN_WIDTHS = (RET_W, RET_W, RET_W, RET_W, LRU_W, LRU_W, GDN_W, GDN_W, GDN_W, GDN_W, GDN_HEADS, GDN_HEADS)
D_IN = sum(IN_WIDTHS)
ROPE_THETA = 10000.0
ALPHA = (2 * DEPTH) ** 0.25
LN_EPS = 1e-5

LANES = 128
SUBLANES = 8
VMEM_LIMIT_BYTES = 56 * 1024 * 1024

MIX_BLOCK = 512
RET_BLOCK = 256
SUPER = 128
FFN_ROWS = 1024
FFN_COLS = 256
ROPE_ROWS = 1024

C_RQ, C_RK, C_RV, C_RG = 0, 256, 512, 768
C_LX, C_LG = 1024, 1408
C_GQ, C_GK, C_GV, C_GZ = 1792, 2176, 2560, 2944
C_AB = 3328
D_IN_PAD = 3456


def _mm(a, b):
    return jnp.dot(a.astype(BF16), b.astype(BF16), preferred_element_type=F32)


def _mm_nt(a, b):
    return lax.dot_general(a.astype(BF16), b.astype(BF16), (((1,), (1,)), ((), ())),
                           preferred_element_type=F32)


def _mm_tn(a, b):
    return lax.dot_general(a.astype(BF16), b.astype(BF16), (((0,), (0,)), ((), ())),
                           preferred_element_type=F32)


def _bmm(a, b):
    return lax.dot_general(a.astype(BF16), b.astype(BF16), (((2,), (1,)), ((0,), (0,))),
                           preferred_element_type=F32)


def _split(x, parts):
    out = []
    r = x
    for _ in range(parts):
        t = r.astype(BF16)
        out.append(t)
        r = r - t.astype(F32)
    return out


def _mm_sel(sel, x, parts):
    acc = None
    for t in _split(x, parts):
        y = jnp.dot(sel, t, preferred_element_type=F32)
        acc = y if acc is None else acc + y
    return acc


def _mm_xsel(x, sel, parts):
    acc = None
    for t in _split(x, parts):
        y = jnp.dot(t, sel, preferred_element_type=F32)
        acc = y if acc is None else acc + y
    return acc


def _sigmoid(x):
    return 1.0 / (1.0 + jnp.exp(-x))


def _silu(x):
    return x * _sigmoid(x)


def _softplus(x):
    return jnp.maximum(x, 0.0) + jnp.log(1.0 + jnp.exp(-jnp.abs(x)))


def _sqrt_nonneg(y):
    return jnp.where(y > 0.0, y * lax.rsqrt(y), 0.0)


def _gelu_tanh(x):
    c = np.float32(np.sqrt(2.0 / np.pi))
    return 0.5 * x * (1.0 + jnp.tanh(c * (x + 0.044715 * (x * x * x))))


def _layer_norm(r, g, b):
    mu = jnp.mean(r, axis=-1, keepdims=True)
    d = r - mu
    var = jnp.mean(d * d, axis=-1, keepdims=True)
    return d * lax.rsqrt(var + LN_EPS) * g + b


def _rope_body(pos_ref, invf_ref, cos_ref, sin_ref):
    half = RET_DK // 2
    per_row = LANES // half
    ang = pos_ref[...] * invf_ref[...]
    rows = ang.shape[0]
    group = jnp.right_shift(lax.broadcasted_iota(jnp.int32, (rows, LANES), 1), half.bit_length() - 1)
    for table, out_ref in ((jnp.cos(ang), cos_ref), (jnp.sin(ang), sin_ref)):
        for j in range(per_row):
            own = jnp.where(group == j, table, 0.0)
            spread = own
            for k in range(1, per_row):
                spread = spread + pltpu.roll(own, k * half, 1)
            out_ref[pl.ds(j, rows, stride=per_row), :] = spread


def _rope_tables(positions):
    B, S = positions.shape
    half = RET_DK // 2
    per_row = LANES // half
    T = B * S
    rows = T // per_row
    inv_freq = ROPE_THETA ** (-jnp.arange(half, dtype=F32) / half)
    pos_rep = jnp.broadcast_to(positions.astype(F32).reshape(rows, per_row, 1), (rows, per_row, half))
    invf = jnp.tile(inv_freq, per_row).reshape(1, LANES)
    blk = min(rows, ROPE_ROWS)
    cos, sin = pl.pallas_call(
        _rope_body,
        grid=(rows // blk,),
        in_specs=[pl.BlockSpec((blk, LANES), lambda i: (i, 0)),
                  pl.BlockSpec((1, LANES), lambda i: (0, 0))],
        out_specs=[pl.BlockSpec((per_row * blk, LANES), lambda i: (i, 0)),
                   pl.BlockSpec((per_row * blk, LANES), lambda i: (i, 0))],
        out_shape=[jax.ShapeDtypeStruct((T, LANES), F32)] * 2,
        name="rope_tables",
    )(pos_rep.reshape(rows, LANES), invf)
    return cos.reshape(B, S, LANES), sin.reshape(B, S, LANES)


def _ffn_body(has_ple, *refs):
    if has_ple:
        (x_ref, wg_ref, wu_ref, wd_ref, g_ref, b_ref, p_ref, pwg_ref, pwp_ref, o_ref, h_scr) = refs
    else:
        (x_ref, wg_ref, wu_ref, wd_ref, g_ref, b_ref, o_ref, h_scr) = refs
    x = x_ref[...]
    xb = x.astype(BF16)
    for c in range(0, D_FF, FFN_COLS):
        gate = jnp.dot(xb, wg_ref[:, c:c + FFN_COLS], preferred_element_type=F32)
        up = jnp.dot(xb, wu_ref[:, c:c + FFN_COLS], preferred_element_type=F32)
        h_scr[:, c:c + FFN_COLS] = (_silu(gate) * up).astype(BF16)
    y = jnp.dot(h_scr[...], wd_ref[...], preferred_element_type=F32)
    r = ALPHA * x + 0.5 * y
    if has_ple:
        gate = _sigmoid(jnp.dot(xb, pwg_ref[...], preferred_element_type=F32))
        proj = jnp.dot(p_ref[...].astype(BF16), pwp_ref[...], preferred_element_type=F32)
        r = r + gate * proj
    o_ref[...] = _layer_norm(r, g_ref[...], b_ref[...])


def _const_spec(shape):
    nd = len(shape)
    return pl.BlockSpec(shape, lambda *_: (0,) * nd, pipeline_mode=pl.Buffered(1))


def _ffn(x2, wg, wu, wd, ln_g, ln_b, ple=None):
    T = x2.shape[0]
    rows = min(FFN_ROWS, T)
    tok = lambda n: pl.BlockSpec((rows, n), lambda i: (i, 0))
    args = [x2, wg, wu, wd, ln_g, ln_b]
    specs = [tok(D_MODEL), _const_spec(wg.shape), _const_spec(wu.shape), _const_spec(wd.shape),
             _const_spec(ln_g.shape), _const_spec(ln_b.shape)]
    if ple is not None:
        p_all, layer, pwg, pwp = ple
        args += [p_all, pwg, pwp]
        specs += [pl.BlockSpec((None, rows, PLE_DIM), lambda i: (layer, i, 0)),
                  _const_spec(pwg.shape), _const_spec(pwp.shape)]
    return pl.pallas_call(
        functools.partial(_ffn_body, ple is not None),
        grid=(T // rows,),
        in_specs=specs,
        out_specs=tok(D_MODEL),
        out_shape=jax.ShapeDtypeStruct((T, D_MODEL), F32),
        scratch_shapes=[pltpu.VMEM((rows, D_FF), BF16)],
        compiler_params=pltpu.CompilerParams(dimension_semantics=("arbitrary",),
                                             vmem_limit_bytes=VMEM_LIMIT_BYTES),
        name="ffn_ple" if ple is not None else "ffn",
    )(*args)


def _iota2(shape, dim):
    return lax.broadcasted_iota(jnp.int32, shape, dim)


def _drain(gen):
    for _ in gen:
        pass


def _retention_steps(proj, cos, sin, consts, norm_g, state_ref, out):
    mk_ref, mv_ref, decay_ref, cross_ref, tail_ref, sdec_ref, bd_ref, avg_ref = consts

    def rot(t):
        t1, t2 = t[:, :LANES], t[:, LANES:]
        return jnp.concatenate([t1 * cos - t2 * sin, t2 * cos + t1 * sin], axis=1)

    q_raw = proj(C_RQ, C_RK)
    k_raw = proj(C_RK, C_RV)
    yield
    v_all = proj(C_RV, C_RG)
    gate = proj(C_RG, C_LX)
    yield
    q_all = rot(q_raw) * (RET_DK ** -0.5)
    k_all = rot(k_raw)
    os = []
    for sb in range(q_all.shape[0] // RET_BLOCK):
        rows = slice(sb * RET_BLOCK, (sb + 1) * RET_BLOCK)
        q, k, v = q_all[rows], k_all[rows], v_all[rows]
        state = state_ref[...]
        o = _mm(q * cross_ref[...], state)
        state_ref[...] = state * sdec_ref[...] + _mm_tn(k * tail_ref[...], v) * bd_ref[...]
        yield
        kb = k.astype(BF16)
        for h in range(RET_HEADS):
            s = _mm_nt(q * mk_ref[h:h + 1, :], kb) * decay_ref[h]
            o = o + _mm(s, v * mv_ref[h:h + 1, :])
            yield
        os.append(o)
    o = jnp.concatenate(os, axis=0)
    avg = avg_ref[...]
    d = o - _mm(o, avg)
    var = _mm(d * d, avg)
    out["ret"] = d * lax.rsqrt(var + 1e-5) * norm_g * _silu(gate)
    yield


def _conv_taps(buf_ref, w_ref, r0, n, c0, c1):
    base = SUBLANES + r0
    y = buf_ref[base:base + n, c0:c1] * w_ref[CONV_W - 1:CONV_W, c0:c1]
    for k in range(CONV_W - 1):
        off = base - (CONV_W - 1) + k
        y = y + buf_ref[off:off + n, c0:c1] * w_ref[k:k + 1, c0:c1]
    return y


def _rg_lru_conv(buf_ref, conv_w_ref, conv_b, wax_ref, L):
    coeffs = []
    for p in range(LRU_W // LANES):
        c0, c1 = p * LANES, (p + 1) * LANES
        xc = _conv_taps(buf_ref, conv_w_ref, 0, L, c0, c1) + conv_b[:, c0:c1]
        y = _mm(xc, wax_ref[p])
        coeffs.append((xc, y, c0, c1))
    return coeffs


def _rg_lru_gates(coeffs, b_a, b_x, lam):
    out = []
    for xc, y, c0, c1 in coeffs:
        r = _sigmoid(y[:, :LANES] + b_a[:, c0:c1])
        i = _sigmoid(y[:, LANES:] + b_x[:, c0:c1])
        log_a = (LRU_C * (-_softplus(-lam[:, c0:c1]))) * r
        t = jnp.tanh(log_a)
        out.append((jnp.exp(log_a), _sqrt_nonneg(-2.0 * t / (1.0 - t)) * (i * xc)))
    return out


def _rg_lru_scan(a, b, h_ref, c0, c1, L):
    G = L // SUBLANES
    a3 = a.reshape(G, SUBLANES, LANES)
    b3 = b.reshape(G, SUBLANES, LANES)
    sub = _iota2((G, SUBLANES, LANES), 1)
    for d in (1, 2, 4):
        ok = sub >= d
        a_s = pltpu.roll(a3, d, 1)
        b_s = pltpu.roll(b3, d, 1)
        b3 = b3 + jnp.where(ok, a3 * b_s, 0.0)
        a3 = jnp.where(ok, a3 * a_s, a3)
    a_last = jnp.broadcast_to(a3[:, SUBLANES - 1:SUBLANES, :], a3.shape)
    b_last = jnp.broadcast_to(b3[:, SUBLANES - 1:SUBLANES, :], b3.shape)
    carry = h_ref[:, c0:c1]
    hs = []
    for g in range(G):
        hs.append(b3[g] + a3[g] * carry)
        carry = b_last[g] + a_last[g] * carry
    h_ref[:, c0:c1] = carry
    return jnp.concatenate(hs, axis=0)


def _gdn_gates(ab, a_log_row, dt_row, tri_ref, e_gc_ref, e_beta_ref, L):
    g_n = -jnp.exp(a_log_row) * _softplus(ab + dt_row)
    tri = tri_ref[...]
    gc_n = jnp.concatenate([_mm_sel(tri, g_n[s * SUPER:(s + 1) * SUPER], 2) for s in range(L // SUPER)], axis=0)
    gc_h = _mm_xsel(gc_n, e_gc_ref[...], 2)
    beta_pair = _mm_xsel(_sigmoid(ab), e_beta_ref[...], 2)
    return gc_h, beta_pair


def _gdn_masks():
    lane = _iota2((SUPER, LANES), 1)
    row = _iota2((SUPER, LANES), 0)
    chunk_bits = CHUNK.bit_length() - 1
    same_chunk = jnp.right_shift(row, chunk_bits) == jnp.right_shift(lane, chunk_bits)
    first_half = lane < CHUNK
    return dict(
        incl=jnp.logical_and(same_chunk, row >= lane), strict=jnp.logical_and(same_chunk, row > lane),
        eye=row == lane, first_half=first_half,
        masks=(jnp.where(first_half, 1.0, 0.0).astype(F32)[0:1, :], jnp.where(first_half, 0.0, 1.0).astype(F32)[0:1, :]))


def _gdn_prep(ctx, sc, buf_ref, conv_w_ref):
    W = GDN_W
    qkv = _silu(_conv_taps(buf_ref, conv_w_ref, sc * SUPER, SUPER, 0, 3 * W))
    bd2b = ctx["bd2"].astype(BF16)
    q, k, v = [], [], []
    for p in range(W // LANES):
        qk = jnp.concatenate([qkv[:, p * LANES:(p + 1) * LANES], qkv[:, W + p * LANES:W + (p + 1) * LANES]], axis=0)
        qk = qk * lax.rsqrt(_mm(qk * qk, bd2b) + 1e-6)
        q.append(qk[:SUPER] * (GDN_DK ** -0.5))
        k.append(qk[SUPER:])
        v.append(qkv[:, 2 * W + p * LANES:2 * W + (p + 1) * LANES])
    ctx["qkv"][sc] = (q, k, v)


def _gdn_parallel_steps(ctx, sc, out):
    rs = slice(sc * SUPER, (sc + 1) * SUPER)
    n_pairs = GDN_W // LANES
    m_a, m_b = ctx["masks"]
    eye, strict, incl = ctx["eye"], ctx["strict"], ctx["incl"]
    q_p, k_p, v_p = ctx["qkv"][sc]
    gcs = jnp.stack([ctx["gc_h"][rs, h * LANES:(h + 1) * LANES] for h in range(GDN_HEADS)])
    rowterm = jnp.sum(jnp.where(eye[None], gcs, 0.0), axis=1, keepdims=True)
    dec = jnp.exp(jnp.minimum(gcs - rowterm, 0.0))
    kk, qk, rhs, gc_pair, q_dec = [], [], [], [], []
    for p in range(n_pairs):
        kp, qp, vp = k_p[p], q_p[p], v_p[p]
        bp = ctx["beta"][rs, p * LANES:(p + 1) * LANES]
        gcp = jnp.where(ctx["first_half"], gcs[2 * p], gcs[2 * p + 1])
        eg = jnp.exp(gcp)
        kbb = kp * bp
        y = _mm_nt(jnp.concatenate([kbb * m_a, kbb * m_b, qp * m_a, qp * m_b], axis=0), kp)
        kk += [y[0:SUPER], y[SUPER:2 * SUPER]]
        qk += [y[2 * SUPER:3 * SUPER], y[3 * SUPER:4 * SUPER]]
        r_p = jnp.concatenate([vp * bp, kbb * eg], axis=1)
        rhs += [r_p * jnp.concatenate([m_a, m_a], axis=1), r_p * jnp.concatenate([m_b, m_b], axis=1)]
        gc_pair.append(gcp)
        q_dec.append(qp * eg)
    yield
    n_mat = jnp.where(strict[None], jnp.stack(kk) * dec, 0.0)
    qkd = jnp.where(incl[None], jnp.stack(qk) * dec, 0.0)
    x_inv = jnp.where(eye, 1.0, 0.0)[None] - n_mat
    pw = _bmm(n_mat, n_mat)
    yield
    for level in range(5):
        if level < 4:
            y = _bmm(jnp.concatenate([pw, x_inv], axis=1), pw)
            pw = y[:, :SUPER]
            x_inv = x_inv + y[:, SUPER:]
        else:
            x_inv = x_inv + _bmm(x_inv, pw)
        yield
    uw = _bmm(x_inv, jnp.stack(rhs))
    out[sc] = dict(
        u=[uw[2 * p, :, :LANES] + uw[2 * p + 1, :, :LANES] for p in range(n_pairs)],
        w=[uw[2 * p, :, LANES:] + uw[2 * p + 1, :, LANES:] for p in range(n_pairs)],
        qkd=qkd, gc_pair=gc_pair, q_dec=q_dec, k=k_p)
    yield


def _gdn_recurrence_steps(ctx, par, state_ref, o_rows):
    n_pairs = GDN_W // LANES
    m_a, m_b = ctx["masks"]
    bd2 = ctx["bd2"]
    zeros = jnp.zeros((CHUNK, LANES), F32)
    for c in range(SUPER // CHUNK):
        cs = slice(c * CHUNK, (c + 1) * CHUNK)
        last = c * CHUNK + CHUNK - 1
        ys = [_mm(jnp.concatenate([par["w"][p][cs], par["q_dec"][p][cs]], axis=0), state_ref[p])
              for p in range(n_pairs)]
        yield
        for p in range(n_pairs):
            gcp = par["gc_pair"][p]
            gl_row = gcp[last:last + 1, :]
            v_new = par["u"][p][cs] - ys[p][:CHUNK]
            k_tail = par["k"][p][cs] * jnp.exp(gl_row - gcp[cs])
            vm_a, vm_b = v_new * m_a, v_new * m_b
            v_cat = jnp.concatenate([vm_a, zeros, vm_b, zeros] if c == 0 else [zeros, vm_a, zeros, vm_b], axis=0)
            q_cat = jnp.concatenate([par["qkd"][2 * p][cs], par["qkd"][2 * p + 1][cs]], axis=1)
            o_rows[p].append(ys[p][CHUNK:] + _mm(q_cat, v_cat))
            state_ref[p] = state_ref[p] * jnp.exp(gl_row) + _mm_tn(k_tail, v_new) * bd2
        yield


def _mixer_body(L, x_ref, cos_ref, sin_ref, win_ref, wout_ref,
                mk_ref, mv_ref, decay_ref, cross_ref, tail_ref, sdec_ref, bdr_ref, avg_ref, rng_ref,
                lcw_ref, lcb_ref, wax_ref, lba_ref, lbx_ref, lam_ref,
                gcw_ref, galog_ref, gdt_ref, gng_ref, tri_ref, egc_ref, ebeta_ref, bd2_ref, avg2_ref,
                lng_ref, lnb_ref,
                o_ref, ret_state, gdn_state, lru_h, lru_buf, gdn_buf):
    @pl.when(pl.program_id(1) == 0)
    def _reset():
        ret_state[...] = jnp.zeros_like(ret_state)
        gdn_state[...] = jnp.zeros_like(gdn_state)
        lru_h[...] = jnp.zeros_like(lru_h)
        lru_buf[...] = jnp.zeros_like(lru_buf)
        gdn_buf[...] = jnp.zeros_like(gdn_buf)

    x = x_ref[...]
    xb = x.astype(BF16)

    def proj(lo, hi):
        return jnp.dot(xb, win_ref[:, lo:hi], preferred_element_type=F32)

    n_pairs = GDN_W // LANES
    n_lru = LRU_W // LANES
    n_super = L // SUPER
    gdn_buf[SUBLANES:SUBLANES + L, :] = proj(C_GQ, C_GZ)
    ab = proj(C_AB, D_IN_PAD)
    lru_buf[SUBLANES:SUBLANES + L, :] = proj(C_LX, C_LG)
    lru_gate = proj(C_LG, C_GQ)
    gc_h, beta_pair = _gdn_gates(ab, galog_ref[...], gdt_ref[...], tri_ref, egc_ref, ebeta_ref, L)
    ctx = dict(gc_h=gc_h, beta=beta_pair, bd2=bd2_ref[...], qkv={}, **_gdn_masks())
    _gdn_prep(ctx, 0, gdn_buf, gcw_ref)
    lru_pre = _rg_lru_conv(lru_buf, lcw_ref, lcb_ref[...], wax_ref, L)
    heads = {}
    ret_gen = _retention_steps(proj, cos_ref[...], sin_ref[...],
                               (mk_ref, mv_ref, decay_ref, cross_ref, tail_ref, sdec_ref, bdr_ref, avg_ref),
                               rng_ref[...], ret_state, heads)
    next(ret_gen)
    lru_ab = _rg_lru_gates(lru_pre, lba_ref[...], lbx_ref[...], lam_ref[...])
    next(ret_gen)
    if n_super > 1:
        _gdn_prep(ctx, 1, gdn_buf, gcw_ref)
    par = {}
    o_rows = [[] for _ in range(n_pairs)]
    lru_cols = []
    _drain(_gdn_parallel_steps(ctx, 0, par))
    for sc in range(n_super):
        if sc + 2 < n_super:
            _gdn_prep(ctx, sc + 2, gdn_buf, gcw_ref)
        if len(lru_cols) < n_lru:
            t = len(lru_cols)
            lru_cols.append(_rg_lru_scan(lru_ab[t][0], lru_ab[t][1], lru_h, t * LANES, (t + 1) * LANES, L))
        nxt = _gdn_parallel_steps(ctx, sc + 1, par) if sc + 1 < n_super else iter(())
        ret_steps = 0 if sc + 2 < n_super else (1 if sc + 1 < n_super else 2)
        for _ in _gdn_recurrence_steps(ctx, par[sc], gdn_state, o_rows):
            next(nxt, None)
            next(nxt, None)
            for _ in range(ret_steps):
                next(ret_gen, None)
        _drain(nxt)
    while len(lru_cols) < n_lru:
        t = len(lru_cols)
        lru_cols.append(_rg_lru_scan(lru_ab[t][0], lru_ab[t][1], lru_h, t * LANES, (t + 1) * LANES, L))
    _drain(ret_gen)
    gdn_buf[0:SUBLANES, :] = gdn_buf[L:L + SUBLANES, :]
    lru_buf[0:SUBLANES, :] = lru_buf[L:L + SUBLANES, :]

    o_l = jnp.concatenate(lru_cols, axis=1) * _gelu_tanh(lru_gate)
    mix = _mm(heads["ret"], wout_ref[0:RET_W, :]) + _mm(o_l, wout_ref[RET_W:RET_W + LRU_W, :])
    avg2 = avg2_ref[...]
    z = proj(C_GZ, C_AB)
    outs = []
    for p in range(n_pairs):
        op = jnp.concatenate(o_rows[p], axis=0)
        outs.append(op * lax.rsqrt(_mm(op * op, avg2) + 1e-6))
    o_g = jnp.concatenate(outs, axis=1) * gng_ref[...] * _silu(z)
    w_g = wout_ref[RET_W + LRU_W:, :]
    slabs = [slice(r, r + RET_BLOCK) for r in range(0, L, RET_BLOCK)]
    mix_g = [_mm(o_g[rows], w_g) for rows in slabs]
    for rows, m_g in zip(slabs, mix_g):
        o_ref[rows, :] = _layer_norm(ALPHA * x[rows] + mix[rows] + m_g, lng_ref[...], lnb_ref[...])


def _mixer_constants():
    L = RET_BLOCK
    lg = jnp.log1p(-jnp.exp2(-5.0 - jnp.arange(RET_HEADS, dtype=F32)))
    lane = np.arange(RET_W)
    khead = (lane % LANES) // (RET_DK // 2)
    vhead = lane // RET_DK
    mk = jnp.asarray(np.pad((khead[None, :] == np.arange(RET_HEADS)[:, None]).astype(np.float32), ((0, 4), (0, 0))))
    mv = jnp.asarray(np.pad((vhead[None, :] == np.arange(RET_HEADS)[:, None]).astype(np.float32), ((0, 4), (0, 0))))
    idx = np.arange(L)
    diff = jnp.asarray((idx[:, None] - idx[None, :]).astype(np.float32))
    same = jnp.asarray(idx[:, None] // CHUNK == idx[None, :] // CHUNK)
    earlier = jnp.asarray(idx[None, :] // CHUNK < idx[:, None] // CHUNK)
    decay = jnp.where(same[None], jnp.exp(jnp.abs(diff)[None] * lg[:, None, None]),
                      jnp.where(earlier[None], jnp.exp(diff[None] * lg[:, None, None]), 0.0))
    lg_lane = lg[jnp.asarray(khead)]
    fidx = jnp.asarray(idx.astype(np.float32))
    cross = jnp.exp((fidx + 1.0)[:, None] * lg_lane[None, :])
    tail = jnp.exp((L - 1.0 - fidx)[:, None] * lg_lane[None, :])
    sdec = jnp.broadcast_to(jnp.exp(float(L) * lg_lane)[:, None], (RET_W, RET_W))
    bdr = jnp.asarray((khead[:, None] == vhead[None, :]).astype(np.float32))
    avg = jnp.asarray((vhead[:, None] == vhead[None, :]).astype(np.float32) / RET_DK).astype(BF16)
    sidx = np.arange(SUPER)
    tri = jnp.asarray(np.logical_and(sidx[:, None] // CHUNK == sidx[None, :] // CHUNK,
                                     sidx[:, None] >= sidx[None, :]).astype(np.float32)).astype(BF16)
    e_gc = np.zeros((LANES, GDN_HEADS * LANES), np.float32)
    e_beta = np.zeros((LANES, GDN_W), np.float32)
    for h in range(GDN_HEADS):
        e_gc[h, h * LANES:(h + 1) * LANES] = 1.0
        e_beta[GDN_HEADS + h, h * GDN_DK:(h + 1) * GDN_DK] = 1.0
    l2 = np.arange(LANES) // GDN_DK
    bd2 = (l2[:, None] == l2[None, :]).astype(np.float32)
    return dict(mk=mk, mv=mv, decay=decay, cross=cross, tail=tail, sdec=sdec, bdr=bdr, avg=avg, tri=tri,
                e_gc=jnp.asarray(e_gc).astype(BF16), e_beta=jnp.asarray(e_beta).astype(BF16),
                bd2=jnp.asarray(bd2), avg2=jnp.asarray(bd2 / GDN_DK).astype(BF16))


def _mixer(x, cos, sin, w, consts):
    B, S, _ = x.shape
    L = min(MIX_BLOCK, S)
    tok = lambda n: pl.BlockSpec((None, L, n), lambda b, j: (b, j, 0))
    c = consts
    operands = [
        (x, tok(D_MODEL)), (cos, tok(LANES)), (sin, tok(LANES)),
        (w["w_in"], None), (w["w_out"], None),
        (c["mk"], None), (c["mv"], None), (c["decay"], None), (c["cross"], None), (c["tail"], None),
        (c["sdec"], None), (c["bdr"], None), (c["avg"], None), (w["ret_norm_g"], None),
        (w["lru_conv_w"], None), (w["lru_conv_b"], None), (w["lru_wax"], None), (w["lru_b_a"], None),
        (w["lru_b_x"], None), (w["lru_lambda"], None),
        (w["gdn_conv_w"], None), (w["gdn_a_log"], None), (w["gdn_dt_bias"], None), (w["gdn_norm_g"], None),
        (c["tri"], None), (c["e_gc"], None), (c["e_beta"], None), (c["bd2"], None), (c["avg2"], None),
        (w["ln_g"], None), (w["ln_b"], None),
    ]
    args = [a for a, _ in operands]
    specs = [s if s is not None else _const_spec(a.shape) for a, s in operands]
    return pl.pallas_call(
        functools.partial(_mixer_body, L),
        grid=(B, S // L),
        in_specs=specs,
        out_specs=tok(D_MODEL),
        out_shape=jax.ShapeDtypeStruct((B, S, D_MODEL), F32),
        scratch_shapes=[
            pltpu.VMEM((RET_W, RET_W), F32),
            pltpu.VMEM((GDN_W // LANES, LANES, LANES), F32),
            pltpu.VMEM((SUBLANES, LRU_W), F32),
            pltpu.VMEM((L + SUBLANES, LRU_W), F32),
            pltpu.VMEM((L + SUBLANES, 3 * GDN_W), F32),
        ],
        compiler_params=pltpu.CompilerParams(dimension_semantics=("arbitrary", "arbitrary"),
                                             vmem_limit_bytes=VMEM_LIMIT_BYTES),
        name="mixer",
    )(*args)


def _relayout_w_in(w):
    d = w.shape[0]
    half = RET_DK // 2

    def halves_apart(cols):
        return cols.reshape(d, RET_HEADS, 2, half).transpose(0, 2, 1, 3).reshape(d, RET_W)

    parts = [halves_apart(w[:, 0:RET_W]), halves_apart(w[:, RET_W:2 * RET_W]), w[:, 2 * RET_W:],
             jnp.zeros((d, D_IN_PAD - D_IN), w.dtype)]
    return jnp.concatenate(parts, axis=1).astype(BF16)


def _pair_block_diag(w):
    z = jnp.zeros((LRU_BLOCK, LRU_BLOCK), w.dtype)
    slabs = []
    for p in range(LRU_W // LANES):
        top = jnp.concatenate([w[2 * p], z], axis=1)
        bot = jnp.concatenate([z, w[2 * p + 1]], axis=1)
        slabs.append(jnp.concatenate([top, bot], axis=0))
    return jnp.stack(slabs)


def _pad_rows(a, rows):
    return jnp.pad(a, ((0, rows - a.shape[0]), (0, 0)))


def _head_row(v):
    return jnp.pad(v, (0, LANES - v.shape[0])).reshape(1, LANES)


def kernel(x, p, positions, ln_ffn1_g, ln_ffn1_b, ffn1_w_gate, ffn1_w_up, ffn1_w_down, w_in, ret_norm_g,
           lru_conv_w, lru_conv_b, lru_w_a, lru_b_a, lru_w_x, lru_b_x, lru_lambda, gdn_conv_w, gdn_a_log,
           gdn_dt_bias, gdn_norm_g, w_out, ln_mix_g, ln_mix_b, ffn2_w_gate, ffn2_w_up, ffn2_w_down,
           ple_w_gate, ple_w_proj, ln_ffn2_g, ln_ffn2_b):
    B, S, _ = x.shape
    T = B * S
    row = lambda v: v.reshape(1, -1)
    cos, sin = _rope_tables(positions)
    consts = _mixer_constants()
    for i in range(DEPTH):
        x2 = _ffn(x.reshape(T, D_MODEL), ffn1_w_gate[i].astype(BF16), ffn1_w_up[i].astype(BF16),
                  ffn1_w_down[i].astype(BF16), row(ln_ffn1_g[i]), row(ln_ffn1_b[i]))
        mix_w = dict(
            w_in=_relayout_w_in(w_in[i]),
            w_out=w_out[i].astype(BF16),
            ret_norm_g=row(ret_norm_g[i]),
            lru_conv_w=_pad_rows(lru_conv_w[i], SUBLANES), lru_conv_b=row(lru_conv_b[i]),
            lru_wax=jnp.concatenate([_pair_block_diag(lru_w_a[i]), _pair_block_diag(lru_w_x[i])], axis=2).astype(BF16),
            lru_b_a=row(lru_b_a[i]), lru_b_x=row(lru_b_x[i]), lru_lambda=row(lru_lambda[i]),
            gdn_conv_w=_pad_rows(gdn_conv_w[i], SUBLANES),
            gdn_a_log=_head_row(gdn_a_log[i]), gdn_dt_bias=_head_row(gdn_dt_bias[i]),
            gdn_norm_g=row(jnp.tile(gdn_norm_g[i], GDN_HEADS)),
            ln_g=row(ln_mix_g[i]), ln_b=row(ln_mix_b[i]),
        )
        x3 = _mixer(x2.reshape(B, S, D_MODEL), cos, sin, mix_w, consts)
        x = _ffn(x3.reshape(T, D_MODEL), ffn2_w_gate[i].astype(BF16), ffn2_w_up[i].astype(BF16),
                 ffn2_w_down[i].astype(BF16), row(ln_ffn2_g[i]), row(ln_ffn2_b[i]),
                 ple=(p.reshape(DEPTH, T, PLE_DIM), i, ple_w_gate[i].astype(BF16), ple_w_proj[i].astype(BF16)))
        x = x.reshape(B, S, D_MODEL)
    return x
```

```python
import functools

import numpy as np
import jax
import jax.numpy as jnp
from jax import lax
from jax.experimental import pallas as pl
from jax.experimental.pallas import tpu as pltpu

F32 = jnp.float32
BF16 = jnp.bfloat16

D_MODEL = 1024
DEPTH = 2
CHUNK = 64
PLE_DIM = 256
D_FF = 2816
CONV_W = 4
RET_HEADS = 4
RET_DK = 64
RET_W = 256
LRU_W = 384
LRU_BLOCK = 64
LRU_C = 8.0
GDN_HEADS = 6
GDN_DK = 64
GDN_W = 384
IN_WIDTHS = (RET_W, RET_W, RET_W, RET_W, LRU_W, LRU_W, GDN_W, GDN_W, GDN_W, GDN_W, GDN_HEADS, GDN_HEADS)
D_IN = sum(IN_WIDTHS)
ROPE_THETA = 10000.0
ALPHA = (2 * DEPTH) ** 0.25
LN_EPS = 1e-5

LANES = 128
SUBLANES = 8
VMEM_LIMIT_BYTES = 56 * 1024 * 1024

MIX_BLOCK = 512
RET_BLOCK = 256
SUPER = 128
FFN_ROWS = 1024
FFN_COLS = 256
FFN_SLABS = 4
ROPE_ROWS = 1024

C_RQ, C_RK, C_RV, C_RG = 0, 256, 512, 768
C_LX, C_LG = 1024, 1408
C_GQ, C_GK, C_GV, C_GZ = 1792, 2176, 2560, 2944
C_AB = 3328
D_IN_PAD = 3456


def _mm(a, b):
    return jnp.dot(a.astype(BF16), b.astype(BF16), preferred_element_type=F32)


def _mm_nt(a, b):
    return lax.dot_general(a.astype(BF16), b.astype(BF16), (((1,), (1,)), ((), ())),
                           preferred_element_type=F32)


def _mm_tn(a, b):
    return lax.dot_general(a.astype(BF16), b.astype(BF16), (((0,), (0,)), ((), ())),
                           preferred_element_type=F32)


def _bmm(a, b):
    return lax.dot_general(a.astype(BF16), b.astype(BF16), (((2,), (1,)), ((0,), (0,))),
                           preferred_element_type=F32)


def _split(x, parts):
    out = []
    r = x
    for _ in range(parts):
        t = r.astype(BF16)
        out.append(t)
        r = r - t.astype(F32)
    return out


def _mm_sel(sel, x, parts):
    acc = None
    for t in _split(x, parts):
        y = jnp.dot(sel, t, preferred_element_type=F32)
        acc = y if acc is None else acc + y
    return acc


def _mm_xsel(x, sel, parts):
    acc = None
    for t in _split(x, parts):
        y = jnp.dot(t, sel, preferred_element_type=F32)
        acc = y if acc is None else acc + y
    return acc


def _sigmoid(x):
    return 1.0 / (1.0 + jnp.exp(-x))


def _silu(x):
    return x * _sigmoid(x)


def _softplus(x):
    return jnp.maximum(x, 0.0) + jnp.log(1.0 + jnp.exp(-jnp.abs(x)))


def _sqrt_nonneg(y):
    return jnp.where(y > 0.0, y * lax.rsqrt(y), 0.0)


def _gelu_tanh(x):
    c = np.float32(np.sqrt(2.0 / np.pi))
    return 0.5 * x * (1.0 + jnp.tanh(c * (x + 0.044715 * (x * x * x))))


def _layer_norm(r, g, b):
    mu = jnp.mean(r, axis=-1, keepdims=True)
    d = r - mu
    var = jnp.mean(d * d, axis=-1, keepdims=True)
    return d * lax.rsqrt(var + LN_EPS) * g + b


def _rope_body(pos_ref, invf_ref, cos_ref, sin_ref):
    half = RET_DK // 2
    per_row = LANES // half
    ang = pos_ref[...] * invf_ref[...]
    rows = ang.shape[0]
    group = jnp.right_shift(lax.broadcasted_iota(jnp.int32, (rows, LANES), 1), half.bit_length() - 1)
    for table, out_ref in ((jnp.cos(ang), cos_ref), (jnp.sin(ang), sin_ref)):
        for j in range(per_row):
            own = jnp.where(group == j, table, 0.0)
            spread = own
            for k in range(1, per_row):
                spread = spread + pltpu.roll(own, k * half, 1)
            out_ref[pl.ds(j, rows, stride=per_row), :] = spread


def _rope_tables(positions):
    B, S = positions.shape
    half = RET_DK // 2
    per_row = LANES // half
    T = B * S
    rows = T // per_row
    inv_freq = ROPE_THETA ** (-jnp.arange(half, dtype=F32) / half)
    pos_rep = jnp.broadcast_to(positions.astype(F32).reshape(rows, per_row, 1), (rows, per_row, half))
    invf = jnp.tile(inv_freq, per_row).reshape(1, LANES)
    blk = min(rows, ROPE_ROWS)
    cos, sin = pl.pallas_call(
        _rope_body,
        grid=(rows // blk,),
        in_specs=[pl.BlockSpec((blk, LANES), lambda i: (i, 0)),
                  pl.BlockSpec((1, LANES), lambda i: (0, 0))],
        out_specs=[pl.BlockSpec((per_row * blk, LANES), lambda i: (i, 0)),
                   pl.BlockSpec((per_row * blk, LANES), lambda i: (i, 0))],
        out_shape=[jax.ShapeDtypeStruct((T, LANES), F32)] * 2,
        name="rope_tables",
    )(pos_rep.reshape(rows, LANES), invf)
    return cos.reshape(B, S, LANES), sin.reshape(B, S, LANES)


def _ffn_body(has_ple, *refs):
    if has_ple:
        (x_ref, wg_ref, wu_ref, wd_ref, g_ref, b_ref, p_ref, pwg_ref, pwp_ref, o_ref, h_scr) = refs
    else:
        (x_ref, wg_ref, wu_ref, wd_ref, g_ref, b_ref, o_ref, h_scr) = refs
    x = x_ref[...]
    xb = x.astype(BF16)
    for c in range(0, D_FF, FFN_COLS):
        gate = jnp.dot(xb, wg_ref[:, c:c + FFN_COLS], preferred_element_type=F32)
        up = jnp.dot(xb, wu_ref[:, c:c + FFN_COLS], preferred_element_type=F32)
        h_scr[:, c:c + FFN_COLS] = (_silu(gate) * up).astype(BF16)
    r = ALPHA * x
    if has_ple:
        gate = _sigmoid(jnp.dot(xb, pwg_ref[...], preferred_element_type=F32))
        proj = jnp.dot(p_ref[...].astype(BF16), pwp_ref[...], preferred_element_type=F32)
        r = r + gate * proj
    n = x.shape[0]
    slabs = [slice(s, s + n // FFN_SLABS) for s in range(0, n, n // FFN_SLABS)]
    ys = [jnp.dot(h_scr[rows, :], wd_ref[...], preferred_element_type=F32) for rows in slabs]
    for rows, y in zip(slabs, ys):
        o_ref[rows, :] = _layer_norm(r[rows] + 0.5 * y, g_ref[...], b_ref[...])


def _const_spec(shape):
    nd = len(shape)
    return pl.BlockSpec(shape, lambda *_: (0,) * nd, pipeline_mode=pl.Buffered(1))


def _ffn(x2, wg, wu, wd, ln_g, ln_b, ple=None):
    T = x2.shape[0]
    rows = min(FFN_ROWS, T)
    tok = lambda n: pl.BlockSpec((rows, n), lambda i: (i, 0))
    args = [x2, wg, wu, wd, ln_g, ln_b]
    specs = [tok(D_MODEL), _const_spec(wg.shape), _const_spec(wu.shape), _const_spec(wd.shape),
             _const_spec(ln_g.shape), _const_spec(ln_b.shape)]
    if ple is not None:
        p_all, layer, pwg, pwp = ple
        args += [p_all, pwg, pwp]
        specs += [pl.BlockSpec((None, rows, PLE_DIM), lambda i: (layer, i, 0)),
                  _const_spec(pwg.shape), _const_spec(pwp.shape)]
    return pl.pallas_call(
        functools.partial(_ffn_body, ple is not None),
        grid=(T // rows,),
        in_specs=specs,
        out_specs=tok(D_MODEL),
        out_shape=jax.ShapeDtypeStruct((T, D_MODEL), F32),
        scratch_shapes=[pltpu.VMEM((rows, D_FF), BF16)],
        compiler_params=pltpu.CompilerParams(dimension_semantics=("arbitrary",),
                                             vmem_limit_bytes=VMEM_LIMIT_BYTES),
        name="ffn_ple" if ple is not None else "ffn",
    )(*args)


def _iota2(shape, dim):
    return lax.broadcasted_iota(jnp.int32, shape, dim)


def _drain(gen):
    for _ in gen:
        pass


def _retention_steps(proj, cos, sin, consts, norm_g, state_ref, out):
    mk_ref, mv_ref, decay_ref, cross_ref, tail_ref, sdec_ref, bd_ref, avg_ref = consts

    def rot(t):
        t1, t2 = t[:, :LANES], t[:, LANES:]
        return jnp.concatenate([t1 * cos - t2 * sin, t2 * cos + t1 * sin], axis=1)

    q_raw = proj(C_RQ, C_RK)
    k_raw = proj(C_RK, C_RV)
    yield
    v_all = proj(C_RV, C_RG)
    gate = proj(C_RG, C_LX)
    yield
    q_all = rot(q_raw) * (RET_DK ** -0.5)
    k_all = rot(k_raw)
    os = []
    for sb in range(q_all.shape[0] // RET_BLOCK):
        rows = slice(sb * RET_BLOCK, (sb + 1) * RET_BLOCK)
        q, k, v = q_all[rows], k_all[rows], v_all[rows]
        state = state_ref[...]
        o = _mm(q * cross_ref[...], state)
        state_ref[...] = state * sdec_ref[...] + _mm_tn(k * tail_ref[...], v) * bd_ref[...]
        yield
        kb = k.astype(BF16)
        for h in range(RET_HEADS):
            s = _mm_nt(q * mk_ref[h:h + 1, :], kb) * decay_ref[h]
            o = o + _mm(s, v * mv_ref[h:h + 1, :])
            yield
        os.append(o)
    o = jnp.concatenate(os, axis=0)
    avg = avg_ref[...]
    d = o - _mm(o, avg)
    var = _mm(d * d, avg)
    out["ret"] = d * lax.rsqrt(var + 1e-5) * norm_g * _silu(gate)
    yield


def _conv_taps(buf_ref, w_ref, r0, n, c0, c1):
    base = SUBLANES + r0
    y = buf_ref[base:base + n, c0:c1] * w_ref[CONV_W - 1:CONV_W, c0:c1]
    for k in range(CONV_W - 1):
        off = base - (CONV_W - 1) + k
        y = y + buf_ref[off:off + n, c0:c1] * w_ref[k:k + 1, c0:c1]
    return y


def _rg_lru_conv(buf_ref, conv_w_ref, conv_b, wax_ref, L):
    coeffs = []
    for p in range(LRU_W // LANES):
        c0, c1 = p * LANES, (p + 1) * LANES
        xc = _conv_taps(buf_ref, conv_w_ref, 0, L, c0, c1) + conv_b[:, c0:c1]
        y = _mm(xc, wax_ref[p])
        coeffs.append((xc, y, c0, c1))
    return coeffs


def _rg_lru_gates(coeffs, b_a, b_x, lam):
    out = []
    for xc, y, c0, c1 in coeffs:
        r = _sigmoid(y[:, :LANES] + b_a[:, c0:c1])
        i = _sigmoid(y[:, LANES:] + b_x[:, c0:c1])
        log_a = (LRU_C * (-_softplus(-lam[:, c0:c1]))) * r
        t = jnp.tanh(log_a)
        out.append((jnp.exp(log_a), _sqrt_nonneg(-2.0 * t / (1.0 - t)) * (i * xc)))
    return out


def _rg_lru_scan(a, b, h_ref, c0, c1, L):
    G = L // SUBLANES
    a3 = a.reshape(G, SUBLANES, LANES)
    b3 = b.reshape(G, SUBLANES, LANES)
    sub = _iota2((G, SUBLANES, LANES), 1)
    for d in (1, 2, 4):
        ok = sub >= d
        a_s = pltpu.roll(a3, d, 1)
        b_s = pltpu.roll(b3, d, 1)
        b3 = b3 + jnp.where(ok, a3 * b_s, 0.0)
        a3 = jnp.where(ok, a3 * a_s, a3)
    a_last = jnp.broadcast_to(a3[:, SUBLANES - 1:SUBLANES, :], a3.shape)
    b_last = jnp.broadcast_to(b3[:, SUBLANES - 1:SUBLANES, :], b3.shape)
    carry = h_ref[:, c0:c1]
    hs = []
    for g in range(G):
        hs.append(b3[g] + a3[g] * carry)
        carry = b_last[g] + a_last[g] * carry
    h_ref[:, c0:c1] = carry
    return jnp.concatenate(hs, axis=0)


def _gdn_gates(ab, a_log_row, dt_row, tri_ref, e_gc_ref, e_beta_ref, L):
    g_n = -jnp.exp(a_log_row) * _softplus(ab + dt_row)
    tri = tri_ref[...]
    gc_n = jnp.concatenate([_mm_sel(tri, g_n[s * SUPER:(s + 1) * SUPER], 2) for s in range(L // SUPER)], axis=0)
    gc_h = _mm_xsel(gc_n, e_gc_ref[...], 2)
    beta_pair = _mm_xsel(_sigmoid(ab), e_beta_ref[...], 2)
    return gc_h, beta_pair


def _gdn_masks():
    lane = _iota2((SUPER, LANES), 1)
    row = _iota2((SUPER, LANES), 0)
    chunk_bits = CHUNK.bit_length() - 1
    same_chunk = jnp.right_shift(row, chunk_bits) == jnp.right_shift(lane, chunk_bits)
    first_half = lane < CHUNK
    return dict(
        incl=jnp.logical_and(same_chunk, row >= lane), strict=jnp.logical_and(same_chunk, row > lane),
        eye=row == lane, first_half=first_half,
        masks=(jnp.where(first_half, 1.0, 0.0).astype(F32)[0:1, :], jnp.where(first_half, 0.0, 1.0).astype(F32)[0:1, :]))


def _gdn_prep(ctx, sc, buf_ref, conv_w_ref):
    W = GDN_W
    qkv = _silu(_conv_taps(buf_ref, conv_w_ref, sc * SUPER, SUPER, 0, 3 * W))
    bd2b = ctx["bd2"].astype(BF16)
    q, k, v = [], [], []
    for p in range(W // LANES):
        qk = jnp.concatenate([qkv[:, p * LANES:(p + 1) * LANES], qkv[:, W + p * LANES:W + (p + 1) * LANES]], axis=0)
        qk = qk * lax.rsqrt(_mm(qk * qk, bd2b) + 1e-6)
        q.append(qk[:SUPER] * (GDN_DK ** -0.5))
        k.append(qk[SUPER:])
        v.append(qkv[:, 2 * W + p * LANES:2 * W + (p + 1) * LANES])
    ctx["qkv"][sc] = (q, k, v)


def _gdn_parallel_steps(ctx, sc, out):
    rs = slice(sc * SUPER, (sc + 1) * SUPER)
    n_pairs = GDN_W // LANES
    m_a, m_b = ctx["masks"]
    eye, strict, incl = ctx["eye"], ctx["strict"], ctx["incl"]
    q_p, k_p, v_p = ctx["qkv"][sc]
    gcs = jnp.stack([ctx["gc_h"][rs, h * LANES:(h + 1) * LANES] for h in range(GDN_HEADS)])
    rowterm = jnp.sum(jnp.where(eye[None], gcs, 0.0), axis=1, keepdims=True)
    dec = jnp.exp(jnp.minimum(gcs - rowterm, 0.0))
    kk, qk, rhs, gc_pair, q_dec = [], [], [], [], []
    for p in range(n_pairs):
        kp, qp, vp = k_p[p], q_p[p], v_p[p]
        bp = ctx["beta"][rs, p * LANES:(p + 1) * LANES]
        gcp = jnp.where(ctx["first_half"], gcs[2 * p], gcs[2 * p + 1])
        eg = jnp.exp(gcp)
        kbb = kp * bp
        y = _mm_nt(jnp.concatenate([kbb * m_a, kbb * m_b, qp * m_a, qp * m_b], axis=0), kp)
        kk += [y[0:SUPER], y[SUPER:2 * SUPER]]
        qk += [y[2 * SUPER:3 * SUPER], y[3 * SUPER:4 * SUPER]]
        r_p = jnp.concatenate([vp * bp, kbb * eg], axis=1)
        rhs += [r_p * jnp.concatenate([m_a, m_a], axis=1), r_p * jnp.concatenate([m_b, m_b], axis=1)]
        gc_pair.append(gcp)
        q_dec.append(qp * eg)
    yield
    n_mat = jnp.where(strict[None], jnp.stack(kk) * dec, 0.0)
    qkd = jnp.where(incl[None], jnp.stack(qk) * dec, 0.0)
    x_inv = jnp.where(eye, 1.0, 0.0)[None] - n_mat
    pw = _bmm(n_mat, n_mat)
    yield
    for level in range(5):
        if level < 4:
            y = _bmm(jnp.concatenate([pw, x_inv], axis=1), pw)
            pw = y[:, :SUPER]
            x_inv = x_inv + y[:, SUPER:]
        else:
            x_inv = x_inv + _bmm(x_inv, pw)
        yield
    uw = _bmm(x_inv, jnp.stack(rhs))
    out[sc] = dict(
        u=[uw[2 * p, :, :LANES] + uw[2 * p + 1, :, :LANES] for p in range(n_pairs)],
        w=[uw[2 * p, :, LANES:] + uw[2 * p + 1, :, LANES:] for p in range(n_pairs)],
        qkd=qkd, gc_pair=gc_pair, q_dec=q_dec, k=k_p)
    yield


def _gdn_recurrence_steps(ctx, par, state_ref, o_rows):
    n_pairs = GDN_W // LANES
    m_a, m_b = ctx["masks"]
    bd2 = ctx["bd2"]
    zeros = jnp.zeros((CHUNK, LANES), F32)
    for c in range(SUPER // CHUNK):
        cs = slice(c * CHUNK, (c + 1) * CHUNK)
        last = c * CHUNK + CHUNK - 1
        ys = [_mm(jnp.concatenate([par["w"][p][cs], par["q_dec"][p][cs]], axis=0), state_ref[p])
              for p in range(n_pairs)]
        yield
        for p in range(n_pairs):
            gcp = par["gc_pair"][p]
            gl_row = gcp[last:last + 1, :]
            v_new = par["u"][p][cs] - ys[p][:CHUNK]
            k_tail = par["k"][p][cs] * jnp.exp(gl_row - gcp[cs])
            vm_a, vm_b = v_new * m_a, v_new * m_b
            v_cat = jnp.concatenate([vm_a, zeros, vm_b, zeros] if c == 0 else [zeros, vm_a, zeros, vm_b], axis=0)
            q_cat = jnp.concatenate([par["qkd"][2 * p][cs], par["qkd"][2 * p + 1][cs]], axis=1)
            o_rows[p].append(ys[p][CHUNK:] + _mm(q_cat, v_cat))
            state_ref[p] = state_ref[p] * jnp.exp(gl_row) + _mm_tn(k_tail, v_new) * bd2
        yield


def _mixer_body(L, x_ref, cos_ref, sin_ref, win_ref, wout_ref,
                mk_ref, mv_ref, decay_ref, cross_ref, tail_ref, sdec_ref, bdr_ref, avg_ref, rng_ref,
                lcw_ref, lcb_ref, wax_ref, lba_ref, lbx_ref, lam_ref,
                gcw_ref, galog_ref, gdt_ref, gng_ref, tri_ref, egc_ref, ebeta_ref, bd2_ref, avg2_ref,
                lng_ref, lnb_ref,
                o_ref, ret_state, gdn_state, lru_h, lru_buf, gdn_buf):
    @pl.when(pl.program_id(1) == 0)
    def _reset():
        ret_state[...] = jnp.zeros_like(ret_state)
        gdn_state[...] = jnp.zeros_like(gdn_state)
        lru_h[...] = jnp.zeros_like(lru_h)
        lru_buf[...] = jnp.zeros_like(lru_buf)
        gdn_buf[...] = jnp.zeros_like(gdn_buf)

    x = x_ref[...]
    xb = x.astype(BF16)

    def proj(lo, hi):
        return jnp.dot(xb, win_ref[:, lo:hi], preferred_element_type=F32)

    n_pairs = GDN_W // LANES
    n_lru = LRU_W // LANES
    n_super = L // SUPER
    gdn_buf[SUBLANES:SUBLANES + L, :] = proj(C_GQ, C_GZ)
    ab = proj(C_AB, D_IN_PAD)
    lru_buf[SUBLANES:SUBLANES + L, :] = proj(C_LX, C_LG)
    lru_gate = proj(C_LG, C_GQ)
    gc_h, beta_pair = _gdn_gates(ab, galog_ref[...], gdt_ref[...], tri_ref, egc_ref, ebeta_ref, L)
    ctx = dict(gc_h=gc_h, beta=beta_pair, bd2=bd2_ref[...], qkv={}, **_gdn_masks())
    _gdn_prep(ctx, 0, gdn_buf, gcw_ref)
    lru_pre = _rg_lru_conv(lru_buf, lcw_ref, lcb_ref[...], wax_ref, L)
    heads = {}
    ret_gen = _retention_steps(proj, cos_ref[...], sin_ref[...],
                               (mk_ref, mv_ref, decay_ref, cross_ref, tail_ref, sdec_ref, bdr_ref, avg_ref),
                               rng_ref[...], ret_state, heads)
    next(ret_gen)
    lru_ab = _rg_lru_gates(lru_pre, lba_ref[...], lbx_ref[...], lam_ref[...])
    next(ret_gen)
    if n_super > 1:
        _gdn_prep(ctx, 1, gdn_buf, gcw_ref)
    par = {}
    o_rows = [[] for _ in range(n_pairs)]
    lru_cols = []
    _drain(_gdn_parallel_steps(ctx, 0, par))
    for sc in range(n_super):
        if sc + 2 < n_super:
            _gdn_prep(ctx, sc + 2, gdn_buf, gcw_ref)
        if len(lru_cols) < n_lru:
            t = len(lru_cols)
            lru_cols.append(_rg_lru_scan(lru_ab[t][0], lru_ab[t][1], lru_h, t * LANES, (t + 1) * LANES, L))
        nxt = _gdn_parallel_steps(ctx, sc + 1, par) if sc + 1 < n_super else iter(())
        ret_steps = 0 if sc + 2 < n_super else (1 if sc + 1 < n_super else 2)
        for _ in _gdn_recurrence_steps(ctx, par[sc], gdn_state, o_rows):
            next(nxt, None)
            next(nxt, None)
            for _ in range(ret_steps):
                next(ret_gen, None)
        _drain(nxt)
    while len(lru_cols) < n_lru:
        t = len(lru_cols)
        lru_cols.append(_rg_lru_scan(lru_ab[t][0], lru_ab[t][1], lru_h, t * LANES, (t + 1) * LANES, L))
    _drain(ret_gen)
    gdn_buf[0:SUBLANES, :] = gdn_buf[L:L + SUBLANES, :]
    lru_buf[0:SUBLANES, :] = lru_buf[L:L + SUBLANES, :]

    o_l = jnp.concatenate(lru_cols, axis=1) * _gelu_tanh(lru_gate)
    mix = _mm(heads["ret"], wout_ref[0:RET_W, :]) + _mm(o_l, wout_ref[RET_W:RET_W + LRU_W, :])
    avg2 = avg2_ref[...]
    z = proj(C_GZ, C_AB)
    outs = []
    for p in range(n_pairs):
        op = jnp.concatenate(o_rows[p], axis=0)
        outs.append(op * lax.rsqrt(_mm(op * op, avg2) + 1e-6))
    o_g = jnp.concatenate(outs, axis=1) * gng_ref[...] * _silu(z)
    w_g = wout_ref[RET_W + LRU_W:, :]
    slabs = [slice(r, r + RET_BLOCK) for r in range(0, L, RET_BLOCK)]
    mix_g = [_mm(o_g[rows], w_g) for rows in slabs]
    for rows, m_g in zip(slabs, mix_g):
        o_ref[rows, :] = _layer_norm(ALPHA * x[rows] + mix[rows] + m_g, lng_ref[...], lnb_ref[...])


def _mixer_constants():
    L = RET_BLOCK
    lg = jnp.log1p(-jnp.exp2(-5.0 - jnp.arange(RET_HEADS, dtype=F32)))
    lane = np.arange(RET_W)
    khead = (lane % LANES) // (RET_DK // 2)
    vhead = lane // RET_DK
    mk = jnp.asarray(np.pad((khead[None, :] == np.arange(RET_HEADS)[:, None]).astype(np.float32), ((0, 4), (0, 0))))
    mv = jnp.asarray(np.pad((vhead[None, :] == np.arange(RET_HEADS)[:, None]).astype(np.float32), ((0, 4), (0, 0))))
    idx = np.arange(L)
    diff = jnp.asarray((idx[:, None] - idx[None, :]).astype(np.float32))
    same = jnp.asarray(idx[:, None] // CHUNK == idx[None, :] // CHUNK)
    earlier = jnp.asarray(idx[None, :] // CHUNK < idx[:, None] // CHUNK)
    decay = jnp.where(same[None], jnp.exp(jnp.abs(diff)[None] * lg[:, None, None]),
                      jnp.where(earlier[None], jnp.exp(diff[None] * lg[:, None, None]), 0.0))
    lg_lane = lg[jnp.asarray(khead)]
    fidx = jnp.asarray(idx.astype(np.float32))
    cross = jnp.exp((fidx + 1.0)[:, None] * lg_lane[None, :])
    tail = jnp.exp((L - 1.0 - fidx)[:, None] * lg_lane[None, :])
    sdec = jnp.broadcast_to(jnp.exp(float(L) * lg_lane)[:, None], (RET_W, RET_W))
    bdr = jnp.asarray((khead[:, None] == vhead[None, :]).astype(np.float32))
    avg = jnp.asarray((vhead[:, None] == vhead[None, :]).astype(np.float32) / RET_DK).astype(BF16)
    sidx = np.arange(SUPER)
    tri = jnp.asarray(np.logical_and(sidx[:, None] // CHUNK == sidx[None, :] // CHUNK,
                                     sidx[:, None] >= sidx[None, :]).astype(np.float32)).astype(BF16)
    e_gc = np.zeros((LANES, GDN_HEADS * LANES), np.float32)
    e_beta = np.zeros((LANES, GDN_W), np.float32)
    for h in range(GDN_HEADS):
        e_gc[h, h * LANES:(h + 1) * LANES] = 1.0
        e_beta[GDN_HEADS + h, h * GDN_DK:(h + 1) * GDN_DK] = 1.0
    l2 = np.arange(LANES) // GDN_DK
    bd2 = (l2[:, None] == l2[None, :]).astype(np.float32)
    return dict(mk=mk, mv=mv, decay=decay, cross=cross, tail=tail, sdec=sdec, bdr=bdr, avg=avg, tri=tri,
                e_gc=jnp.asarray(e_gc).astype(BF16), e_beta=jnp.asarray(e_beta).astype(BF16),
                bd2=jnp.asarray(bd2), avg2=jnp.asarray(bd2 / GDN_DK).astype(BF16))


def _mixer(x, cos, sin, w, consts):
    B, S, _ = x.shape
    L = min(MIX_BLOCK, S)
    tok = lambda n: pl.BlockSpec((None, L, n), lambda b, j: (b, j, 0))
    c = consts
    operands = [
        (x, tok(D_MODEL)), (cos, tok(LANES)), (sin, tok(LANES)),
        (w["w_in"], None), (w["w_out"], None),
        (c["mk"], None), (c["mv"], None), (c["decay"], None), (c["cross"], None), (c["tail"], None),
        (c["sdec"], None), (c["bdr"], None), (c["avg"], None), (w["ret_norm_g"], None),
        (w["lru_conv_w"], None), (w["lru_conv_b"], None), (w["lru_wax"], None), (w["lru_b_a"], None),
        (w["lru_b_x"], None), (w["lru_lambda"], None),
        (w["gdn_conv_w"], None), (w["gdn_a_log"], None), (w["gdn_dt_bias"], None), (w["gdn_norm_g"], None),
        (c["tri"], None), (c["e_gc"], None), (c["e_beta"], None), (c["bd2"], None), (c["avg2"], None),
        (w["ln_g"], None), (w["ln_b"], None),
    ]
    args = [a for a, _ in operands]
    specs = [s if s is not None else _const_spec(a.shape) for a, s in operands]
    return pl.pallas_call(
        functools.partial(_mixer_body, L),
        grid=(B, S // L),
        in_specs=specs,
        out_specs=tok(D_MODEL),
        out_shape=jax.ShapeDtypeStruct((B, S, D_MODEL), F32),
        scratch_shapes=[
            pltpu.VMEM((RET_W, RET_W), F32),
            pltpu.VMEM((GDN_W // LANES, LANES, LANES), F32),
            pltpu.VMEM((SUBLANES, LRU_W), F32),
            pltpu.VMEM((L + SUBLANES, LRU_W), F32),
            pltpu.VMEM((L + SUBLANES, 3 * GDN_W), F32),
        ],
        compiler_params=pltpu.CompilerParams(dimension_semantics=("arbitrary", "arbitrary"),
                                             vmem_limit_bytes=VMEM_LIMIT_BYTES),
        name="mixer",
    )(*args)


def _relayout_w_in(w):
    d = w.shape[0]
    half = RET_DK // 2

    def halves_apart(cols):
        return cols.reshape(d, RET_HEADS, 2, half).transpose(0, 2, 1, 3).reshape(d, RET_W)

    parts = [halves_apart(w[:, 0:RET_W]), halves_apart(w[:, RET_W:2 * RET_W]), w[:, 2 * RET_W:],
             jnp.zeros((d, D_IN_PAD - D_IN), w.dtype)]
    return jnp.concatenate(parts, axis=1).astype(BF16)


def _pair_block_diag(w):
    z = jnp.zeros((LRU_BLOCK, LRU_BLOCK), w.dtype)
    slabs = []
    for p in range(LRU_W // LANES):
        top = jnp.concatenate([w[2 * p], z], axis=1)
        bot = jnp.concatenate([z, w[2 * p + 1]], axis=1)
        slabs.append(jnp.concatenate([top, bot], axis=0))
    return jnp.stack(slabs)


def _pad_rows(a, rows):
    return jnp.pad(a, ((0, rows - a.shape[0]), (0, 0)))


def _head_row(v):
    return jnp.pad(v, (0, LANES - v.shape[0])).reshape(1, LANES)


def kernel(x, p, positions, ln_ffn1_g, ln_ffn1_b, ffn1_w_gate, ffn1_w_up, ffn1_w_down, w_in, ret_norm_g,
           lru_conv_w, lru_conv_b, lru_w_a, lru_b_a, lru_w_x, lru_b_x, lru_lambda, gdn_conv_w, gdn_a_log,
           gdn_dt_bias, gdn_norm_g, w_out, ln_mix_g, ln_mix_b, ffn2_w_gate, ffn2_w_up, ffn2_w_down,
           ple_w_gate, ple_w_proj, ln_ffn2_g, ln_ffn2_b):
    B, S, _ = x.shape
    T = B * S
    row = lambda v: v.reshape(1, -1)
    cos, sin = _rope_tables(positions)
    consts = _mixer_constants()
    for i in range(DEPTH):
        x2 = _ffn(x.reshape(T, D_MODEL), ffn1_w_gate[i].astype(BF16), ffn1_w_up[i].astype(BF16),
                  ffn1_w_down[i].astype(BF16), row(ln_ffn1_g[i]), row(ln_ffn1_b[i]))
        mix_w = dict(
            w_in=_relayout_w_in(w_in[i]),
            w_out=w_out[i].astype(BF16),
            ret_norm_g=row(ret_norm_g[i]),
            lru_conv_w=_pad_rows(lru_conv_w[i], SUBLANES), lru_conv_b=row(lru_conv_b[i]),
            lru_wax=jnp.concatenate([_pair_block_diag(lru_w_a[i]), _pair_block_diag(lru_w_x[i])], axis=2).astype(BF16),
            lru_b_a=row(lru_b_a[i]), lru_b_x=row(lru_b_x[i]), lru_lambda=row(lru_lambda[i]),
            gdn_conv_w=_pad_rows(gdn_conv_w[i], SUBLANES),
            gdn_a_log=_head_row(gdn_a_log[i]), gdn_dt_bias=_head_row(gdn_dt_bias[i]),
            gdn_norm_g=row(jnp.tile(gdn_norm_g[i], GDN_HEADS)),
            ln_g=row(ln_mix_g[i]), ln_b=row(ln_mix_b[i]),
        )
        x3 = _mixer(x2.reshape(B, S, D_MODEL), cos, sin, mix_w, consts)
        x = _ffn(x3.reshape(T, D_MODEL), ffn2_w_gate[i].astype(BF16), ffn2_w_up[i].astype(BF16),
                 ffn2_w_down[i].astype(BF16), row(ln_ffn2_g[i]), row(ln_ffn2_b[i]),
                 ple=(p.reshape(DEPTH, T, PLE_DIM), i, ple_w_gate[i].astype(BF16), ple_w_proj[i].astype(BF16)))
        x = x.reshape(B, S, D_MODEL)
    return x
```

```python
import functools

import numpy as np
import jax
import jax.numpy as jnp
from jax import lax
from jax.experimental import pallas as pl
from jax.experimental.pallas import tpu as pltpu

F32 = jnp.float32
BF16 = jnp.bfloat16

D_MODEL = 1024
DEPTH = 2
CHUNK = 64
PLE_DIM = 256
D_FF = 2816
CONV_W = 4
RET_HEADS = 4
RET_DK = 64
RET_W = 256
LRU_W = 384
LRU_BLOCK = 64
LRU_C = 8.0
GDN_HEADS = 6
GDN_DK = 64
GDN_W = 384
IN_WIDTHS = (RET_W, RET_W, RET_W, RET_W, LRU_W, LRU_W, GDN_W, GDN_W, GDN_W, GDN_W, GDN_HEADS, GDN_HEADS)
D_IN = sum(IN_WIDTHS)
ROPE_THETA = 10000.0
ALPHA = (2 * DEPTH) ** 0.25
LN_EPS = 1e-5

LANES = 128
SUBLANES = 8
VMEM_LIMIT_BYTES = 56 * 1024 * 1024

MIX_BLOCK = 512
RET_BLOCK = 256
SUPER = 128
FFN_ROWS = 1024
FFN_COLS = 256
FFN_SLABS = 4
ROPE_ROWS = 1024

C_RQ, C_RK, C_RV, C_RG = 0, 256, 512, 768
C_LX, C_LG = 1024, 1408
C_GQ, C_GK, C_GV, C_GZ = 1792, 2176, 2560, 2944
C_AB = 3328
D_IN_PAD = 3456


def _mm(a, b):
    return jnp.dot(a.astype(BF16), b.astype(BF16), preferred_element_type=F32)


def _mm_nt(a, b):
    return lax.dot_general(a.astype(BF16), b.astype(BF16), (((1,), (1,)), ((), ())),
                           preferred_element_type=F32)


def _mm_tn(a, b):
    return lax.dot_general(a.astype(BF16), b.astype(BF16), (((0,), (0,)), ((), ())),
                           preferred_element_type=F32)


def _bmm(a, b):
    return lax.dot_general(a.astype(BF16), b.astype(BF16), (((2,), (1,)), ((0,), (0,))),
                           preferred_element_type=F32)


def _split(x, parts):
    out = []
    r = x
    for _ in range(parts):
        t = r.astype(BF16)
        out.append(t)
        r = r - t.astype(F32)
    return out


def _mm_sel(sel, x, parts):
    acc = None
    for t in _split(x, parts):
        y = jnp.dot(sel, t, preferred_element_type=F32)
        acc = y if acc is None else acc + y
    return acc


def _mm_xsel(x, sel, parts):
    acc = None
    for t in _split(x, parts):
        y = jnp.dot(t, sel, preferred_element_type=F32)
        acc = y if acc is None else acc + y
    return acc


def _sigmoid(x):
    return 1.0 / (1.0 + jnp.exp(-x))


def _silu(x):
    return x * _sigmoid(x)


def _softplus(x):
    return jnp.maximum(x, 0.0) + jnp.log(1.0 + jnp.exp(-jnp.abs(x)))


def _sqrt_nonneg(y):
    return jnp.where(y > 0.0, y * lax.rsqrt(y), 0.0)


def _gelu_tanh(x):
    c = np.float32(np.sqrt(2.0 / np.pi))
    return 0.5 * x * (1.0 + jnp.tanh(c * (x + 0.044715 * (x * x * x))))


def _layer_norm(r, g, b):
    mu = jnp.mean(r, axis=-1, keepdims=True)
    d = r - mu
    var = jnp.mean(d * d, axis=-1, keepdims=True)
    return d * lax.rsqrt(var + LN_EPS) * g + b


def _rope_body(pos_ref, invf_ref, cos_ref, sin_ref):
    half = RET_DK // 2
    per_row = LANES // half
    ang = pos_ref[...] * invf_ref[...]
    rows = ang.shape[0]
    group = jnp.right_shift(lax.broadcasted_iota(jnp.int32, (rows, LANES), 1), half.bit_length() - 1)
    for table, out_ref in ((jnp.cos(ang), cos_ref), (jnp.sin(ang), sin_ref)):
        for j in range(per_row):
            own = jnp.where(group == j, table, 0.0)
            spread = own
            for k in range(1, per_row):
                spread = spread + pltpu.roll(own, k * half, 1)
            out_ref[pl.ds(j, rows, stride=per_row), :] = spread


def _rope_tables(positions):
    B, S = positions.shape
    half = RET_DK // 2
    per_row = LANES // half
    T = B * S
    rows = T // per_row
    inv_freq = ROPE_THETA ** (-jnp.arange(half, dtype=F32) / half)
    pos_rep = jnp.broadcast_to(positions.astype(F32).reshape(rows, per_row, 1), (rows, per_row, half))
    invf = jnp.tile(inv_freq, per_row).reshape(1, LANES)
    blk = min(rows, ROPE_ROWS)
    cos, sin = pl.pallas_call(
        _rope_body,
        grid=(rows // blk,),
        in_specs=[pl.BlockSpec((blk, LANES), lambda i: (i, 0)),
                  pl.BlockSpec((1, LANES), lambda i: (0, 0))],
        out_specs=[pl.BlockSpec((per_row * blk, LANES), lambda i: (i, 0)),
                   pl.BlockSpec((per_row * blk, LANES), lambda i: (i, 0))],
        out_shape=[jax.ShapeDtypeStruct((T, LANES), F32)] * 2,
        name="rope_tables",
    )(pos_rep.reshape(rows, LANES), invf)
    return cos.reshape(B, S, LANES), sin.reshape(B, S, LANES)


def _ffn_body(has_ple, *refs):
    if has_ple:
        (x_ref, wg_ref, wu_ref, wd_ref, g_ref, b_ref, p_ref, pwg_ref, pwp_ref, o_ref, h_scr) = refs
    else:
        (x_ref, wg_ref, wu_ref, wd_ref, g_ref, b_ref, o_ref, h_scr) = refs
    x = x_ref[...]
    xb = x.astype(BF16)
    for c in range(0, D_FF, FFN_COLS):
        gate = jnp.dot(xb, wg_ref[:, c:c + FFN_COLS], preferred_element_type=F32)
        up = jnp.dot(xb, wu_ref[:, c:c + FFN_COLS], preferred_element_type=F32)
        h_scr[:, c:c + FFN_COLS] = (_silu(gate) * up).astype(BF16)
    r = ALPHA * x
    if has_ple:
        gate = _sigmoid(jnp.dot(xb, pwg_ref[...], preferred_element_type=F32))
        proj = jnp.dot(p_ref[...].astype(BF16), pwp_ref[...], preferred_element_type=F32)
        r = r + gate * proj
    n = x.shape[0]
    slabs = [slice(s, s + n // FFN_SLABS) for s in range(0, n, n // FFN_SLABS)]
    ys = [jnp.dot(h_scr[rows, :], wd_ref[...], preferred_element_type=F32) for rows in slabs]
    for rows, y in zip(slabs, ys):
        o_ref[rows, :] = _layer_norm(r[rows] + 0.5 * y, g_ref[...], b_ref[...])


def _const_spec(shape):
    nd = len(shape)
    return pl.BlockSpec(shape, lambda *_: (0,) * nd, pipeline_mode=pl.Buffered(1))


def _ffn(x2, wg, wu, wd, ln_g, ln_b, ple=None):
    T = x2.shape[0]
    rows = min(FFN_ROWS, T)
    tok = lambda n: pl.BlockSpec((rows, n), lambda i: (i, 0))
    args = [x2, wg, wu, wd, ln_g, ln_b]
    specs = [tok(D_MODEL), _const_spec(wg.shape), _const_spec(wu.shape), _const_spec(wd.shape),
             _const_spec(ln_g.shape), _const_spec(ln_b.shape)]
    if ple is not None:
        p_all, layer, pwg, pwp = ple
        args += [p_all, pwg, pwp]
        specs += [pl.BlockSpec((None, rows, PLE_DIM), lambda i: (layer, i, 0)),
                  _const_spec(pwg.shape), _const_spec(pwp.shape)]
    return pl.pallas_call(
        functools.partial(_ffn_body, ple is not None),
        grid=(T // rows,),
        in_specs=specs,
        out_specs=tok(D_MODEL),
        out_shape=jax.ShapeDtypeStruct((T, D_MODEL), F32),
        scratch_shapes=[pltpu.VMEM((rows, D_FF), BF16)],
        compiler_params=pltpu.CompilerParams(dimension_semantics=("arbitrary",),
                                             vmem_limit_bytes=VMEM_LIMIT_BYTES),
        name="ffn_ple" if ple is not None else "ffn",
    )(*args)


def _iota2(shape, dim):
    return lax.broadcasted_iota(jnp.int32, shape, dim)


def _drain(gen):
    for _ in gen:
        pass


def _retention_steps(proj, cos, sin, consts, norm_g, state_ref, out):
    mk_ref, mv_ref, decay_ref, cross_ref, tail_ref, sdec_ref, bd_ref, avg_ref = consts

    def rot(t):
        t1, t2 = t[:, :LANES], t[:, LANES:]
        return jnp.concatenate([t1 * cos - t2 * sin, t2 * cos + t1 * sin], axis=1)

    q_raw = proj(C_RQ, C_RK)
    k_raw = proj(C_RK, C_RV)
    yield
    v_all = proj(C_RV, C_RG)
    gate = proj(C_RG, C_LX)
    yield
    q_all = rot(q_raw) * (RET_DK ** -0.5)
    k_all = rot(k_raw)
    os = []
    for sb in range(q_all.shape[0] // RET_BLOCK):
        rows = slice(sb * RET_BLOCK, (sb + 1) * RET_BLOCK)
        q, k, v = q_all[rows], k_all[rows], v_all[rows]
        state = state_ref[...]
        o = _mm(q * cross_ref[...], state)
        state_ref[...] = state * sdec_ref[...] + _mm_tn(k * tail_ref[...], v) * bd_ref[...]
        yield
        qb, kb, vb = q.astype(BF16), k.astype(BF16), v.astype(BF16)
        for h in range(RET_HEADS):
            s = _mm_nt(qb * mk_ref[h:h + 1, :].astype(BF16), kb) * decay_ref[h]
            o = o + _mm(s, vb * mv_ref[h:h + 1, :].astype(BF16))
            yield
        os.append(o)
    o = jnp.concatenate(os, axis=0)
    avg = avg_ref[...]
    d = o - _mm(o, avg)
    var = _mm(d * d, avg)
    out["ret"] = d * lax.rsqrt(var + 1e-5) * norm_g * _silu(gate)
    yield


def _conv_taps(buf_ref, w_ref, r0, n, c0, c1):
    base = SUBLANES + r0
    y = buf_ref[base:base + n, c0:c1] * w_ref[CONV_W - 1:CONV_W, c0:c1]
    for k in range(CONV_W - 1):
        off = base - (CONV_W - 1) + k
        y = y + buf_ref[off:off + n, c0:c1] * w_ref[k:k + 1, c0:c1]
    return y


def _rg_lru_conv(buf_ref, conv_w_ref, conv_b, wax_ref, L):
    coeffs = []
    for p in range(LRU_W // LANES):
        c0, c1 = p * LANES, (p + 1) * LANES
        xc = _conv_taps(buf_ref, conv_w_ref, 0, L, c0, c1) + conv_b[:, c0:c1]
        y = _mm(xc, wax_ref[p])
        coeffs.append((xc, y, c0, c1))
    return coeffs


def _rg_lru_gates(coeffs, b_a, b_x, lam):
    out = []
    for xc, y, c0, c1 in coeffs:
        r = _sigmoid(y[:, :LANES] + b_a[:, c0:c1])
        i = _sigmoid(y[:, LANES:] + b_x[:, c0:c1])
        log_a = (LRU_C * (-_softplus(-lam[:, c0:c1]))) * r
        t = jnp.tanh(log_a)
        out.append((jnp.exp(log_a), _sqrt_nonneg(-2.0 * t / (1.0 - t)) * (i * xc)))
    return out


def _rg_lru_scan(a, b, h_ref, c0, c1, L):
    G = L // SUBLANES
    a3 = a.reshape(G, SUBLANES, LANES)
    b3 = b.reshape(G, SUBLANES, LANES)
    sub = _iota2((G, SUBLANES, LANES), 1)
    for d in (1, 2, 4):
        ok = sub >= d
        a_s = pltpu.roll(a3, d, 1)
        b_s = pltpu.roll(b3, d, 1)
        b3 = b3 + jnp.where(ok, a3 * b_s, 0.0)
        a3 = jnp.where(ok, a3 * a_s, a3)
    a_last = jnp.broadcast_to(a3[:, SUBLANES - 1:SUBLANES, :], a3.shape)
    b_last = jnp.broadcast_to(b3[:, SUBLANES - 1:SUBLANES, :], b3.shape)
    carry = h_ref[:, c0:c1]
    hs = []
    for g in range(G):
        hs.append(b3[g] + a3[g] * carry)
        carry = b_last[g] + a_last[g] * carry
    h_ref[:, c0:c1] = carry
    return jnp.concatenate(hs, axis=0)


def _gdn_gates(ab, a_log_row, dt_row, tri_ref, e_gc_ref, e_beta_ref, L):
    g_n = -jnp.exp(a_log_row) * _softplus(ab + dt_row)
    tri = tri_ref[...]
    gc_n = jnp.concatenate([_mm_sel(tri, g_n[s * SUPER:(s + 1) * SUPER], 2) for s in range(L // SUPER)], axis=0)
    gc_h = _mm_xsel(gc_n, e_gc_ref[...], 2)
    beta_pair = _mm_xsel(_sigmoid(ab), e_beta_ref[...], 2)
    return gc_h, beta_pair


def _gdn_masks():
    lane = _iota2((SUPER, LANES), 1)
    row = _iota2((SUPER, LANES), 0)
    chunk_bits = CHUNK.bit_length() - 1
    same_chunk = jnp.right_shift(row, chunk_bits) == jnp.right_shift(lane, chunk_bits)
    first_half = lane < CHUNK
    return dict(
        incl=jnp.logical_and(same_chunk, row >= lane), strict=jnp.logical_and(same_chunk, row > lane),
        eye=row == lane, first_half=first_half,
        masks=(jnp.where(first_half, 1.0, 0.0).astype(F32)[0:1, :], jnp.where(first_half, 0.0, 1.0).astype(F32)[0:1, :]))


def _gdn_prep(ctx, sc, buf_ref, conv_w_ref):
    W = GDN_W
    qkv = _silu(_conv_taps(buf_ref, conv_w_ref, sc * SUPER, SUPER, 0, 3 * W))
    bd2b = ctx["bd2"].astype(BF16)
    q, k, v = [], [], []
    for p in range(W // LANES):
        qk = jnp.concatenate([qkv[:, p * LANES:(p + 1) * LANES], qkv[:, W + p * LANES:W + (p + 1) * LANES]], axis=0)
        qk = qk * lax.rsqrt(_mm(qk * qk, bd2b) + 1e-6)
        q.append(qk[:SUPER] * (GDN_DK ** -0.5))
        k.append(qk[SUPER:])
        v.append(qkv[:, 2 * W + p * LANES:2 * W + (p + 1) * LANES])
    ctx["qkv"][sc] = (q, k, v)


def _gdn_parallel_steps(ctx, sc, out):
    rs = slice(sc * SUPER, (sc + 1) * SUPER)
    n_pairs = GDN_W // LANES
    m_a, m_b = ctx["masks"]
    eye, strict, incl = ctx["eye"], ctx["strict"], ctx["incl"]
    q_p, k_p, v_p = ctx["qkv"][sc]
    mb_a, mb_b = m_a.astype(BF16), m_b.astype(BF16)
    gcs = jnp.stack([ctx["gc_h"][rs, h * LANES:(h + 1) * LANES] for h in range(GDN_HEADS)])
    rowterm = jnp.sum(jnp.where(eye[None], gcs, 0.0), axis=1, keepdims=True)
    dec = jnp.exp(jnp.minimum(gcs - rowterm, 0.0))
    kk, qk, rhs, gc_pair, q_dec = [], [], [], [], []
    for p in range(n_pairs):
        kp, qp, vp = k_p[p], q_p[p], v_p[p]
        bp = ctx["beta"][rs, p * LANES:(p + 1) * LANES]
        gcp = jnp.where(ctx["first_half"], gcs[2 * p], gcs[2 * p + 1])
        eg = jnp.exp(gcp)
        kbb = kp * bp
        kbb_b, qp_b = kbb.astype(BF16), qp.astype(BF16)
        y = _mm_nt(jnp.concatenate([kbb_b * mb_a, kbb_b * mb_b, qp_b * mb_a, qp_b * mb_b], axis=0), kp)
        kk += [y[0:SUPER], y[SUPER:2 * SUPER]]
        qk += [y[2 * SUPER:3 * SUPER], y[3 * SUPER:4 * SUPER]]
        r_p = jnp.concatenate([vp * bp, kbb * eg], axis=1).astype(BF16)
        rhs += [r_p * jnp.concatenate([mb_a, mb_a], axis=1), r_p * jnp.concatenate([mb_b, mb_b], axis=1)]
        gc_pair.append(gcp)
        q_dec.append(qp * eg)
    yield
    n_mat = jnp.where(strict[None], jnp.stack(kk) * dec, 0.0)
    qkd = jnp.where(incl[None], jnp.stack(qk) * dec, 0.0)
    x_inv = jnp.where(eye, 1.0, 0.0)[None] - n_mat
    pw = _bmm(n_mat, n_mat)
    yield
    for level in range(5):
        if level < 4:
            pw = pw.astype(BF16)
            y = _bmm(jnp.concatenate([pw, x_inv.astype(BF16)], axis=1), pw)
            pw = y[:, :SUPER]
            x_inv = x_inv + y[:, SUPER:]
        else:
            x_inv = x_inv + _bmm(x_inv, pw)
        yield
    uw = _bmm(x_inv, jnp.stack(rhs))
    out[sc] = dict(
        u=[uw[2 * p, :, :LANES] + uw[2 * p + 1, :, :LANES] for p in range(n_pairs)],
        w=[uw[2 * p, :, LANES:] + uw[2 * p + 1, :, LANES:] for p in range(n_pairs)],
        qkd=qkd, gc_pair=gc_pair, q_dec=q_dec, k=k_p)
    yield


def _gdn_recurrence_steps(ctx, par, state_ref, o_rows):
    n_pairs = GDN_W // LANES
    mb_a, mb_b = (m.astype(BF16) for m in ctx["masks"])
    bd2 = ctx["bd2"]
    zeros = jnp.zeros((CHUNK, LANES), BF16)
    for c in range(SUPER // CHUNK):
        cs = slice(c * CHUNK, (c + 1) * CHUNK)
        last = c * CHUNK + CHUNK - 1
        ys = [_mm(jnp.concatenate([par["w"][p][cs], par["q_dec"][p][cs]], axis=0), state_ref[p])
              for p in range(n_pairs)]
        yield
        for p in range(n_pairs):
            gcp = par["gc_pair"][p]
            gl_row = gcp[last:last + 1, :]
            v_new = par["u"][p][cs] - ys[p][:CHUNK]
            k_tail = par["k"][p][cs] * jnp.exp(gl_row - gcp[cs])
            v_new_b = v_new.astype(BF16)
            vm_a, vm_b = v_new_b * mb_a, v_new_b * mb_b
            v_cat = jnp.concatenate([vm_a, zeros, vm_b, zeros] if c == 0 else [zeros, vm_a, zeros, vm_b], axis=0)
            q_cat = jnp.concatenate([par["qkd"][2 * p][cs], par["qkd"][2 * p + 1][cs]], axis=1)
            o_rows[p].append(ys[p][CHUNK:] + _mm(q_cat, v_cat))
            state_ref[p] = state_ref[p] * jnp.exp(gl_row) + _mm_tn(k_tail, v_new_b) * bd2
        yield


def _mixer_body(L, x_ref, cos_ref, sin_ref, win_ref, wout_ref,
                mk_ref, mv_ref, decay_ref, cross_ref, tail_ref, sdec_ref, bdr_ref, avg_ref, rng_ref,
                lcw_ref, lcb_ref, wax_ref, lba_ref, lbx_ref, lam_ref,
                gcw_ref, galog_ref, gdt_ref, gng_ref, tri_ref, egc_ref, ebeta_ref, bd2_ref, avg2_ref,
                lng_ref, lnb_ref,
                o_ref, ret_state, gdn_state, lru_h, lru_buf, gdn_buf):
    @pl.when(pl.program_id(1) == 0)
    def _reset():
        ret_state[...] = jnp.zeros_like(ret_state)
        gdn_state[...] = jnp.zeros_like(gdn_state)
        lru_h[...] = jnp.zeros_like(lru_h)
        lru_buf[...] = jnp.zeros_like(lru_buf)
        gdn_buf[...] = jnp.zeros_like(gdn_buf)

    x = x_ref[...]
    xb = x.astype(BF16)

    def proj(lo, hi):
        return jnp.dot(xb, win_ref[:, lo:hi], preferred_element_type=F32)

    n_pairs = GDN_W // LANES
    n_lru = LRU_W // LANES
    n_super = L // SUPER
    gdn_buf[SUBLANES:SUBLANES + L, :] = proj(C_GQ, C_GZ)
    ab = proj(C_AB, D_IN_PAD)
    lru_buf[SUBLANES:SUBLANES + L, :] = proj(C_LX, C_LG)
    lru_gate = proj(C_LG, C_GQ)
    gc_h, beta_pair = _gdn_gates(ab, galog_ref[...], gdt_ref[...], tri_ref, egc_ref, ebeta_ref, L)
    ctx = dict(gc_h=gc_h, beta=beta_pair, bd2=bd2_ref[...], qkv={}, **_gdn_masks())
    _gdn_prep(ctx, 0, gdn_buf, gcw_ref)
    lru_pre = _rg_lru_conv(lru_buf, lcw_ref, lcb_ref[...], wax_ref, L)
    heads = {}
    ret_gen = _retention_steps(proj, cos_ref[...], sin_ref[...],
                               (mk_ref, mv_ref, decay_ref, cross_ref, tail_ref, sdec_ref, bdr_ref, avg_ref),
                               rng_ref[...], ret_state, heads)
    next(ret_gen)
    lru_ab = _rg_lru_gates(lru_pre, lba_ref[...], lbx_ref[...], lam_ref[...])
    next(ret_gen)
    if n_super > 1:
        _gdn_prep(ctx, 1, gdn_buf, gcw_ref)
    par = {}
    o_rows = [[] for _ in range(n_pairs)]
    lru_cols = []
    mix_l = []

    def lru_out_steps():
        o_l = (jnp.concatenate(lru_cols, axis=1) * _gelu_tanh(lru_gate)).astype(BF16)
        for c in range(0, D_MODEL, RET_W):
            mix_l.append(jnp.dot(o_l, wout_ref[RET_W:RET_W + LRU_W, c:c + RET_W], preferred_element_type=F32))
            yield

    _drain(_gdn_parallel_steps(ctx, 0, par))
    for sc in range(n_super):
        if sc + 2 < n_super:
            _gdn_prep(ctx, sc + 2, gdn_buf, gcw_ref)
        if len(lru_cols) < n_lru:
            t = len(lru_cols)
            lru_cols.append(_rg_lru_scan(lru_ab[t][0], lru_ab[t][1], lru_h, t * LANES, (t + 1) * LANES, L))
        nxt = _gdn_parallel_steps(ctx, sc + 1, par) if sc + 1 < n_super else iter(())
        ret_steps = 0 if sc + 2 < n_super else (1 if sc + 1 < n_super else 2)
        if sc + 1 == n_super:
            while len(lru_cols) < n_lru:
                t = len(lru_cols)
                lru_cols.append(_rg_lru_scan(lru_ab[t][0], lru_ab[t][1], lru_h, t * LANES, (t + 1) * LANES, L))
            lru_out = lru_out_steps()
        for _ in _gdn_recurrence_steps(ctx, par[sc], gdn_state, o_rows):
            next(nxt, None)
            next(nxt, None)
            for _ in range(ret_steps):
                next(ret_gen, None)
            if sc + 1 == n_super:
                next(lru_out, None)
        _drain(nxt)
    _drain(ret_gen)
    _drain(lru_out)
    gdn_buf[0:SUBLANES, :] = gdn_buf[L:L + SUBLANES, :]
    lru_buf[0:SUBLANES, :] = lru_buf[L:L + SUBLANES, :]

    mix = _mm(heads["ret"], wout_ref[0:RET_W, :]) + jnp.concatenate(mix_l, axis=1)
    avg2 = avg2_ref[...]
    z = proj(C_GZ, C_AB)
    outs = []
    for p in range(n_pairs):
        op = jnp.concatenate(o_rows[p], axis=0)
        outs.append(op * lax.rsqrt(_mm(op * op, avg2) + 1e-6))
    o_g = jnp.concatenate(outs, axis=1) * gng_ref[...] * _silu(z)
    w_g = wout_ref[RET_W + LRU_W:, :]
    slabs = [slice(r, r + RET_BLOCK) for r in range(0, L, RET_BLOCK)]
    mix_g = [_mm(o_g[rows], w_g) for rows in slabs]
    for rows, m_g in zip(slabs, mix_g):
        o_ref[rows, :] = _layer_norm(ALPHA * x[rows] + mix[rows] + m_g, lng_ref[...], lnb_ref[...])


def _mixer_constants():
    L = RET_BLOCK
    lg = jnp.log1p(-jnp.exp2(-5.0 - jnp.arange(RET_HEADS, dtype=F32)))
    lane = np.arange(RET_W)
    khead = (lane % LANES) // (RET_DK // 2)
    vhead = lane // RET_DK
    mk = jnp.asarray(np.pad((khead[None, :] == np.arange(RET_HEADS)[:, None]).astype(np.float32), ((0, 4), (0, 0))))
    mv = jnp.asarray(np.pad((vhead[None, :] == np.arange(RET_HEADS)[:, None]).astype(np.float32), ((0, 4), (0, 0))))
    idx = np.arange(L)
    diff = jnp.asarray((idx[:, None] - idx[None, :]).astype(np.float32))
    same = jnp.asarray(idx[:, None] // CHUNK == idx[None, :] // CHUNK)
    earlier = jnp.asarray(idx[None, :] // CHUNK < idx[:, None] // CHUNK)
    decay = jnp.where(same[None], jnp.exp(jnp.abs(diff)[None] * lg[:, None, None]),
                      jnp.where(earlier[None], jnp.exp(diff[None] * lg[:, None, None]), 0.0))
    lg_lane = lg[jnp.asarray(khead)]
    fidx = jnp.asarray(idx.astype(np.float32))
    cross = jnp.exp((fidx + 1.0)[:, None] * lg_lane[None, :])
    tail = jnp.exp((L - 1.0 - fidx)[:, None] * lg_lane[None, :])
    sdec = jnp.broadcast_to(jnp.exp(float(L) * lg_lane)[:, None], (RET_W, RET_W))
    bdr = jnp.asarray((khead[:, None] == vhead[None, :]).astype(np.float32))
    avg = jnp.asarray((vhead[:, None] == vhead[None, :]).astype(np.float32) / RET_DK).astype(BF16)
    sidx = np.arange(SUPER)
    tri = jnp.asarray(np.logical_and(sidx[:, None] // CHUNK == sidx[None, :] // CHUNK,
                                     sidx[:, None] >= sidx[None, :]).astype(np.float32)).astype(BF16)
    e_gc = np.zeros((LANES, GDN_HEADS * LANES), np.float32)
    e_beta = np.zeros((LANES, GDN_W), np.float32)
    for h in range(GDN_HEADS):
        e_gc[h, h * LANES:(h + 1) * LANES] = 1.0
        e_beta[GDN_HEADS + h, h * GDN_DK:(h + 1) * GDN_DK] = 1.0
    l2 = np.arange(LANES) // GDN_DK
    bd2 = (l2[:, None] == l2[None, :]).astype(np.float32)
    return dict(mk=mk, mv=mv, decay=decay, cross=cross, tail=tail, sdec=sdec, bdr=bdr, avg=avg, tri=tri,
                e_gc=jnp.asarray(e_gc).astype(BF16), e_beta=jnp.asarray(e_beta).astype(BF16),
                bd2=jnp.asarray(bd2), avg2=jnp.asarray(bd2 / GDN_DK).astype(BF16))


def _mixer(x, cos, sin, w, consts):
    B, S, _ = x.shape
    L = min(MIX_BLOCK, S)
    tok = lambda n: pl.BlockSpec((None, L, n), lambda b, j: (b, j, 0))
    c = consts
    operands = [
        (x, tok(D_MODEL)), (cos, tok(LANES)), (sin, tok(LANES)),
        (w["w_in"], None), (w["w_out"], None),
        (c["mk"], None), (c["mv"], None), (c["decay"], None), (c["cross"], None), (c["tail"], None),
        (c["sdec"], None), (c["bdr"], None), (c["avg"], None), (w["ret_norm_g"], None),
        (w["lru_conv_w"], None), (w["lru_conv_b"], None), (w["lru_wax"], None), (w["lru_b_a"], None),
        (w["lru_b_x"], None), (w["lru_lambda"], None),
        (w["gdn_conv_w"], None), (w["gdn_a_log"], None), (w["gdn_dt_bias"], None), (w["gdn_norm_g"], None),
        (c["tri"], None), (c["e_gc"], None), (c["e_beta"], None), (c["bd2"], None), (c["avg2"], None),
        (w["ln_g"], None), (w["ln_b"], None),
    ]
    args = [a for a, _ in operands]
    specs = [s if s is not None else _const_spec(a.shape) for a, s in operands]
    return pl.pallas_call(
        functools.partial(_mixer_body, L),
        grid=(B, S // L),
        in_specs=specs,
        out_specs=tok(D_MODEL),
        out_shape=jax.ShapeDtypeStruct((B, S, D_MODEL), F32),
        scratch_shapes=[
            pltpu.VMEM((RET_W, RET_W), F32),
            pltpu.VMEM((GDN_W // LANES, LANES, LANES), F32),
            pltpu.VMEM((SUBLANES, LRU_W), F32),
            pltpu.VMEM((L + SUBLANES, LRU_W), F32),
            pltpu.VMEM((L + SUBLANES, 3 * GDN_W), F32),
        ],
        compiler_params=pltpu.CompilerParams(dimension_semantics=("arbitrary", "arbitrary"),
                                             vmem_limit_bytes=VMEM_LIMIT_BYTES),
        name="mixer",
    )(*args)


def _relayout_w_in(w):
    d = w.shape[0]
    half = RET_DK // 2

    def halves_apart(cols):
        return cols.reshape(d, RET_HEADS, 2, half).transpose(0, 2, 1, 3).reshape(d, RET_W)

    parts = [halves_apart(w[:, 0:RET_W]), halves_apart(w[:, RET_W:2 * RET_W]), w[:, 2 * RET_W:],
             jnp.zeros((d, D_IN_PAD - D_IN), w.dtype)]
    return jnp.concatenate(parts, axis=1).astype(BF16)


def _pair_block_diag(w):
    z = jnp.zeros((LRU_BLOCK, LRU_BLOCK), w.dtype)
    slabs = []
    for p in range(LRU_W // LANES):
        top = jnp.concatenate([w[2 * p], z], axis=1)
        bot = jnp.concatenate([z, w[2 * p + 1]], axis=1)
        slabs.append(jnp.concatenate([top, bot], axis=0))
    return jnp.stack(slabs)


def _pad_rows(a, rows):
    return jnp.pad(a, ((0, rows - a.shape[0]), (0, 0)))


def _head_row(v):
    return jnp.pad(v, (0, LANES - v.shape[0])).reshape(1, LANES)


def kernel(x, p, positions, ln_ffn1_g, ln_ffn1_b, ffn1_w_gate, ffn1_w_up, ffn1_w_down, w_in, ret_norm_g,
           lru_conv_w, lru_conv_b, lru_w_a, lru_b_a, lru_w_x, lru_b_x, lru_lambda, gdn_conv_w, gdn_a_log,
           gdn_dt_bias, gdn_norm_g, w_out, ln_mix_g, ln_mix_b, ffn2_w_gate, ffn2_w_up, ffn2_w_down,
           ple_w_gate, ple_w_proj, ln_ffn2_g, ln_ffn2_b):
    B, S, _ = x.shape
    T = B * S
    row = lambda v: v.reshape(1, -1)
    cos, sin = _rope_tables(positions)
    consts = _mixer_constants()
    for i in range(DEPTH):
        x2 = _ffn(x.reshape(T, D_MODEL), ffn1_w_gate[i].astype(BF16), ffn1_w_up[i].astype(BF16),
                  ffn1_w_down[i].astype(BF16), row(ln_ffn1_g[i]), row(ln_ffn1_b[i]))
        mix_w = dict(
            w_in=_relayout_w_in(w_in[i]),
            w_out=w_out[i].astype(BF16),
            ret_norm_g=row(ret_norm_g[i]),
            lru_conv_w=_pad_rows(lru_conv_w[i], SUBLANES), lru_conv_b=row(lru_conv_b[i]),
            lru_wax=jnp.concatenate([_pair_block_diag(lru_w_a[i]), _pair_block_diag(lru_w_x[i])], axis=2).astype(BF16),
            lru_b_a=row(lru_b_a[i]), lru_b_x=row(lru_b_x[i]), lru_lambda=row(lru_lambda[i]),
            gdn_conv_w=_pad_rows(gdn_conv_w[i], SUBLANES),
            gdn_a_log=_head_row(gdn_a_log[i]), gdn_dt_bias=_head_row(gdn_dt_bias[i]),
            gdn_norm_g=row(jnp.tile(gdn_norm_g[i], GDN_HEADS)),
            ln_g=row(ln_mix_g[i]), ln_b=row(ln_mix_b[i]),
        )
        x3 = _mixer(x2.reshape(B, S, D_MODEL), cos, sin, mix_w, consts)
        x = _ffn(x3.reshape(T, D_MODEL), ffn2_w_gate[i].astype(BF16), ffn2_w_up[i].astype(BF16),
                 ffn2_w_down[i].astype(BF16), row(ln_ffn2_g[i]), row(ln_ffn2_b[i]),
                 ple=(p.reshape(DEPTH, T, PLE_DIM), i, ple_w_gate[i].astype(BF16), ple_w_proj[i].astype(BF16)))
        x = x.reshape(B, S, D_MODEL)
    return x
```

```python
import functools

import numpy as np
import jax
import jax.numpy as jnp
from jax import lax
from jax.experimental import pallas as pl
from jax.experimental.pallas import tpu as pltpu

F32 = jnp.float32
BF16 = jnp.bfloat16

D_MODEL = 1024
DEPTH = 2
CHUNK = 64
PLE_DIM = 256
D_FF = 2816
CONV_W = 4
RET_HEADS = 4
RET_DK = 64
RET_W = 256
LRU_W = 384
LRU_BLOCK = 64
LRU_C = 8.0
GDN_HEADS = 6
GDN_DK = 64
GDN_W = 384
IN_WIDTHS = (RET_W, RET_W, RET_W, RET_W, LRU_W, LRU_W, GDN_W, GDN_W, GDN_W, GDN_W, GDN_HEADS, GDN_HEADS)
D_IN = sum(IN_WIDTHS)
ROPE_THETA = 10000.0
ALPHA = (2 * DEPTH) ** 0.25
LN_EPS = 1e-5

LANES = 128
SUBLANES = 8
VMEM_LIMIT_BYTES = 56 * 1024 * 1024

MIX_BLOCK = 512
RET_BLOCK = 256
SUPER = 128
FFN_ROWS = 1024
FFN_COLS = 256
FFN_SLABS = 4
ROPE_ROWS = 1024

C_RQ, C_RK, C_RV, C_RG = 0, 256, 512, 768
C_LX, C_LG = 1024, 1408
C_GQ, C_GK, C_GV, C_GZ = 1792, 2176, 2560, 2944
C_AB = 3328
D_IN_PAD = 3456


def _mm(a, b):
    return jnp.dot(a.astype(BF16), b.astype(BF16), preferred_element_type=F32)


def _mm_nt(a, b):
    return lax.dot_general(a.astype(BF16), b.astype(BF16), (((1,), (1,)), ((), ())),
                           preferred_element_type=F32)


def _mm_tn(a, b):
    return lax.dot_general(a.astype(BF16), b.astype(BF16), (((0,), (0,)), ((), ())),
                           preferred_element_type=F32)


def _bmm(a, b):
    return lax.dot_general(a.astype(BF16), b.astype(BF16), (((2,), (1,)), ((0,), (0,))),
                           preferred_element_type=F32)


def _split(x, parts):
    out = []
    r = x
    for _ in range(parts):
        t = r.astype(BF16)
        out.append(t)
        r = r - t.astype(F32)
    return out


def _mm_sel(sel, x, parts):
    acc = None
    for t in _split(x, parts):
        y = jnp.dot(sel, t, preferred_element_type=F32)
        acc = y if acc is None else acc + y
    return acc


def _mm_xsel(x, sel, parts):
    acc = None
    for t in _split(x, parts):
        y = jnp.dot(t, sel, preferred_element_type=F32)
        acc = y if acc is None else acc + y
    return acc


LOG2_E = 1.4426950408889634


def _sigmoid(x):
    return 1.0 / (1.0 + jnp.exp2(x * -LOG2_E))


def _silu(x):
    return x * _sigmoid(x)


def _softplus(x):
    return jnp.maximum(x, 0.0) + jnp.log(1.0 + jnp.exp(-jnp.abs(x)))


def _sqrt_nonneg(y):
    return jnp.where(y > 0.0, y * lax.rsqrt(y), 0.0)


def _gelu_tanh(x):
    c = np.float32(np.sqrt(2.0 / np.pi))
    return 0.5 * x * (1.0 + jnp.tanh(c * (x + 0.044715 * (x * x * x))))


def _layer_norm(r, g, b):
    mu = jnp.mean(r, axis=-1, keepdims=True)
    d = r - mu
    var = jnp.mean(d * d, axis=-1, keepdims=True)
    return d * lax.rsqrt(var + LN_EPS) * g + b


def _rope_body(pos_ref, invf_ref, cos_ref, sin_ref):
    half = RET_DK // 2
    per_row = LANES // half
    ang = pos_ref[...] * invf_ref[...]
    rows = ang.shape[0]
    group = jnp.right_shift(lax.broadcasted_iota(jnp.int32, (rows, LANES), 1), half.bit_length() - 1)
    for table, out_ref in ((jnp.cos(ang), cos_ref), (jnp.sin(ang), sin_ref)):
        for j in range(per_row):
            own = jnp.where(group == j, table, 0.0)
            spread = own
            for k in range(1, per_row):
                spread = spread + pltpu.roll(own, k * half, 1)
            out_ref[pl.ds(j, rows, stride=per_row), :] = spread


def _rope_tables(positions):
    B, S = positions.shape
    half = RET_DK // 2
    per_row = LANES // half
    T = B * S
    rows = T // per_row
    inv_freq = ROPE_THETA ** (-jnp.arange(half, dtype=F32) / half)
    pos_rep = jnp.broadcast_to(positions.astype(F32).reshape(rows, per_row, 1), (rows, per_row, half))
    invf = jnp.tile(inv_freq, per_row).reshape(1, LANES)
    blk = min(rows, ROPE_ROWS)
    cos, sin = pl.pallas_call(
        _rope_body,
        grid=(rows // blk,),
        in_specs=[pl.BlockSpec((blk, LANES), lambda i: (i, 0)),
                  pl.BlockSpec((1, LANES), lambda i: (0, 0))],
        out_specs=[pl.BlockSpec((per_row * blk, LANES), lambda i: (i, 0)),
                   pl.BlockSpec((per_row * blk, LANES), lambda i: (i, 0))],
        out_shape=[jax.ShapeDtypeStruct((T, LANES), F32)] * 2,
        name="rope_tables",
    )(pos_rep.reshape(rows, LANES), invf)
    return cos.reshape(B, S, LANES), sin.reshape(B, S, LANES)


def _ffn_body(has_ple, *refs):
    if has_ple:
        (x_ref, wg_ref, wu_ref, wd_ref, g_ref, b_ref, p_ref, pwg_ref, pwp_ref, o_ref, h_scr) = refs
    else:
        (x_ref, wg_ref, wu_ref, wd_ref, g_ref, b_ref, o_ref, h_scr) = refs
    x = x_ref[...]
    xb = x.astype(BF16)
    for c in range(0, D_FF, FFN_COLS):
        gate = jnp.dot(xb, wg_ref[:, c:c + FFN_COLS], preferred_element_type=F32)
        up = jnp.dot(xb, wu_ref[:, c:c + FFN_COLS], preferred_element_type=F32)
        h_scr[:, c:c + FFN_COLS] = (_silu(gate) * up).astype(BF16)
    r = ALPHA * x
    if has_ple:
        gate = _sigmoid(jnp.dot(xb, pwg_ref[...], preferred_element_type=F32))
        proj = jnp.dot(p_ref[...].astype(BF16), pwp_ref[...], preferred_element_type=F32)
        r = r + gate * proj
    n = x.shape[0]
    slabs = [slice(s, s + n // FFN_SLABS) for s in range(0, n, n // FFN_SLABS)]
    ys = [jnp.dot(h_scr[rows, :], wd_ref[...], preferred_element_type=F32) for rows in slabs]
    for rows, y in zip(slabs, ys):
        o_ref[rows, :] = _layer_norm(r[rows] + 0.5 * y, g_ref[...], b_ref[...])


def _const_spec(shape):
    nd = len(shape)
    return pl.BlockSpec(shape, lambda *_: (0,) * nd, pipeline_mode=pl.Buffered(1))


def _layer_spec(shape, layer):
    nd = len(shape)
    return pl.BlockSpec((None,) + tuple(shape[1:]), lambda *_: (layer,) + (0,) * (nd - 1),
                        pipeline_mode=pl.Buffered(1))


def _ffn(x2, layer, wg, wu, wd, ln_g, ln_b, ple=None):
    T = x2.shape[0]
    rows = min(FFN_ROWS, T)
    tok = lambda n: pl.BlockSpec((rows, n), lambda i: (i, 0))
    args = [x2, wg, wu, wd, ln_g, ln_b]
    specs = [tok(D_MODEL), _layer_spec(wg.shape, layer), _layer_spec(wu.shape, layer), _layer_spec(wd.shape, layer),
             _const_spec(ln_g.shape), _const_spec(ln_b.shape)]
    if ple is not None:
        p_all, pwg, pwp = ple
        args += [p_all, pwg, pwp]
        specs += [pl.BlockSpec((None, rows, PLE_DIM), lambda i: (layer, i, 0)),
                  _layer_spec(pwg.shape, layer), _layer_spec(pwp.shape, layer)]
    return pl.pallas_call(
        functools.partial(_ffn_body, ple is not None),
        grid=(T // rows,),
        in_specs=specs,
        out_specs=tok(D_MODEL),
        out_shape=jax.ShapeDtypeStruct((T, D_MODEL), F32),
        scratch_shapes=[pltpu.VMEM((rows, D_FF), BF16)],
        compiler_params=pltpu.CompilerParams(dimension_semantics=("arbitrary",),
                                             vmem_limit_bytes=VMEM_LIMIT_BYTES),
        name="ffn_ple" if ple is not None else "ffn",
    )(*args)


def _iota2(shape, dim):
    return lax.broadcasted_iota(jnp.int32, shape, dim)


def _drain(gen):
    for _ in gen:
        pass


def _retention_steps(proj, cos, sin, consts, norm_g, state_ref, out):
    mk_ref, mv_ref, decay_ref, cross_ref, tail_ref, sdec_ref, bd_ref, avg_ref = consts

    def rot(t):
        t1, t2 = t[:, :LANES], t[:, LANES:]
        return jnp.concatenate([t1 * cos - t2 * sin, t2 * cos + t1 * sin], axis=1)

    q_raw = proj(C_RQ, C_RK)
    k_raw = proj(C_RK, C_RV)
    yield
    v_all = proj(C_RV, C_RG)
    gate = proj(C_RG, C_LX)
    yield
    q_all = rot(q_raw) * (RET_DK ** -0.5)
    k_all = rot(k_raw)
    os = []
    for sb in range(q_all.shape[0] // RET_BLOCK):
        rows = slice(sb * RET_BLOCK, (sb + 1) * RET_BLOCK)
        q, k, v = q_all[rows], k_all[rows], v_all[rows]
        state = state_ref[...]
        o = _mm(q * cross_ref[...], state)
        state_ref[...] = state * sdec_ref[...] + _mm_tn(k * tail_ref[...], v) * bd_ref[...]
        yield
        qb, kb, vb = q.astype(BF16), k.astype(BF16), v.astype(BF16)
        for h in range(RET_HEADS):
            s = _mm_nt(qb * mk_ref[h:h + 1, :].astype(BF16), kb) * decay_ref[h]
            o = o + _mm(s, vb * mv_ref[h:h + 1, :].astype(BF16))
            yield
        os.append(o)
    o = jnp.concatenate(os, axis=0)
    avg = avg_ref[...]
    d = o - _mm(o, avg)
    var = _mm(d * d, avg)
    out["ret"] = d * lax.rsqrt(var + 1e-5) * norm_g * _silu(gate)
    yield


def _conv_taps(buf_ref, w_ref, r0, n, c0, c1):
    base = SUBLANES + r0
    y = buf_ref[base:base + n, c0:c1] * w_ref[CONV_W - 1:CONV_W, c0:c1]
    for k in range(CONV_W - 1):
        off = base - (CONV_W - 1) + k
        y = y + buf_ref[off:off + n, c0:c1] * w_ref[k:k + 1, c0:c1]
    return y


def _rg_lru_conv(buf_ref, conv_w_ref, conv_b, wax_ref, L):
    coeffs = []
    for p in range(LRU_W // LANES):
        c0, c1 = p * LANES, (p + 1) * LANES
        xc = _conv_taps(buf_ref, conv_w_ref, 0, L, c0, c1) + conv_b[:, c0:c1]
        y = _mm(xc, wax_ref[p])
        coeffs.append((xc, y, c0, c1))
    return coeffs


def _rg_lru_gates(coeffs, b_a, b_x, lam):
    out = []
    for xc, y, c0, c1 in coeffs:
        r = _sigmoid(y[:, :LANES] + b_a[:, c0:c1])
        i = _sigmoid(y[:, LANES:] + b_x[:, c0:c1])
        log_a = (LRU_C * (-_softplus(-lam[:, c0:c1]))) * r
        t = jnp.tanh(log_a)
        out.append((jnp.exp(log_a), _sqrt_nonneg(-2.0 * t / (1.0 - t)) * (i * xc)))
    return out


def _rg_lru_scan(a, b, h_ref, c0, c1, L):
    G = L // SUBLANES
    a3 = a.reshape(G, SUBLANES, LANES)
    b3 = b.reshape(G, SUBLANES, LANES)
    sub = _iota2((G, SUBLANES, LANES), 1)
    for d in (1, 2, 4):
        ok = sub >= d
        a_s = pltpu.roll(a3, d, 1)
        b_s = pltpu.roll(b3, d, 1)
        b3 = b3 + jnp.where(ok, a3 * b_s, 0.0)
        a3 = jnp.where(ok, a3 * a_s, a3)
    a_last = jnp.broadcast_to(a3[:, SUBLANES - 1:SUBLANES, :], a3.shape)
    b_last = jnp.broadcast_to(b3[:, SUBLANES - 1:SUBLANES, :], b3.shape)
    carry = h_ref[:, c0:c1]
    hs = []
    for g in range(G):
        hs.append(b3[g] + a3[g] * carry)
        carry = b_last[g] + a_last[g] * carry
    h_ref[:, c0:c1] = carry
    return jnp.concatenate(hs, axis=0)


def _gdn_gates(ab, a_log_row, dt_row, tri_ref, e_gc_ref, e_beta_ref, L):
    g_n = -jnp.exp(a_log_row) * _softplus(ab + dt_row)
    tri = tri_ref[...]
    gc_n = jnp.concatenate([_mm_sel(tri, g_n[s * SUPER:(s + 1) * SUPER], 2) for s in range(L // SUPER)], axis=0)
    gc_h = _mm_xsel(gc_n, e_gc_ref[...], 2)
    beta_pair = _mm_xsel(_sigmoid(ab), e_beta_ref[...], 2)
    return gc_h, beta_pair


def _gdn_masks():
    lane = _iota2((SUPER, LANES), 1)
    row = _iota2((SUPER, LANES), 0)
    chunk_bits = CHUNK.bit_length() - 1
    same_chunk = jnp.right_shift(row, chunk_bits) == jnp.right_shift(lane, chunk_bits)
    first_half = lane < CHUNK
    return dict(
        incl=jnp.logical_and(same_chunk, row >= lane), strict=jnp.logical_and(same_chunk, row > lane),
        eye=row == lane, first_half=first_half,
        masks=(jnp.where(first_half, 1.0, 0.0).astype(F32)[0:1, :], jnp.where(first_half, 0.0, 1.0).astype(F32)[0:1, :]))


def _gdn_prep(ctx, sc, buf_ref, conv_w_ref):
    W = GDN_W
    qkv = _silu(_conv_taps(buf_ref, conv_w_ref, sc * SUPER, SUPER, 0, 3 * W))
    bd2b = ctx["bd2"].astype(BF16)
    q, k, v = [], [], []
    for p in range(W // LANES):
        qk = jnp.concatenate([qkv[:, p * LANES:(p + 1) * LANES], qkv[:, W + p * LANES:W + (p + 1) * LANES]], axis=0)
        qk = qk * lax.rsqrt(_mm(qk * qk, bd2b) + 1e-6)
        q.append(qk[:SUPER] * (GDN_DK ** -0.5))
        k.append(qk[SUPER:])
        v.append(qkv[:, 2 * W + p * LANES:2 * W + (p + 1) * LANES])
    ctx["qkv"][sc] = (q, k, v)


def _gdn_parallel_steps(ctx, sc, out):
    rs = slice(sc * SUPER, (sc + 1) * SUPER)
    n_pairs = GDN_W // LANES
    m_a, m_b = ctx["masks"]
    eye, strict, incl = ctx["eye"], ctx["strict"], ctx["incl"]
    q_p, k_p, v_p = ctx["qkv"][sc]
    mb_a, mb_b = m_a.astype(BF16), m_b.astype(BF16)
    gcs = jnp.stack([ctx["gc_h"][rs, h * LANES:(h + 1) * LANES] for h in range(GDN_HEADS)])
    rowterm = jnp.sum(jnp.where(eye[None], gcs, 0.0), axis=1, keepdims=True)
    dec = jnp.exp(jnp.minimum(gcs - rowterm, 0.0))
    kk, qk, rhs, gc_pair, q_dec = [], [], [], [], []
    for p in range(n_pairs):
        kp, qp, vp = k_p[p], q_p[p], v_p[p]
        bp = ctx["beta"][rs, p * LANES:(p + 1) * LANES]
        gcp = jnp.where(ctx["first_half"], gcs[2 * p], gcs[2 * p + 1])
        eg = jnp.exp(gcp)
        kbb = kp * bp
        kbb_b, qp_b = kbb.astype(BF16), qp.astype(BF16)
        y = _mm_nt(jnp.concatenate([kbb_b * mb_a, kbb_b * mb_b, qp_b * mb_a, qp_b * mb_b], axis=0), kp)
        kk += [y[0:SUPER], y[SUPER:2 * SUPER]]
        qk += [y[2 * SUPER:3 * SUPER], y[3 * SUPER:4 * SUPER]]
        r_p = jnp.concatenate([vp * bp, kbb * eg], axis=1).astype(BF16)
        rhs += [r_p * jnp.concatenate([mb_a, mb_a], axis=1), r_p * jnp.concatenate([mb_b, mb_b], axis=1)]
        gc_pair.append(gcp)
        q_dec.append(qp * eg)
    yield
    n_mat = jnp.where(strict[None], jnp.stack(kk) * dec, 0.0)
    qkd = jnp.where(incl[None], jnp.stack(qk) * dec, 0.0)
    x_inv = jnp.where(eye, 1.0, 0.0)[None] - n_mat
    pw = _bmm(n_mat, n_mat)
    yield
    for level in range(5):
        if level < 4:
            pw = pw.astype(BF16)
            y = _bmm(jnp.concatenate([pw, x_inv.astype(BF16)], axis=1), pw)
            pw = y[:, :SUPER]
            x_inv = x_inv + y[:, SUPER:]
        else:
            x_inv = x_inv + _bmm(x_inv, pw)
        yield
    uw = _bmm(x_inv, jnp.stack(rhs))
    out[sc] = dict(
        u=[uw[2 * p, :, :LANES] + uw[2 * p + 1, :, :LANES] for p in range(n_pairs)],
        w=[uw[2 * p, :, LANES:] + uw[2 * p + 1, :, LANES:] for p in range(n_pairs)],
        qkd=qkd, gc_pair=gc_pair, q_dec=q_dec, k=k_p)
    yield


def _gdn_recurrence_steps(ctx, par, state_ref, o_rows):
    n_pairs = GDN_W // LANES
    mb_a, mb_b = (m.astype(BF16) for m in ctx["masks"])
    bd2 = ctx["bd2"]
    zeros = jnp.zeros((CHUNK, LANES), BF16)
    for c in range(SUPER // CHUNK):
        cs = slice(c * CHUNK, (c + 1) * CHUNK)
        last = c * CHUNK + CHUNK - 1
        ys = [_mm(jnp.concatenate([par["w"][p][cs], par["q_dec"][p][cs]], axis=0), state_ref[p])
              for p in range(n_pairs)]
        yield
        for p in range(n_pairs):
            gcp = par["gc_pair"][p]
            gl_row = gcp[last:last + 1, :]
            v_new = par["u"][p][cs] - ys[p][:CHUNK]
            k_tail = par["k"][p][cs] * jnp.exp(gl_row - gcp[cs])
            v_new_b = v_new.astype(BF16)
            vm_a, vm_b = v_new_b * mb_a, v_new_b * mb_b
            v_cat = jnp.concatenate([vm_a, zeros, vm_b, zeros] if c == 0 else [zeros, vm_a, zeros, vm_b], axis=0)
            q_cat = jnp.concatenate([par["qkd"][2 * p][cs], par["qkd"][2 * p + 1][cs]], axis=1)
            o_rows[p].append(ys[p][CHUNK:] + _mm(q_cat, v_cat))
            state_ref[p] = state_ref[p] * jnp.exp(gl_row) + _mm_tn(k_tail, v_new_b) * bd2
        yield


def _mixer_body(L, x_ref, cos_ref, sin_ref, win_ref, wout_ref,
                mk_ref, mv_ref, decay_ref, cross_ref, tail_ref, sdec_ref, bdr_ref, avg_ref, rng_ref,
                lcw_ref, lcb_ref, wax_ref, lba_ref, lbx_ref, lam_ref,
                gcw_ref, galog_ref, gdt_ref, gng_ref, tri_ref, egc_ref, ebeta_ref, bd2_ref, avg2_ref,
                lng_ref, lnb_ref,
                o_ref, ret_state, gdn_state, lru_h, lru_buf, gdn_buf):
    @pl.when(pl.program_id(1) == 0)
    def _reset():
        ret_state[...] = jnp.zeros_like(ret_state)
        gdn_state[...] = jnp.zeros_like(gdn_state)
        lru_h[...] = jnp.zeros_like(lru_h)
        lru_buf[...] = jnp.zeros_like(lru_buf)
        gdn_buf[...] = jnp.zeros_like(gdn_buf)

    x = x_ref[...]
    xb = x.astype(BF16)

    def proj(lo, hi):
        return jnp.dot(xb, win_ref[:, lo:hi], preferred_element_type=F32)

    n_pairs = GDN_W // LANES
    n_lru = LRU_W // LANES
    n_super = L // SUPER
    gdn_buf[SUBLANES:SUBLANES + L, :] = proj(C_GQ, C_GZ)
    ab = proj(C_AB, D_IN_PAD)
    lru_buf[SUBLANES:SUBLANES + L, :] = proj(C_LX, C_LG)
    lru_gate = proj(C_LG, C_GQ)
    gc_h, beta_pair = _gdn_gates(ab, galog_ref[...], gdt_ref[...], tri_ref, egc_ref, ebeta_ref, L)
    ctx = dict(gc_h=gc_h, beta=beta_pair, bd2=bd2_ref[...], qkv={}, **_gdn_masks())
    _gdn_prep(ctx, 0, gdn_buf, gcw_ref)
    lru_pre = _rg_lru_conv(lru_buf, lcw_ref, lcb_ref[...], wax_ref, L)
    heads = {}
    ret_gen = _retention_steps(proj, cos_ref[...], sin_ref[...],
                               (mk_ref, mv_ref, decay_ref, cross_ref, tail_ref, sdec_ref, bdr_ref, avg_ref),
                               rng_ref[...], ret_state, heads)
    next(ret_gen)
    lru_ab = _rg_lru_gates(lru_pre, lba_ref[...], lbx_ref[...], lam_ref[...])
    next(ret_gen)
    if n_super > 1:
        _gdn_prep(ctx, 1, gdn_buf, gcw_ref)
    par = {}
    o_rows = [[] for _ in range(n_pairs)]
    lru_cols = []
    mix_l = []

    def lru_out_steps():
        o_l = (jnp.concatenate(lru_cols, axis=1) * _gelu_tanh(lru_gate)).astype(BF16)
        for c in range(0, D_MODEL, RET_W):
            mix_l.append(jnp.dot(o_l, wout_ref[RET_W:RET_W + LRU_W, c:c + RET_W], preferred_element_type=F32))
            yield

    _drain(_gdn_parallel_steps(ctx, 0, par))
    for sc in range(n_super):
        if sc + 2 < n_super:
            _gdn_prep(ctx, sc + 2, gdn_buf, gcw_ref)
        if len(lru_cols) < n_lru:
            t = len(lru_cols)
            lru_cols.append(_rg_lru_scan(lru_ab[t][0], lru_ab[t][1], lru_h, t * LANES, (t + 1) * LANES, L))
        nxt = _gdn_parallel_steps(ctx, sc + 1, par) if sc + 1 < n_super else iter(())
        ret_steps = 0 if sc + 2 < n_super else (1 if sc + 1 < n_super else 2)
        if sc + 1 == n_super:
            while len(lru_cols) < n_lru:
                t = len(lru_cols)
                lru_cols.append(_rg_lru_scan(lru_ab[t][0], lru_ab[t][1], lru_h, t * LANES, (t + 1) * LANES, L))
            lru_out = lru_out_steps()
        for _ in _gdn_recurrence_steps(ctx, par[sc], gdn_state, o_rows):
            next(nxt, None)
            next(nxt, None)
            for _ in range(ret_steps):
                next(ret_gen, None)
            if sc + 1 == n_super:
                next(lru_out, None)
        _drain(nxt)
    _drain(ret_gen)
    _drain(lru_out)
    gdn_buf[0:SUBLANES, :] = gdn_buf[L:L + SUBLANES, :]
    lru_buf[0:SUBLANES, :] = lru_buf[L:L + SUBLANES, :]

    mix = _mm(heads["ret"], wout_ref[0:RET_W, :]) + jnp.concatenate(mix_l, axis=1)
    avg2 = avg2_ref[...]
    z = proj(C_GZ, C_AB)
    outs = []
    for p in range(n_pairs):
        op = jnp.concatenate(o_rows[p], axis=0)
        outs.append(op * lax.rsqrt(_mm(op * op, avg2) + 1e-6))
    o_g = jnp.concatenate(outs, axis=1) * gng_ref[...] * _silu(z)
    w_g = wout_ref[RET_W + LRU_W:, :]
    slabs = [slice(r, r + RET_BLOCK) for r in range(0, L, RET_BLOCK)]
    mix_g = [_mm(o_g[rows], w_g) for rows in slabs]
    for rows, m_g in zip(slabs, mix_g):
        o_ref[rows, :] = _layer_norm(ALPHA * x[rows] + mix[rows] + m_g, lng_ref[...], lnb_ref[...])


def _mixer_constants():
    L = RET_BLOCK
    lg = jnp.log1p(-jnp.exp2(-5.0 - jnp.arange(RET_HEADS, dtype=F32)))
    lane = np.arange(RET_W)
    khead = (lane % LANES) // (RET_DK // 2)
    vhead = lane // RET_DK
    mk = jnp.asarray(np.pad((khead[None, :] == np.arange(RET_HEADS)[:, None]).astype(np.float32), ((0, 4), (0, 0))))
    mv = jnp.asarray(np.pad((vhead[None, :] == np.arange(RET_HEADS)[:, None]).astype(np.float32), ((0, 4), (0, 0))))
    idx = np.arange(L)
    diff = jnp.asarray((idx[:, None] - idx[None, :]).astype(np.float32))
    same = jnp.asarray(idx[:, None] // CHUNK == idx[None, :] // CHUNK)
    earlier = jnp.asarray(idx[None, :] // CHUNK < idx[:, None] // CHUNK)
    decay = jnp.where(same[None], jnp.exp(jnp.abs(diff)[None] * lg[:, None, None]),
                      jnp.where(earlier[None], jnp.exp(diff[None] * lg[:, None, None]), 0.0))
    lg_lane = lg[jnp.asarray(khead)]
    fidx = jnp.asarray(idx.astype(np.float32))
    cross = jnp.exp((fidx + 1.0)[:, None] * lg_lane[None, :])
    tail = jnp.exp((L - 1.0 - fidx)[:, None] * lg_lane[None, :])
    sdec = jnp.broadcast_to(jnp.exp(float(L) * lg_lane)[:, None], (RET_W, RET_W))
    bdr = jnp.asarray((khead[:, None] == vhead[None, :]).astype(np.float32))
    avg = jnp.asarray((vhead[:, None] == vhead[None, :]).astype(np.float32) / RET_DK).astype(BF16)
    sidx = np.arange(SUPER)
    tri = jnp.asarray(np.logical_and(sidx[:, None] // CHUNK == sidx[None, :] // CHUNK,
                                     sidx[:, None] >= sidx[None, :]).astype(np.float32)).astype(BF16)
    e_gc = np.zeros((LANES, GDN_HEADS * LANES), np.float32)
    e_beta = np.zeros((LANES, GDN_W), np.float32)
    for h in range(GDN_HEADS):
        e_gc[h, h * LANES:(h + 1) * LANES] = 1.0
        e_beta[GDN_HEADS + h, h * GDN_DK:(h + 1) * GDN_DK] = 1.0
    l2 = np.arange(LANES) // GDN_DK
    bd2 = (l2[:, None] == l2[None, :]).astype(np.float32)
    return dict(mk=mk, mv=mv, decay=decay, cross=cross, tail=tail, sdec=sdec, bdr=bdr, avg=avg, tri=tri,
                e_gc=jnp.asarray(e_gc).astype(BF16), e_beta=jnp.asarray(e_beta).astype(BF16),
                bd2=jnp.asarray(bd2), avg2=jnp.asarray(bd2 / GDN_DK).astype(BF16))


def _mixer(x, cos, sin, w, consts):
    B, S, _ = x.shape
    L = min(MIX_BLOCK, S)
    tok = lambda n: pl.BlockSpec((None, L, n), lambda b, j: (b, j, 0))
    c = consts
    operands = [
        (x, tok(D_MODEL)), (cos, tok(LANES)), (sin, tok(LANES)),
        (w["w_in"], _layer_spec(w["w_in"].shape, w["layer"])), (w["w_out"], _layer_spec(w["w_out"].shape, w["layer"])),
        (c["mk"], None), (c["mv"], None), (c["decay"], None), (c["cross"], None), (c["tail"], None),
        (c["sdec"], None), (c["bdr"], None), (c["avg"], None), (w["ret_norm_g"], None),
        (w["lru_conv_w"], None), (w["lru_conv_b"], None), (w["lru_wax"], None), (w["lru_b_a"], None),
        (w["lru_b_x"], None), (w["lru_lambda"], None),
        (w["gdn_conv_w"], None), (w["gdn_a_log"], None), (w["gdn_dt_bias"], None), (w["gdn_norm_g"], None),
        (c["tri"], None), (c["e_gc"], None), (c["e_beta"], None), (c["bd2"], None), (c["avg2"], None),
        (w["ln_g"], None), (w["ln_b"], None),
    ]
    args = [a for a, _ in operands]
    specs = [s if s is not None else _const_spec(a.shape) for a, s in operands]
    return pl.pallas_call(
        functools.partial(_mixer_body, L),
        grid=(B, S // L),
        in_specs=specs,
        out_specs=tok(D_MODEL),
        out_shape=jax.ShapeDtypeStruct((B, S, D_MODEL), F32),
        scratch_shapes=[
            pltpu.VMEM((RET_W, RET_W), F32),
            pltpu.VMEM((GDN_W // LANES, LANES, LANES), F32),
            pltpu.VMEM((SUBLANES, LRU_W), F32),
            pltpu.VMEM((L + SUBLANES, LRU_W), F32),
            pltpu.VMEM((L + SUBLANES, 3 * GDN_W), F32),
        ],
        compiler_params=pltpu.CompilerParams(dimension_semantics=("arbitrary", "arbitrary"),
                                             vmem_limit_bytes=VMEM_LIMIT_BYTES),
        name="mixer",
    )(*args)


def _relayout_w_in(w):
    n, d = w.shape[0], w.shape[1]
    half = RET_DK // 2

    def halves_apart(cols):
        return cols.reshape(n, d, RET_HEADS, 2, half).transpose(0, 1, 3, 2, 4).reshape(n, d, RET_W)

    parts = [halves_apart(w[..., 0:RET_W]), halves_apart(w[..., RET_W:2 * RET_W]), w[..., 2 * RET_W:],
             jnp.zeros((n, d, D_IN_PAD - D_IN), w.dtype)]
    return jnp.concatenate(parts, axis=2).astype(BF16)


def _pair_block_diag(w):
    z = jnp.zeros((LRU_BLOCK, LRU_BLOCK), w.dtype)
    slabs = []
    for p in range(LRU_W // LANES):
        top = jnp.concatenate([w[2 * p], z], axis=1)
        bot = jnp.concatenate([z, w[2 * p + 1]], axis=1)
        slabs.append(jnp.concatenate([top, bot], axis=0))
    return jnp.stack(slabs)


def _pad_rows(a, rows):
    return jnp.pad(a, ((0, rows - a.shape[0]), (0, 0)))


def _head_row(v):
    return jnp.pad(v, (0, LANES - v.shape[0])).reshape(1, LANES)


def kernel(x, p, positions, ln_ffn1_g, ln_ffn1_b, ffn1_w_gate, ffn1_w_up, ffn1_w_down, w_in, ret_norm_g,
           lru_conv_w, lru_conv_b, lru_w_a, lru_b_a, lru_w_x, lru_b_x, lru_lambda, gdn_conv_w, gdn_a_log,
           gdn_dt_bias, gdn_norm_g, w_out, ln_mix_g, ln_mix_b, ffn2_w_gate, ffn2_w_up, ffn2_w_down,
           ple_w_gate, ple_w_proj, ln_ffn2_g, ln_ffn2_b):
    B, S, _ = x.shape
    T = B * S
    row = lambda v: v.reshape(1, -1)
    cos, sin = _rope_tables(positions)
    consts = _mixer_constants()
    bf = lambda w: w.astype(BF16)
    ffn1_w, ffn2_w = (bf(ffn1_w_gate), bf(ffn1_w_up), bf(ffn1_w_down)), (bf(ffn2_w_gate), bf(ffn2_w_up), bf(ffn2_w_down))
    ple_w = (bf(ple_w_gate), bf(ple_w_proj))
    w_in_all, w_out_all = _relayout_w_in(w_in), bf(w_out)
    for i in range(DEPTH):
        x2 = _ffn(x.reshape(T, D_MODEL), i, *ffn1_w, row(ln_ffn1_g[i]), row(ln_ffn1_b[i]))
        mix_w = dict(
            layer=i, w_in=w_in_all, w_out=w_out_all,
            ret_norm_g=row(ret_norm_g[i]),
            lru_conv_w=_pad_rows(lru_conv_w[i], SUBLANES), lru_conv_b=row(lru_conv_b[i]),
            lru_wax=jnp.concatenate([_pair_block_diag(lru_w_a[i]), _pair_block_diag(lru_w_x[i])], axis=2).astype(BF16),
            lru_b_a=row(lru_b_a[i]), lru_b_x=row(lru_b_x[i]), lru_lambda=row(lru_lambda[i]),
            gdn_conv_w=_pad_rows(gdn_conv_w[i], SUBLANES),
            gdn_a_log=_head_row(gdn_a_log[i]), gdn_dt_bias=_head_row(gdn_dt_bias[i]),
            gdn_norm_g=row(jnp.tile(gdn_norm_g[i], GDN_HEADS)),
            ln_g=row(ln_mix_g[i]), ln_b=row(ln_mix_b[i]),
        )
        x3 = _mixer(x2.reshape(B, S, D_MODEL), cos, sin, mix_w, consts)
        x = _ffn(x3.reshape(T, D_MODEL), i, *ffn2_w, row(ln_ffn2_g[i]), row(ln_ffn2_b[i]),
                 ple=(p.reshape(DEPTH, T, PLE_DIM),) + ple_w)
        x = x.reshape(B, S, D_MODEL)
    return x
```

```python
import functools

import numpy as np
import jax
import jax.numpy as jnp
from jax import lax
from jax.experimental import pallas as pl
from jax.experimental.pallas import tpu as pltpu

F32 = jnp.float32
BF16 = jnp.bfloat16

D_MODEL = 1024
DEPTH = 2
CHUNK = 64
PLE_DIM = 256
D_FF = 2816
CONV_W = 4
RET_HEADS = 4
RET_DK = 64
RET_W = 256
LRU_W = 384
LRU_BLOCK = 64
LRU_C = 8.0
GDN_HEADS = 6
GDN_DK = 64
GDN_W = 384
IN_WIDTHS = (RET_W, RET_W, RET_W, RET_W, LRU_W, LRU_W, GDN_W, GDN_W, GDN_W, GDN_W, GDN_HEADS, GDN_HEADS)
D_IN = sum(IN_WIDTHS)
ROPE_THETA = 10000.0
ALPHA = (2 * DEPTH) ** 0.25
LN_EPS = 1e-5

LANES = 128
SUBLANES = 8
VMEM_LIMIT_BYTES = 56 * 1024 * 1024

MIX_BLOCK = 512
RET_BLOCK = 256
SUPER = 128
FFN_ROWS = 1024
FFN_COLS = 256
FFN_SLABS = 4
ROPE_ROWS = 1024

C_RQ, C_RK, C_RV, C_RG = 0, 256, 512, 768
C_LX, C_LG = 1024, 1408
C_GQ, C_GK, C_GV, C_GZ = 1792, 2176, 2560, 2944
C_AB = 3328
D_IN_PAD = 3456


def _mm(a, b):
    return jnp.dot(a.astype(BF16), b.astype(BF16), preferred_element_type=F32)


def _mm_nt(a, b):
    return lax.dot_general(a.astype(BF16), b.astype(BF16), (((1,), (1,)), ((), ())),
                           preferred_element_type=F32)


def _mm_tn(a, b):
    return lax.dot_general(a.astype(BF16), b.astype(BF16), (((0,), (0,)), ((), ())),
                           preferred_element_type=F32)


def _bmm(a, b):
    return lax.dot_general(a.astype(BF16), b.astype(BF16), (((2,), (1,)), ((0,), (0,))),
                           preferred_element_type=F32)


def _split(x, parts):
    out = []
    r = x
    for _ in range(parts):
        t = r.astype(BF16)
        out.append(t)
        r = r - t.astype(F32)
    return out


def _mm_sel(sel, x, parts):
    acc = None
    for t in _split(x, parts):
        y = jnp.dot(sel, t, preferred_element_type=F32)
        acc = y if acc is None else acc + y
    return acc


def _mm_xsel(x, sel, parts):
    acc = None
    for t in _split(x, parts):
        y = jnp.dot(t, sel, preferred_element_type=F32)
        acc = y if acc is None else acc + y
    return acc


LOG2_E = 1.4426950408889634


def _sigmoid(x):
    return 1.0 / (1.0 + jnp.exp2(x * -LOG2_E))


def _silu(x):
    return x * _sigmoid(x)


def _softplus(x):
    return jnp.maximum(x, 0.0) + jnp.log(1.0 + jnp.exp(-jnp.abs(x)))


def _sqrt_nonneg(y):
    return jnp.where(y > 0.0, y * lax.rsqrt(y), 0.0)


def _gelu_tanh(x):
    c = np.float32(np.sqrt(2.0 / np.pi))
    return 0.5 * x * (1.0 + jnp.tanh(c * (x + 0.044715 * (x * x * x))))


def _layer_norm(r, g, b):
    mu = jnp.mean(r, axis=-1, keepdims=True)
    d = r - mu
    var = jnp.mean(d * d, axis=-1, keepdims=True)
    return d * lax.rsqrt(var + LN_EPS) * g + b


def _rope_body(pos_ref, invf_ref, cos_ref, sin_ref):
    half = RET_DK // 2
    per_row = LANES // half
    ang = pos_ref[...] * invf_ref[...]
    rows = ang.shape[0]
    group = jnp.right_shift(lax.broadcasted_iota(jnp.int32, (rows, LANES), 1), half.bit_length() - 1)
    for table, out_ref in ((jnp.cos(ang), cos_ref), (jnp.sin(ang), sin_ref)):
        for j in range(per_row):
            own = jnp.where(group == j, table, 0.0)
            spread = own
            for k in range(1, per_row):
                spread = spread + pltpu.roll(own, k * half, 1)
            out_ref[pl.ds(j, rows, stride=per_row), :] = spread


def _rope_tables(positions):
    B, S = positions.shape
    half = RET_DK // 2
    per_row = LANES // half
    T = B * S
    rows = T // per_row
    inv_freq = ROPE_THETA ** (-jnp.arange(half, dtype=F32) / half)
    pos_rep = jnp.broadcast_to(positions.astype(F32).reshape(rows, per_row, 1), (rows, per_row, half))
    invf = jnp.tile(inv_freq, per_row).reshape(1, LANES)
    blk = min(rows, ROPE_ROWS)
    cos, sin = pl.pallas_call(
        _rope_body,
        grid=(rows // blk,),
        in_specs=[pl.BlockSpec((blk, LANES), lambda i: (i, 0)),
                  pl.BlockSpec((1, LANES), lambda i: (0, 0))],
        out_specs=[pl.BlockSpec((per_row * blk, LANES), lambda i: (i, 0)),
                   pl.BlockSpec((per_row * blk, LANES), lambda i: (i, 0))],
        out_shape=[jax.ShapeDtypeStruct((T, LANES), F32)] * 2,
        name="rope_tables",
    )(pos_rep.reshape(rows, LANES), invf)
    return cos.reshape(B, S, LANES), sin.reshape(B, S, LANES)


def _ffn_body(has_ple, *refs):
    if has_ple:
        (x_ref, wg_ref, wu_ref, wd_ref, g_ref, b_ref, p_ref, pwg_ref, pwp_ref, o_ref, h_scr) = refs
    else:
        (x_ref, wg_ref, wu_ref, wd_ref, g_ref, b_ref, o_ref, h_scr) = refs
    x = x_ref[...]
    xb = x.astype(BF16)
    for c in range(0, D_FF, FFN_COLS):
        gate = jnp.dot(xb, wg_ref[:, c:c + FFN_COLS], preferred_element_type=F32)
        up = jnp.dot(xb, wu_ref[:, c:c + FFN_COLS], preferred_element_type=F32)
        h_scr[:, c:c + FFN_COLS] = (_silu(gate) * up).astype(BF16)
    r = ALPHA * x
    if has_ple:
        gate = _sigmoid(jnp.dot(xb, pwg_ref[...], preferred_element_type=F32))
        proj = jnp.dot(p_ref[...].astype(BF16), pwp_ref[...], preferred_element_type=F32)
        r = r + gate * proj
    n = x.shape[0]
    slabs = [slice(s, s + n // FFN_SLABS) for s in range(0, n, n // FFN_SLABS)]
    ys = [jnp.dot(h_scr[rows, :], wd_ref[...], preferred_element_type=F32) for rows in slabs]
    for rows, y in zip(slabs, ys):
        o_ref[rows, :] = _layer_norm(r[rows] + 0.5 * y, g_ref[...], b_ref[...])


def _const_spec(shape):
    nd = len(shape)
    return pl.BlockSpec(shape, lambda *_: (0,) * nd, pipeline_mode=pl.Buffered(1))


def _layer_spec(shape, layer):
    nd = len(shape)
    return pl.BlockSpec((None,) + tuple(shape[1:]), lambda *_: (layer,) + (0,) * (nd - 1),
                        pipeline_mode=pl.Buffered(1))


def _ffn(x2, layer, wg, wu, wd, ln_g, ln_b, ple=None):
    T = x2.shape[0]
    rows = min(FFN_ROWS, T)
    tok = lambda n: pl.BlockSpec((rows, n), lambda i: (i, 0))
    args = [x2, wg, wu, wd, ln_g, ln_b]
    specs = [tok(D_MODEL), _layer_spec(wg.shape, layer), _layer_spec(wu.shape, layer), _layer_spec(wd.shape, layer),
             _const_spec(ln_g.shape), _const_spec(ln_b.shape)]
    if ple is not None:
        p_all, pwg, pwp = ple
        args += [p_all, pwg, pwp]
        specs += [pl.BlockSpec((None, rows, PLE_DIM), lambda i: (layer, i, 0)),
                  _layer_spec(pwg.shape, layer), _layer_spec(pwp.shape, layer)]
    return pl.pallas_call(
        functools.partial(_ffn_body, ple is not None),
        grid=(T // rows,),
        in_specs=specs,
        out_specs=tok(D_MODEL),
        out_shape=jax.ShapeDtypeStruct((T, D_MODEL), F32),
        scratch_shapes=[pltpu.VMEM((rows, D_FF), BF16)],
        compiler_params=pltpu.CompilerParams(dimension_semantics=("arbitrary",),
                                             vmem_limit_bytes=VMEM_LIMIT_BYTES),
        name="ffn_ple" if ple is not None else "ffn",
    )(*args)


def _iota2(shape, dim):
    return lax.broadcasted_iota(jnp.int32, shape, dim)


def _drain(gen):
    for _ in gen:
        pass


def _retention_steps(proj, cos, sin, consts, norm_g, state_ref, out):
    mk_ref, mv_ref, decay_ref, cross_ref, tail_ref, sdec_ref, bd_ref, avg_ref = consts

    def rot(t):
        t1, t2 = t[:, :LANES], t[:, LANES:]
        return jnp.concatenate([t1 * cos - t2 * sin, t2 * cos + t1 * sin], axis=1)

    q_raw = proj(C_RQ, C_RK)
    k_raw = proj(C_RK, C_RV)
    yield
    v_all = proj(C_RV, C_RG)
    gate = proj(C_RG, C_LX)
    yield
    q_all = rot(q_raw) * (RET_DK ** -0.5)
    k_all = rot(k_raw)
    os = []
    for sb in range(q_all.shape[0] // RET_BLOCK):
        rows = slice(sb * RET_BLOCK, (sb + 1) * RET_BLOCK)
        q, k, v = q_all[rows], k_all[rows], v_all[rows]
        state = state_ref[...]
        o = _mm(q * cross_ref[...], state)
        state_ref[...] = state * sdec_ref[...] + _mm_tn(k * tail_ref[...], v) * bd_ref[...]
        yield
        qb, kb, vb = q.astype(BF16), k.astype(BF16), v.astype(BF16)
        for h in range(RET_HEADS):
            s = _mm_nt(qb * mk_ref[h:h + 1, :].astype(BF16), kb) * decay_ref[h]
            o = o + _mm(s, vb * mv_ref[h:h + 1, :].astype(BF16))
            yield
        os.append(o)
    o = jnp.concatenate(os, axis=0)
    avg = avg_ref[...]
    d = o - _mm(o, avg)
    var = _mm(d * d, avg)
    out["ret"] = d * lax.rsqrt(var + 1e-5) * norm_g * _silu(gate)
    yield


def _conv_taps(buf_ref, w_ref, r0, n, c0, c1):
    base = SUBLANES + r0
    y = buf_ref[base:base + n, c0:c1] * w_ref[CONV_W - 1:CONV_W, c0:c1]
    for k in range(CONV_W - 1):
        off = base - (CONV_W - 1) + k
        y = y + buf_ref[off:off + n, c0:c1] * w_ref[k:k + 1, c0:c1]
    return y


def _rg_lru_conv(buf_ref, conv_w_ref, conv_b, wax_ref, L):
    coeffs = []
    for p in range(LRU_W // LANES):
        c0, c1 = p * LANES, (p + 1) * LANES
        xc = _conv_taps(buf_ref, conv_w_ref, 0, L, c0, c1) + conv_b[:, c0:c1]
        y = _mm(xc, wax_ref[p])
        coeffs.append((xc, y, c0, c1))
    return coeffs


def _rg_lru_gates(coeffs, b_a, b_x, lam):
    out = []
    for xc, y, c0, c1 in coeffs:
        r = _sigmoid(y[:, :LANES] + b_a[:, c0:c1])
        i = _sigmoid(y[:, LANES:] + b_x[:, c0:c1])
        log_a = (LRU_C * (-_softplus(-lam[:, c0:c1]))) * r
        t = jnp.tanh(log_a)
        out.append((jnp.exp(log_a), _sqrt_nonneg(-2.0 * t / (1.0 - t)) * (i * xc)))
    return out


def _rg_lru_scan(a, b, h_ref, c0, c1, L):
    G = L // SUBLANES
    a3 = a.reshape(G, SUBLANES, LANES)
    b3 = b.reshape(G, SUBLANES, LANES)
    sub = _iota2((G, SUBLANES, LANES), 1)
    for d in (1, 2, 4):
        ok = sub >= d
        a_s = pltpu.roll(a3, d, 1)
        b_s = pltpu.roll(b3, d, 1)
        b3 = b3 + jnp.where(ok, a3 * b_s, 0.0)
        a3 = jnp.where(ok, a3 * a_s, a3)
    a_last = jnp.broadcast_to(a3[:, SUBLANES - 1:SUBLANES, :], a3.shape)
    b_last = jnp.broadcast_to(b3[:, SUBLANES - 1:SUBLANES, :], b3.shape)
    carry = h_ref[:, c0:c1]
    hs = []
    for g in range(G):
        hs.append(b3[g] + a3[g] * carry)
        carry = b_last[g] + a_last[g] * carry
    h_ref[:, c0:c1] = carry
    return jnp.concatenate(hs, axis=0)


def _gdn_gates(ab, a_log_row, dt_row, tri_ref, e_gc_ref, e_beta_ref, L):
    g_n = -jnp.exp(a_log_row) * _softplus(ab + dt_row)
    tri = tri_ref[...]
    gc_n = jnp.concatenate([_mm_sel(tri, g_n[s * SUPER:(s + 1) * SUPER], 2) for s in range(L // SUPER)], axis=0)
    gc_h = _mm_xsel(gc_n, e_gc_ref[...], 2)
    beta_pair = _mm_xsel(_sigmoid(ab), e_beta_ref[...], 2)
    return gc_h, beta_pair


def _gdn_masks():
    lane = _iota2((SUPER, LANES), 1)
    row = _iota2((SUPER, LANES), 0)
    chunk_bits = CHUNK.bit_length() - 1
    same_chunk = jnp.right_shift(row, chunk_bits) == jnp.right_shift(lane, chunk_bits)
    first_half = lane < CHUNK

    def block(t, size):
        return jnp.right_shift(t, size.bit_length() - 1)

    def merge_mask(m):
        lower_left = jnp.logical_and(jnp.bitwise_and(block(row, m), 1) == 1, jnp.bitwise_and(block(lane, m), 1) == 0)
        return jnp.where(jnp.logical_and(block(row, 2 * m) == block(lane, 2 * m), lower_left), 1.0, 0.0).astype(F32)

    return dict(
        incl=jnp.logical_and(same_chunk, row >= lane), strict=jnp.logical_and(same_chunk, row > lane),
        eye=row == lane, first_half=first_half,
        diag8=jnp.where(block(row, SUBLANES) == block(lane, SUBLANES), 1.0, 0.0).astype(F32),
        merge_masks=tuple(merge_mask(m) for m in (8, 16, 32)),
        masks=(jnp.where(first_half, 1.0, 0.0).astype(F32)[0:1, :], jnp.where(first_half, 0.0, 1.0).astype(F32)[0:1, :]))


def _gdn_prep(ctx, sc, buf_ref, conv_w_ref):
    W = GDN_W
    qkv = _silu(_conv_taps(buf_ref, conv_w_ref, sc * SUPER, SUPER, 0, 3 * W))
    bd2b = ctx["bd2"].astype(BF16)
    q, k, v = [], [], []
    for p in range(W // LANES):
        qk = jnp.concatenate([qkv[:, p * LANES:(p + 1) * LANES], qkv[:, W + p * LANES:W + (p + 1) * LANES]], axis=0)
        qk = qk * lax.rsqrt(_mm(qk * qk, bd2b) + 1e-6)
        q.append(qk[:SUPER] * (GDN_DK ** -0.5))
        k.append(qk[SUPER:])
        v.append(qkv[:, 2 * W + p * LANES:2 * W + (p + 1) * LANES])
    ctx["qkv"][sc] = (q, k, v)


def _gdn_parallel_steps(ctx, sc, out):
    rs = slice(sc * SUPER, (sc + 1) * SUPER)
    n_pairs = GDN_W // LANES
    m_a, m_b = ctx["masks"]
    eye, strict, incl = ctx["eye"], ctx["strict"], ctx["incl"]
    q_p, k_p, v_p = ctx["qkv"][sc]
    mb_a, mb_b = m_a.astype(BF16), m_b.astype(BF16)
    gcs = jnp.stack([ctx["gc_h"][rs, h * LANES:(h + 1) * LANES] for h in range(GDN_HEADS)])
    rowterm = jnp.sum(jnp.where(eye[None], gcs, 0.0), axis=1, keepdims=True)
    dec = jnp.exp(jnp.minimum(gcs - rowterm, 0.0))
    kk, qk, rhs, gc_pair, q_dec = [], [], [], [], []
    for p in range(n_pairs):
        kp, qp, vp = k_p[p], q_p[p], v_p[p]
        bp = ctx["beta"][rs, p * LANES:(p + 1) * LANES]
        gcp = jnp.where(ctx["first_half"], gcs[2 * p], gcs[2 * p + 1])
        eg = jnp.exp(gcp)
        kbb = kp * bp
        kbb_b, qp_b = kbb.astype(BF16), qp.astype(BF16)
        y = _mm_nt(jnp.concatenate([kbb_b * mb_a, kbb_b * mb_b, qp_b * mb_a, qp_b * mb_b], axis=0), kp)
        kk += [y[0:SUPER], y[SUPER:2 * SUPER]]
        qk += [y[2 * SUPER:3 * SUPER], y[3 * SUPER:4 * SUPER]]
        r_p = jnp.concatenate([vp * bp, kbb * eg], axis=1).astype(BF16)
        rhs += [r_p * jnp.concatenate([mb_a, mb_a], axis=1), r_p * jnp.concatenate([mb_b, mb_b], axis=1)]
        gc_pair.append(gcp)
        q_dec.append(qp * eg)
    yield
    n_mat = jnp.where(strict[None], jnp.stack(kk) * dec, 0.0)
    qkd = jnp.where(incl[None], jnp.stack(qk) * dec, 0.0)
    n8 = n_mat * ctx["diag8"]
    x_inv = jnp.where(eye, 1.0, 0.0)[None] - n8
    pw = _bmm(n8, n8)
    yield
    pw = pw.astype(BF16)
    y = _bmm(jnp.concatenate([pw, x_inv.astype(BF16)], axis=1), pw)
    x_inv = x_inv + y[:, SUPER:]
    yield
    x_inv = x_inv + _bmm(x_inv, y[:, :SUPER])
    yield
    for off in ctx["merge_masks"]:
        t_a = _bmm(x_inv, n_mat * off)
        yield
        x_inv = x_inv - _bmm(t_a, x_inv)
        yield
    uw = _bmm(x_inv, jnp.stack(rhs))
    out[sc] = dict(
        u=[uw[2 * p, :, :LANES] + uw[2 * p + 1, :, :LANES] for p in range(n_pairs)],
        w=[uw[2 * p, :, LANES:] + uw[2 * p + 1, :, LANES:] for p in range(n_pairs)],
        qkd=qkd, gc_pair=gc_pair, q_dec=q_dec, k=k_p)
    yield


def _gdn_recurrence_steps(ctx, par, state_ref, o_rows):
    n_pairs = GDN_W // LANES
    mb_a, mb_b = (m.astype(BF16) for m in ctx["masks"])
    bd2 = ctx["bd2"]
    zeros = jnp.zeros((CHUNK, LANES), BF16)
    for c in range(SUPER // CHUNK):
        cs = slice(c * CHUNK, (c + 1) * CHUNK)
        last = c * CHUNK + CHUNK - 1
        ys = [_mm(jnp.concatenate([par["w"][p][cs], par["q_dec"][p][cs]], axis=0), state_ref[p])
              for p in range(n_pairs)]
        yield
        for p in range(n_pairs):
            gcp = par["gc_pair"][p]
            gl_row = gcp[last:last + 1, :]
            v_new = par["u"][p][cs] - ys[p][:CHUNK]
            k_tail = par["k"][p][cs] * jnp.exp(gl_row - gcp[cs])
            v_new_b = v_new.astype(BF16)
            vm_a, vm_b = v_new_b * mb_a, v_new_b * mb_b
            v_cat = jnp.concatenate([vm_a, zeros, vm_b, zeros] if c == 0 else [zeros, vm_a, zeros, vm_b], axis=0)
            q_cat = jnp.concatenate([par["qkd"][2 * p][cs], par["qkd"][2 * p + 1][cs]], axis=1)
            o_rows[p].append(ys[p][CHUNK:] + _mm(q_cat, v_cat))
            state_ref[p] = state_ref[p] * jnp.exp(gl_row) + _mm_tn(k_tail, v_new_b) * bd2
        yield


def _mixer_body(L, x_ref, cos_ref, sin_ref, win_ref, wout_ref,
                mk_ref, mv_ref, decay_ref, cross_ref, tail_ref, sdec_ref, bdr_ref, avg_ref, rng_ref,
                lcw_ref, lcb_ref, wax_ref, lba_ref, lbx_ref, lam_ref,
                gcw_ref, galog_ref, gdt_ref, gng_ref, tri_ref, egc_ref, ebeta_ref, bd2_ref, avg2_ref,
                lng_ref, lnb_ref,
                o_ref, ret_state, gdn_state, lru_h, lru_buf, gdn_buf):
    @pl.when(pl.program_id(1) == 0)
    def _reset():
        ret_state[...] = jnp.zeros_like(ret_state)
        gdn_state[...] = jnp.zeros_like(gdn_state)
        lru_h[...] = jnp.zeros_like(lru_h)
        lru_buf[...] = jnp.zeros_like(lru_buf)
        gdn_buf[...] = jnp.zeros_like(gdn_buf)

    x = x_ref[...]
    xb = x.astype(BF16)

    def proj(lo, hi):
        return jnp.dot(xb, win_ref[:, lo:hi], preferred_element_type=F32)

    n_pairs = GDN_W // LANES
    n_lru = LRU_W // LANES
    n_super = L // SUPER
    gdn_buf[SUBLANES:SUBLANES + L, :] = proj(C_GQ, C_GZ)
    ab = proj(C_AB, D_IN_PAD)
    lru_buf[SUBLANES:SUBLANES + L, :] = proj(C_LX, C_LG)
    lru_gate = proj(C_LG, C_GQ)
    gc_h, beta_pair = _gdn_gates(ab, galog_ref[...], gdt_ref[...], tri_ref, egc_ref, ebeta_ref, L)
    ctx = dict(gc_h=gc_h, beta=beta_pair, bd2=bd2_ref[...], qkv={}, **_gdn_masks())
    _gdn_prep(ctx, 0, gdn_buf, gcw_ref)
    lru_pre = _rg_lru_conv(lru_buf, lcw_ref, lcb_ref[...], wax_ref, L)
    heads = {}
    ret_gen = _retention_steps(proj, cos_ref[...], sin_ref[...],
                               (mk_ref, mv_ref, decay_ref, cross_ref, tail_ref, sdec_ref, bdr_ref, avg_ref),
                               rng_ref[...], ret_state, heads)
    next(ret_gen)
    lru_ab = _rg_lru_gates(lru_pre, lba_ref[...], lbx_ref[...], lam_ref[...])
    next(ret_gen)
    if n_super > 1:
        _gdn_prep(ctx, 1, gdn_buf, gcw_ref)
    par = {}
    o_rows = [[] for _ in range(n_pairs)]
    lru_cols = []
    mix_l = []

    def lru_out_steps():
        o_l = (jnp.concatenate(lru_cols, axis=1) * _gelu_tanh(lru_gate)).astype(BF16)
        for c in range(0, D_MODEL, RET_W):
            mix_l.append(jnp.dot(o_l, wout_ref[RET_W:RET_W + LRU_W, c:c + RET_W], preferred_element_type=F32))
            yield

    _drain(_gdn_parallel_steps(ctx, 0, par))
    for sc in range(n_super):
        if sc + 2 < n_super:
            _gdn_prep(ctx, sc + 2, gdn_buf, gcw_ref)
        if len(lru_cols) < n_lru:
            t = len(lru_cols)
            lru_cols.append(_rg_lru_scan(lru_ab[t][0], lru_ab[t][1], lru_h, t * LANES, (t + 1) * LANES, L))
        nxt = _gdn_parallel_steps(ctx, sc + 1, par) if sc + 1 < n_super else iter(())
        ret_steps = 0 if sc + 2 < n_super else (1 if sc + 1 < n_super else 2)
        if sc + 1 == n_super:
            while len(lru_cols) < n_lru:
                t = len(lru_cols)
                lru_cols.append(_rg_lru_scan(lru_ab[t][0], lru_ab[t][1], lru_h, t * LANES, (t + 1) * LANES, L))
            lru_out = lru_out_steps()
        for _ in _gdn_recurrence_steps(ctx, par[sc], gdn_state, o_rows):
            next(nxt, None)
            next(nxt, None)
            next(nxt, None)
            for _ in range(ret_steps):
                next(ret_gen, None)
            if sc + 1 == n_super:
                next(lru_out, None)
        _drain(nxt)
    _drain(ret_gen)
    _drain(lru_out)
    gdn_buf[0:SUBLANES, :] = gdn_buf[L:L + SUBLANES, :]
    lru_buf[0:SUBLANES, :] = lru_buf[L:L + SUBLANES, :]

    mix = _mm(heads["ret"], wout_ref[0:RET_W, :]) + jnp.concatenate(mix_l, axis=1)
    avg2 = avg2_ref[...]
    z = proj(C_GZ, C_AB)
    outs = []
    for p in range(n_pairs):
        op = jnp.concatenate(o_rows[p], axis=0)
        outs.append(op * lax.rsqrt(_mm(op * op, avg2) + 1e-6))
    o_g = jnp.concatenate(outs, axis=1) * gng_ref[...] * _silu(z)
    w_g = wout_ref[RET_W + LRU_W:, :]
    slabs = [slice(r, r + RET_BLOCK) for r in range(0, L, RET_BLOCK)]
    mix_g = [_mm(o_g[rows], w_g) for rows in slabs]
    for rows, m_g in zip(slabs, mix_g):
        o_ref[rows, :] = _layer_norm(ALPHA * x[rows] + mix[rows] + m_g, lng_ref[...], lnb_ref[...])


def _mixer_constants():
    L = RET_BLOCK
    lg = jnp.log1p(-jnp.exp2(-5.0 - jnp.arange(RET_HEADS, dtype=F32)))
    lane = np.arange(RET_W)
    khead = (lane % LANES) // (RET_DK // 2)
    vhead = lane // RET_DK
    mk = jnp.asarray(np.pad((khead[None, :] == np.arange(RET_HEADS)[:, None]).astype(np.float32), ((0, 4), (0, 0))))
    mv = jnp.asarray(np.pad((vhead[None, :] == np.arange(RET_HEADS)[:, None]).astype(np.float32), ((0, 4), (0, 0))))
    idx = np.arange(L)
    diff = jnp.asarray((idx[:, None] - idx[None, :]).astype(np.float32))
    same = jnp.asarray(idx[:, None] // CHUNK == idx[None, :] // CHUNK)
    earlier = jnp.asarray(idx[None, :] // CHUNK < idx[:, None] // CHUNK)
    decay = jnp.where(same[None], jnp.exp(jnp.abs(diff)[None] * lg[:, None, None]),
                      jnp.where(earlier[None], jnp.exp(diff[None] * lg[:, None, None]), 0.0))
    lg_lane = lg[jnp.asarray(khead)]
    fidx = jnp.asarray(idx.astype(np.float32))
    cross = jnp.exp((fidx + 1.0)[:, None] * lg_lane[None, :])
    tail = jnp.exp((L - 1.0 - fidx)[:, None] * lg_lane[None, :])
    sdec = jnp.broadcast_to(jnp.exp(float(L) * lg_lane)[:, None], (RET_W, RET_W))
    bdr = jnp.asarray((khead[:, None] == vhead[None, :]).astype(np.float32))
    avg = jnp.asarray((vhead[:, None] == vhead[None, :]).astype(np.float32) / RET_DK).astype(BF16)
    sidx = np.arange(SUPER)
    tri = jnp.asarray(np.logical_and(sidx[:, None] // CHUNK == sidx[None, :] // CHUNK,
                                     sidx[:, None] >= sidx[None, :]).astype(np.float32)).astype(BF16)
    e_gc = np.zeros((LANES, GDN_HEADS * LANES), np.float32)
    e_beta = np.zeros((LANES, GDN_W), np.float32)
    for h in range(GDN_HEADS):
        e_gc[h, h * LANES:(h + 1) * LANES] = 1.0
        e_beta[GDN_HEADS + h, h * GDN_DK:(h + 1) * GDN_DK] = 1.0
    l2 = np.arange(LANES) // GDN_DK
    bd2 = (l2[:, None] == l2[None, :]).astype(np.float32)
    return dict(mk=mk, mv=mv, decay=decay, cross=cross, tail=tail, sdec=sdec, bdr=bdr, avg=avg, tri=tri,
                e_gc=jnp.asarray(e_gc).astype(BF16), e_beta=jnp.asarray(e_beta).astype(BF16),
                bd2=jnp.asarray(bd2), avg2=jnp.asarray(bd2 / GDN_DK).astype(BF16))


def _mixer(x, cos, sin, w, consts):
    B, S, _ = x.shape
    L = min(MIX_BLOCK, S)
    tok = lambda n: pl.BlockSpec((None, L, n), lambda b, j: (b, j, 0))
    c = consts
    operands = [
        (x, tok(D_MODEL)), (cos, tok(LANES)), (sin, tok(LANES)),
        (w["w_in"], _layer_spec(w["w_in"].shape, w["layer"])), (w["w_out"], _layer_spec(w["w_out"].shape, w["layer"])),
        (c["mk"], None), (c["mv"], None), (c["decay"], None), (c["cross"], None), (c["tail"], None),
        (c["sdec"], None), (c["bdr"], None), (c["avg"], None), (w["ret_norm_g"], None),
        (w["lru_conv_w"], None), (w["lru_conv_b"], None), (w["lru_wax"], None), (w["lru_b_a"], None),
        (w["lru_b_x"], None), (w["lru_lambda"], None),
        (w["gdn_conv_w"], None), (w["gdn_a_log"], None), (w["gdn_dt_bias"], None), (w["gdn_norm_g"], None),
        (c["tri"], None), (c["e_gc"], None), (c["e_beta"], None), (c["bd2"], None), (c["avg2"], None),
        (w["ln_g"], None), (w["ln_b"], None),
    ]
    args = [a for a, _ in operands]
    specs = [s if s is not None else _const_spec(a.shape) for a, s in operands]
    return pl.pallas_call(
        functools.partial(_mixer_body, L),
        grid=(B, S // L),
        in_specs=specs,
        out_specs=tok(D_MODEL),
        out_shape=jax.ShapeDtypeStruct((B, S, D_MODEL), F32),
        scratch_shapes=[
            pltpu.VMEM((RET_W, RET_W), F32),
            pltpu.VMEM((GDN_W // LANES, LANES, LANES), F32),
            pltpu.VMEM((SUBLANES, LRU_W), F32),
            pltpu.VMEM((L + SUBLANES, LRU_W), F32),
            pltpu.VMEM((L + SUBLANES, 3 * GDN_W), F32),
        ],
        compiler_params=pltpu.CompilerParams(dimension_semantics=("arbitrary", "arbitrary"),
                                             vmem_limit_bytes=VMEM_LIMIT_BYTES),
        name="mixer",
    )(*args)


def _relayout_w_in(w):
    n, d = w.shape[0], w.shape[1]
    half = RET_DK // 2

    def halves_apart(cols):
        return cols.reshape(n, d, RET_HEADS, 2, half).transpose(0, 1, 3, 2, 4).reshape(n, d, RET_W)

    parts = [halves_apart(w[..., 0:RET_W]), halves_apart(w[..., RET_W:2 * RET_W]), w[..., 2 * RET_W:],
             jnp.zeros((n, d, D_IN_PAD - D_IN), w.dtype)]
    return jnp.concatenate(parts, axis=2).astype(BF16)


def _pair_block_diag(w):
    z = jnp.zeros((LRU_BLOCK, LRU_BLOCK), w.dtype)
    slabs = []
    for p in range(LRU_W // LANES):
        top = jnp.concatenate([w[2 * p], z], axis=1)
        bot = jnp.concatenate([z, w[2 * p + 1]], axis=1)
        slabs.append(jnp.concatenate([top, bot], axis=0))
    return jnp.stack(slabs)


def _pad_rows(a, rows):
    return jnp.pad(a, ((0, rows - a.shape[0]), (0, 0)))


def _head_row(v):
    return jnp.pad(v, (0, LANES - v.shape[0])).reshape(1, LANES)


def kernel(x, p, positions, ln_ffn1_g, ln_ffn1_b, ffn1_w_gate, ffn1_w_up, ffn1_w_down, w_in, ret_norm_g,
           lru_conv_w, lru_conv_b, lru_w_a, lru_b_a, lru_w_x, lru_b_x, lru_lambda, gdn_conv_w, gdn_a_log,
           gdn_dt_bias, gdn_norm_g, w_out, ln_mix_g, ln_mix_b, ffn2_w_gate, ffn2_w_up, ffn2_w_down,
           ple_w_gate, ple_w_proj, ln_ffn2_g, ln_ffn2_b):
    B, S, _ = x.shape
    T = B * S
    row = lambda v: v.reshape(1, -1)
    cos, sin = _rope_tables(positions)
    consts = _mixer_constants()
    bf = lambda w: w.astype(BF16)
    ffn1_w, ffn2_w = (bf(ffn1_w_gate), bf(ffn1_w_up), bf(ffn1_w_down)), (bf(ffn2_w_gate), bf(ffn2_w_up), bf(ffn2_w_down))
    ple_w = (bf(ple_w_gate), bf(ple_w_proj))
    w_in_all, w_out_all = _relayout_w_in(w_in), bf(w_out)
    for i in range(DEPTH):
        x2 = _ffn(x.reshape(T, D_MODEL), i, *ffn1_w, row(ln_ffn1_g[i]), row(ln_ffn1_b[i]))
        mix_w = dict(
            layer=i, w_in=w_in_all, w_out=w_out_all,
            ret_norm_g=row(ret_norm_g[i]),
            lru_conv_w=_pad_rows(lru_conv_w[i], SUBLANES), lru_conv_b=row(lru_conv_b[i]),
            lru_wax=jnp.concatenate([_pair_block_diag(lru_w_a[i]), _pair_block_diag(lru_w_x[i])], axis=2).astype(BF16),
            lru_b_a=row(lru_b_a[i]), lru_b_x=row(lru_b_x[i]), lru_lambda=row(lru_lambda[i]),
            gdn_conv_w=_pad_rows(gdn_conv_w[i], SUBLANES),
            gdn_a_log=_head_row(gdn_a_log[i]), gdn_dt_bias=_head_row(gdn_dt_bias[i]),
            gdn_norm_g=row(jnp.tile(gdn_norm_g[i], GDN_HEADS)),
            ln_g=row(ln_mix_g[i]), ln_b=row(ln_mix_b[i]),
        )
        x3 = _mixer(x2.reshape(B, S, D_MODEL), cos, sin, mix_w, consts)
        x = _ffn(x3.reshape(T, D_MODEL), i, *ffn2_w, row(ln_ffn2_g[i]), row(ln_ffn2_b[i]),
                 ple=(p.reshape(DEPTH, T, PLE_DIM),) + ple_w)
        x = x.reshape(B, S, D_MODEL)
    return x
```

```python
import functools

import numpy as np
import jax
import jax.numpy as jnp
from jax import lax
from jax.experimental import pallas as pl
from jax.experimental.pallas import tpu as pltpu

F32 = jnp.float32
BF16 = jnp.bfloat16

D_MODEL = 1024
DEPTH = 2
CHUNK = 64
PLE_DIM = 256
D_FF = 2816
CONV_W = 4
RET_HEADS = 4
RET_DK = 64
RET_W = 256
LRU_W = 384
LRU_BLOCK = 64
LRU_C = 8.0
GDN_HEADS = 6
GDN_DK = 64
GDN_W = 384
IN_WIDTHS = (RET_W, RET_W, RET_W, RET_W, LRU_W, LRU_W, GDN_W, GDN_W, GDN_W, GDN_W, GDN_HEADS, GDN_HEADS)
D_IN = sum(IN_WIDTHS)
ROPE_THETA = 10000.0
ALPHA = (2 * DEPTH) ** 0.25
LN_EPS = 1e-5

LANES = 128
SUBLANES = 8
VMEM_LIMIT_BYTES = 56 * 1024 * 1024

MIX_BLOCK = 512
RET_BLOCK = 256
SUPER = 128
FFN_ROWS = 1024
FFN_COLS = 256
FFN_SLABS = 4
ROPE_ROWS = 1024

C_RQ, C_RK, C_RV, C_RG = 0, 256, 512, 768
C_LX, C_LG = 1024, 1408
C_GQ, C_GK, C_GV, C_GZ = 1792, 2176, 2560, 2944
C_AB = 3328
D_IN_PAD = 3456


def _mm(a, b):
    return jnp.dot(a.astype(BF16), b.astype(BF16), preferred_element_type=F32)


def _mm_nt(a, b):
    return lax.dot_general(a.astype(BF16), b.astype(BF16), (((1,), (1,)), ((), ())),
                           preferred_element_type=F32)


def _mm_tn(a, b):
    return lax.dot_general(a.astype(BF16), b.astype(BF16), (((0,), (0,)), ((), ())),
                           preferred_element_type=F32)


def _bmm(a, b):
    return lax.dot_general(a.astype(BF16), b.astype(BF16), (((2,), (1,)), ((0,), (0,))),
                           preferred_element_type=F32)


def _split(x, parts):
    out = []
    r = x
    for _ in range(parts):
        t = r.astype(BF16)
        out.append(t)
        r = r - t.astype(F32)
    return out


def _mm_sel(sel, x, parts):
    acc = None
    for t in _split(x, parts):
        y = jnp.dot(sel, t, preferred_element_type=F32)
        acc = y if acc is None else acc + y
    return acc


def _mm_xsel(x, sel, parts):
    acc = None
    for t in _split(x, parts):
        y = jnp.dot(t, sel, preferred_element_type=F32)
        acc = y if acc is None else acc + y
    return acc


LOG2_E = 1.4426950408889634


def _sigmoid(x):
    return 1.0 / (1.0 + jnp.exp2(x * -LOG2_E))


def _silu(x):
    return x * _sigmoid(x)


def _softplus(x):
    return jnp.maximum(x, 0.0) + jnp.log(1.0 + jnp.exp(-jnp.abs(x)))


def _sqrt_nonneg(y):
    return jnp.where(y > 0.0, y * lax.rsqrt(y), 0.0)


def _gelu_tanh(x):
    c = np.float32(np.sqrt(2.0 / np.pi))
    return 0.5 * x * (1.0 + jnp.tanh(c * (x + 0.044715 * (x * x * x))))


def _layer_norm(r, g, b):
    mu = jnp.mean(r, axis=-1, keepdims=True)
    d = r - mu
    var = jnp.mean(d * d, axis=-1, keepdims=True)
    return d * lax.rsqrt(var + LN_EPS) * g + b


def _rope_body(pos_ref, invf_ref, cos_ref, sin_ref):
    half = RET_DK // 2
    per_row = LANES // half
    ang = pos_ref[...] * invf_ref[...]
    rows = ang.shape[0]
    group = jnp.right_shift(lax.broadcasted_iota(jnp.int32, (rows, LANES), 1), half.bit_length() - 1)
    for table, out_ref in ((jnp.cos(ang), cos_ref), (jnp.sin(ang), sin_ref)):
        for j in range(per_row):
            own = jnp.where(group == j, table, 0.0)
            spread = own
            for k in range(1, per_row):
                spread = spread + pltpu.roll(own, k * half, 1)
            out_ref[pl.ds(j, rows, stride=per_row), :] = spread


def _rope_tables(positions):
    B, S = positions.shape
    half = RET_DK // 2
    per_row = LANES // half
    T = B * S
    rows = T // per_row
    inv_freq = ROPE_THETA ** (-jnp.arange(half, dtype=F32) / half)
    pos_rep = jnp.broadcast_to(positions.astype(F32).reshape(rows, per_row, 1), (rows, per_row, half))
    invf = jnp.tile(inv_freq, per_row).reshape(1, LANES)
    blk = min(rows, ROPE_ROWS)
    cos, sin = pl.pallas_call(
        _rope_body,
        grid=(rows // blk,),
        in_specs=[pl.BlockSpec((blk, LANES), lambda i: (i, 0)),
                  pl.BlockSpec((1, LANES), lambda i: (0, 0))],
        out_specs=[pl.BlockSpec((per_row * blk, LANES), lambda i: (i, 0)),
                   pl.BlockSpec((per_row * blk, LANES), lambda i: (i, 0))],
        out_shape=[jax.ShapeDtypeStruct((T, LANES), F32)] * 2,
        name="rope_tables",
    )(pos_rep.reshape(rows, LANES), invf)
    return cos.reshape(B, S, LANES), sin.reshape(B, S, LANES)


def _ffn_body(has_ple, *refs):
    if has_ple:
        (x_ref, wg_ref, wu_ref, wd_ref, g_ref, b_ref, p_ref, pwg_ref, pwp_ref, o_ref, h_scr) = refs
    else:
        (x_ref, wg_ref, wu_ref, wd_ref, g_ref, b_ref, o_ref, h_scr) = refs
    x = x_ref[...]
    xb = x.astype(BF16)
    for c in range(0, D_FF, FFN_COLS):
        gate = jnp.dot(xb, wg_ref[:, c:c + FFN_COLS], preferred_element_type=F32)
        up = jnp.dot(xb, wu_ref[:, c:c + FFN_COLS], preferred_element_type=F32)
        h_scr[:, c:c + FFN_COLS] = (_silu(gate) * up).astype(BF16)
    r = ALPHA * x
    if has_ple:
        gate = _sigmoid(jnp.dot(xb, pwg_ref[...], preferred_element_type=F32))
        proj = jnp.dot(p_ref[...].astype(BF16), pwp_ref[...], preferred_element_type=F32)
        r = r + gate * proj
    n = x.shape[0]
    slabs = [slice(s, s + n // FFN_SLABS) for s in range(0, n, n // FFN_SLABS)]
    ys = [jnp.dot(h_scr[rows, :], wd_ref[...], preferred_element_type=F32) for rows in slabs]
    for rows, y in zip(slabs, ys):
        o_ref[rows, :] = _layer_norm(r[rows] + 0.5 * y, g_ref[...], b_ref[...])


def _const_spec(shape):
    nd = len(shape)
    return pl.BlockSpec(shape, lambda *_: (0,) * nd, pipeline_mode=pl.Buffered(1))


def _layer_spec(shape, layer):
    nd = len(shape)
    return pl.BlockSpec((None,) + tuple(shape[1:]), lambda *_: (layer,) + (0,) * (nd - 1),
                        pipeline_mode=pl.Buffered(1))


def _ffn(x2, layer, wg, wu, wd, ln_g, ln_b, ple=None):
    T = x2.shape[0]
    rows = min(FFN_ROWS, T)
    tok = lambda n: pl.BlockSpec((rows, n), lambda i: (i, 0))
    args = [x2, wg, wu, wd, ln_g, ln_b]
    specs = [tok(D_MODEL), _layer_spec(wg.shape, layer), _layer_spec(wu.shape, layer), _layer_spec(wd.shape, layer),
             _const_spec(ln_g.shape), _const_spec(ln_b.shape)]
    if ple is not None:
        p_all, pwg, pwp = ple
        args += [p_all, pwg, pwp]
        specs += [pl.BlockSpec((None, rows, PLE_DIM), lambda i: (layer, i, 0)),
                  _layer_spec(pwg.shape, layer), _layer_spec(pwp.shape, layer)]
    return pl.pallas_call(
        functools.partial(_ffn_body, ple is not None),
        grid=(T // rows,),
        in_specs=specs,
        out_specs=tok(D_MODEL),
        out_shape=jax.ShapeDtypeStruct((T, D_MODEL), F32),
        scratch_shapes=[pltpu.VMEM((rows, D_FF), BF16)],
        compiler_params=pltpu.CompilerParams(dimension_semantics=("arbitrary",),
                                             vmem_limit_bytes=VMEM_LIMIT_BYTES),
        name="ffn_ple" if ple is not None else "ffn",
    )(*args)


def _iota2(shape, dim):
    return lax.broadcasted_iota(jnp.int32, shape, dim)


def _drain(gen):
    for _ in gen:
        pass


def _retention_steps(proj, cos, sin, consts, norm_g, state_ref, out):
    mk_ref, mv_ref, decay_ref, cross_ref, tail_ref, sdec_ref, bd_ref, avg_ref = consts

    def rot(t):
        t1, t2 = t[:, :LANES], t[:, LANES:]
        return jnp.concatenate([t1 * cos - t2 * sin, t2 * cos + t1 * sin], axis=1)

    q_raw = proj(C_RQ, C_RK)
    k_raw = proj(C_RK, C_RV)
    yield
    v_all = proj(C_RV, C_RG)
    gate = proj(C_RG, C_LX)
    yield
    q_all = rot(q_raw) * (RET_DK ** -0.5)
    k_all = rot(k_raw)
    os = []
    for sb in range(q_all.shape[0] // RET_BLOCK):
        rows = slice(sb * RET_BLOCK, (sb + 1) * RET_BLOCK)
        q, k, v = q_all[rows], k_all[rows], v_all[rows]
        state = state_ref[...]
        o = _mm(q * cross_ref[...], state)
        state_ref[...] = state * sdec_ref[...] + _mm_tn(k * tail_ref[...], v) * bd_ref[...]
        yield
        qb, kb, vb = q.astype(BF16), k.astype(BF16), v.astype(BF16)
        for h in range(RET_HEADS):
            s = _mm_nt(qb * mk_ref[h:h + 1, :].astype(BF16), kb) * decay_ref[h]
            o = o + _mm(s, vb * mv_ref[h:h + 1, :].astype(BF16))
            yield
        os.append(o)
    o = jnp.concatenate(os, axis=0)
    avg = avg_ref[...]
    d = o - _mm(o, avg)
    var = _mm(d * d, avg)
    out["ret"] = d * lax.rsqrt(var + 1e-5) * norm_g * _silu(gate)
    yield


def _conv_taps(buf_ref, w_ref, r0, n, c0, c1):
    base = SUBLANES + r0
    y = buf_ref[base:base + n, c0:c1] * w_ref[CONV_W - 1:CONV_W, c0:c1]
    for k in range(CONV_W - 1):
        off = base - (CONV_W - 1) + k
        y = y + buf_ref[off:off + n, c0:c1] * w_ref[k:k + 1, c0:c1]
    return y


def _rg_lru_conv(buf_ref, conv_w_ref, conv_b, wax_ref, L):
    coeffs = []
    for p in range(LRU_W // LANES):
        c0, c1 = p * LANES, (p + 1) * LANES
        xc = _conv_taps(buf_ref, conv_w_ref, 0, L, c0, c1) + conv_b[:, c0:c1]
        y = _mm(xc, wax_ref[p])
        coeffs.append((xc, y, c0, c1))
    return coeffs


def _rg_lru_gates(coeffs, b_a, b_x, lam):
    out = []
    for xc, y, c0, c1 in coeffs:
        r = _sigmoid(y[:, :LANES] + b_a[:, c0:c1])
        i = _sigmoid(y[:, LANES:] + b_x[:, c0:c1])
        log_a = (LRU_C * (-_softplus(-lam[:, c0:c1]))) * r
        t = jnp.tanh(log_a)
        out.append((jnp.exp(log_a), _sqrt_nonneg(-2.0 * t / (1.0 - t)) * (i * xc)))
    return out


def _rg_lru_scan(a, b, h_ref, c0, c1, L):
    G = L // SUBLANES
    a3 = a.reshape(G, SUBLANES, LANES)
    b3 = b.reshape(G, SUBLANES, LANES)
    sub = _iota2((G, SUBLANES, LANES), 1)
    for d in (1, 2, 4):
        ok = sub >= d
        a_s = pltpu.roll(a3, d, 1)
        b_s = pltpu.roll(b3, d, 1)
        b3 = b3 + jnp.where(ok, a3 * b_s, 0.0)
        a3 = jnp.where(ok, a3 * a_s, a3)
    a_last = jnp.broadcast_to(a3[:, SUBLANES - 1:SUBLANES, :], a3.shape)
    b_last = jnp.broadcast_to(b3[:, SUBLANES - 1:SUBLANES, :], b3.shape)
    carry = h_ref[:, c0:c1]
    hs = []
    for g in range(G):
        hs.append(b3[g] + a3[g] * carry)
        carry = b_last[g] + a_last[g] * carry
    h_ref[:, c0:c1] = carry
    return jnp.concatenate(hs, axis=0)


def _gdn_gates(ab, a_log_row, dt_row, tri_ref, e_gc_ref, e_beta_ref, L):
    g_n = -jnp.exp(a_log_row) * _softplus(ab + dt_row)
    tri = tri_ref[...]
    gc_n = jnp.concatenate([_mm_sel(tri, g_n[s * SUPER:(s + 1) * SUPER], 2) for s in range(L // SUPER)], axis=0)
    gc_h = _mm_xsel(gc_n, e_gc_ref[...], 2)
    beta_pair = _mm_xsel(_sigmoid(ab), e_beta_ref[...], 2)
    return gc_h, beta_pair


def _gdn_masks():
    lane = _iota2((SUPER, LANES), 1)
    row = _iota2((SUPER, LANES), 0)
    chunk_bits = CHUNK.bit_length() - 1
    same_chunk = jnp.right_shift(row, chunk_bits) == jnp.right_shift(lane, chunk_bits)
    first_half = lane < CHUNK

    def block(t, size):
        return jnp.right_shift(t, size.bit_length() - 1)

    def merge_mask(m):
        lower_left = jnp.logical_and(jnp.bitwise_and(block(row, m), 1) == 1, jnp.bitwise_and(block(lane, m), 1) == 0)
        return jnp.where(jnp.logical_and(block(row, 2 * m) == block(lane, 2 * m), lower_left), 1.0, 0.0).astype(F32)

    return dict(
        incl=jnp.logical_and(same_chunk, row >= lane), strict=jnp.logical_and(same_chunk, row > lane),
        eye=row == lane, first_half=first_half,
        diag8=jnp.where(block(row, SUBLANES) == block(lane, SUBLANES), 1.0, 0.0).astype(F32),
        merge_masks=tuple(merge_mask(m) for m in (8, 16, 32)),
        masks=(jnp.where(first_half, 1.0, 0.0).astype(F32)[0:1, :], jnp.where(first_half, 0.0, 1.0).astype(F32)[0:1, :]))


def _gdn_prep(ctx, sc, buf_ref, conv_w_ref):
    W = GDN_W
    qkv = _silu(_conv_taps(buf_ref, conv_w_ref, sc * SUPER, SUPER, 0, 3 * W))
    bd2b = ctx["bd2"].astype(BF16)
    q, k, v = [], [], []
    for p in range(W // LANES):
        qk = jnp.concatenate([qkv[:, p * LANES:(p + 1) * LANES], qkv[:, W + p * LANES:W + (p + 1) * LANES]], axis=0)
        qk = qk * lax.rsqrt(_mm(qk * qk, bd2b) + 1e-6)
        q.append(qk[:SUPER] * (GDN_DK ** -0.5))
        k.append(qk[SUPER:])
        v.append(qkv[:, 2 * W + p * LANES:2 * W + (p + 1) * LANES])
    ctx["qkv"][sc] = (q, k, v)


def _gdn_parallel_steps(ctx, scs, out):
    n_pairs = GDN_W // LANES
    m_a, m_b = ctx["masks"]
    eye, strict, incl = ctx["eye"], ctx["strict"], ctx["incl"]
    mb_a, mb_b = m_a.astype(BF16), m_b.astype(BF16)
    kk, qk, rhs, gcs_all, per_sc = [], [], [], [], []
    for sc in scs:
        rs = slice(sc * SUPER, (sc + 1) * SUPER)
        q_p, k_p, v_p = ctx["qkv"][sc]
        gcs = jnp.stack([ctx["gc_h"][rs, h * LANES:(h + 1) * LANES] for h in range(GDN_HEADS)])
        gcs_all.append(gcs)
        gc_pair, q_dec = [], []
        for p in range(n_pairs):
            kp, qp, vp = k_p[p], q_p[p], v_p[p]
            bp = ctx["beta"][rs, p * LANES:(p + 1) * LANES]
            gcp = jnp.where(ctx["first_half"], gcs[2 * p], gcs[2 * p + 1])
            eg = jnp.exp(gcp)
            kbb = kp * bp
            kbb_b, qp_b = kbb.astype(BF16), qp.astype(BF16)
            y = _mm_nt(jnp.concatenate([kbb_b * mb_a, kbb_b * mb_b, qp_b * mb_a, qp_b * mb_b], axis=0), kp)
            kk += [y[0:SUPER], y[SUPER:2 * SUPER]]
            qk += [y[2 * SUPER:3 * SUPER], y[3 * SUPER:4 * SUPER]]
            r_p = jnp.concatenate([vp * bp, kbb * eg], axis=1).astype(BF16)
            rhs += [r_p * jnp.concatenate([mb_a, mb_a], axis=1), r_p * jnp.concatenate([mb_b, mb_b], axis=1)]
            gc_pair.append(gcp)
            q_dec.append(qp * eg)
        per_sc.append(dict(gc_pair=gc_pair, q_dec=q_dec, k=k_p))
    gcs = jnp.concatenate(gcs_all, axis=0)
    rowterm = jnp.sum(jnp.where(eye[None], gcs, 0.0), axis=1, keepdims=True)
    dec = jnp.exp(jnp.minimum(gcs - rowterm, 0.0))
    yield
    n_mat = jnp.where(strict[None], jnp.stack(kk) * dec, 0.0)
    qkd = jnp.where(incl[None], jnp.stack(qk) * dec, 0.0)
    n8 = n_mat * ctx["diag8"]
    x_inv = jnp.where(eye, 1.0, 0.0)[None] - n8
    pw = _bmm(n8, n8)
    yield
    pw = pw.astype(BF16)
    y = _bmm(jnp.concatenate([pw, x_inv.astype(BF16)], axis=1), pw)
    x_inv = x_inv + y[:, SUPER:]
    yield
    x_inv = x_inv + _bmm(x_inv, y[:, :SUPER])
    yield
    for off in ctx["merge_masks"]:
        t_a = _bmm(x_inv, n_mat * off)
        yield
        x_inv = x_inv - _bmm(t_a, x_inv)
        yield
    uw = _bmm(x_inv, jnp.stack(rhs))
    for i, sc in enumerate(scs):
        o = GDN_HEADS * i
        out[sc] = dict(
            u=[uw[o + 2 * p, :, :LANES] + uw[o + 2 * p + 1, :, :LANES] for p in range(n_pairs)],
            w=[uw[o + 2 * p, :, LANES:] + uw[o + 2 * p + 1, :, LANES:] for p in range(n_pairs)],
            qkd=qkd[o:o + GDN_HEADS], **per_sc[i])
    yield


def _gdn_recurrence_steps(ctx, par, state_ref, o_rows):
    n_pairs = GDN_W // LANES
    mb_a, mb_b = (m.astype(BF16) for m in ctx["masks"])
    bd2 = ctx["bd2"]
    zeros = jnp.zeros((CHUNK, LANES), BF16)
    for c in range(SUPER // CHUNK):
        cs = slice(c * CHUNK, (c + 1) * CHUNK)
        last = c * CHUNK + CHUNK - 1
        ys = [_mm(jnp.concatenate([par["w"][p][cs], par["q_dec"][p][cs]], axis=0), state_ref[p])
              for p in range(n_pairs)]
        yield
        for p in range(n_pairs):
            gcp = par["gc_pair"][p]
            gl_row = gcp[last:last + 1, :]
            v_new = par["u"][p][cs] - ys[p][:CHUNK]
            k_tail = par["k"][p][cs] * jnp.exp(gl_row - gcp[cs])
            v_new_b = v_new.astype(BF16)
            vm_a, vm_b = v_new_b * mb_a, v_new_b * mb_b
            v_cat = jnp.concatenate([vm_a, zeros, vm_b, zeros] if c == 0 else [zeros, vm_a, zeros, vm_b], axis=0)
            q_cat = jnp.concatenate([par["qkd"][2 * p][cs], par["qkd"][2 * p + 1][cs]], axis=1)
            o_rows[p].append(ys[p][CHUNK:] + _mm(q_cat, v_cat))
            state_ref[p] = state_ref[p] * jnp.exp(gl_row) + _mm_tn(k_tail, v_new_b) * bd2
        yield


def _mixer_body(L, x_ref, cos_ref, sin_ref, win_ref, wout_ref,
                mk_ref, mv_ref, decay_ref, cross_ref, tail_ref, sdec_ref, bdr_ref, avg_ref, rng_ref,
                lcw_ref, lcb_ref, wax_ref, lba_ref, lbx_ref, lam_ref,
                gcw_ref, galog_ref, gdt_ref, gng_ref, tri_ref, egc_ref, ebeta_ref, bd2_ref, avg2_ref,
                lng_ref, lnb_ref,
                o_ref, ret_state, gdn_state, lru_h, lru_buf, gdn_buf):
    @pl.when(pl.program_id(1) == 0)
    def _reset():
        ret_state[...] = jnp.zeros_like(ret_state)
        gdn_state[...] = jnp.zeros_like(gdn_state)
        lru_h[...] = jnp.zeros_like(lru_h)
        lru_buf[...] = jnp.zeros_like(lru_buf)
        gdn_buf[...] = jnp.zeros_like(gdn_buf)

    x = x_ref[...]
    xb = x.astype(BF16)

    def proj(lo, hi):
        return jnp.dot(xb, win_ref[:, lo:hi], preferred_element_type=F32)

    n_pairs = GDN_W // LANES
    n_lru = LRU_W // LANES
    n_super = L // SUPER
    gdn_buf[SUBLANES:SUBLANES + L, :] = proj(C_GQ, C_GZ)
    ab = proj(C_AB, D_IN_PAD)
    lru_buf[SUBLANES:SUBLANES + L, :] = proj(C_LX, C_LG)
    lru_gate = proj(C_LG, C_GQ)
    gc_h, beta_pair = _gdn_gates(ab, galog_ref[...], gdt_ref[...], tri_ref, egc_ref, ebeta_ref, L)
    ctx = dict(gc_h=gc_h, beta=beta_pair, bd2=bd2_ref[...], qkv={}, **_gdn_masks())
    _gdn_prep(ctx, 0, gdn_buf, gcw_ref)
    lru_pre = _rg_lru_conv(lru_buf, lcw_ref, lcb_ref[...], wax_ref, L)
    heads = {}
    ret_gen = _retention_steps(proj, cos_ref[...], sin_ref[...],
                               (mk_ref, mv_ref, decay_ref, cross_ref, tail_ref, sdec_ref, bdr_ref, avg_ref),
                               rng_ref[...], ret_state, heads)
    next(ret_gen)
    lru_ab = _rg_lru_gates(lru_pre, lba_ref[...], lbx_ref[...], lam_ref[...])
    next(ret_gen)
    if n_super > 1:
        _gdn_prep(ctx, 1, gdn_buf, gcw_ref)
    par = {}
    o_rows = [[] for _ in range(n_pairs)]
    lru_cols = []
    mix_l = []

    def lru_out_steps():
        o_l = (jnp.concatenate(lru_cols, axis=1) * _gelu_tanh(lru_gate)).astype(BF16)
        for c in range(0, D_MODEL, RET_W):
            mix_l.append(jnp.dot(o_l, wout_ref[RET_W:RET_W + LRU_W, c:c + RET_W], preferred_element_type=F32))
            yield

    pairs = [tuple(range(s, min(s + 2, n_super))) for s in range(0, n_super, 2)]
    _drain(_gdn_parallel_steps(ctx, pairs[0], par))
    for b, scs in enumerate(pairs):
        last = b + 1 == len(pairs)
        if not last:
            for sc in pairs[b + 1]:
                _gdn_prep(ctx, sc, gdn_buf, gcw_ref)
        while len(lru_cols) < (n_lru if last else min(n_lru, 2 * (b + 1))):
            t = len(lru_cols)
            lru_cols.append(_rg_lru_scan(lru_ab[t][0], lru_ab[t][1], lru_h, t * LANES, (t + 1) * LANES, L))
        nxt = iter(()) if last else _gdn_parallel_steps(ctx, pairs[b + 1], par)
        lru_out = lru_out_steps() if last else iter(())
        step = 0
        for sc in scs:
            for _ in _gdn_recurrence_steps(ctx, par[sc], gdn_state, o_rows):
                next(nxt, None)
                next(nxt, None)
                if last:
                    next(ret_gen, None)
                    if step % 2 == 0:
                        next(ret_gen, None)
                        next(lru_out, None)
                step += 1
        _drain(nxt)
    _drain(ret_gen)
    _drain(lru_out)
    gdn_buf[0:SUBLANES, :] = gdn_buf[L:L + SUBLANES, :]
    lru_buf[0:SUBLANES, :] = lru_buf[L:L + SUBLANES, :]

    mix = _mm(heads["ret"], wout_ref[0:RET_W, :]) + jnp.concatenate(mix_l, axis=1)
    avg2 = avg2_ref[...]
    z = proj(C_GZ, C_AB)
    outs = []
    for p in range(n_pairs):
        op = jnp.concatenate(o_rows[p], axis=0)
        outs.append(op * lax.rsqrt(_mm(op * op, avg2) + 1e-6))
    o_g = jnp.concatenate(outs, axis=1) * gng_ref[...] * _silu(z)
    w_g = wout_ref[RET_W + LRU_W:, :]
    slabs = [slice(r, r + RET_BLOCK) for r in range(0, L, RET_BLOCK)]
    mix_g = [_mm(o_g[rows], w_g) for rows in slabs]
    for rows, m_g in zip(slabs, mix_g):
        o_ref[rows, :] = _layer_norm(ALPHA * x[rows] + mix[rows] + m_g, lng_ref[...], lnb_ref[...])


def _mixer_constants():
    L = RET_BLOCK
    lg = jnp.log1p(-jnp.exp2(-5.0 - jnp.arange(RET_HEADS, dtype=F32)))
    lane = np.arange(RET_W)
    khead = (lane % LANES) // (RET_DK // 2)
    vhead = lane // RET_DK
    mk = jnp.asarray(np.pad((khead[None, :] == np.arange(RET_HEADS)[:, None]).astype(np.float32), ((0, 4), (0, 0))))
    mv = jnp.asarray(np.pad((vhead[None, :] == np.arange(RET_HEADS)[:, None]).astype(np.float32), ((0, 4), (0, 0))))
    idx = np.arange(L)
    diff = jnp.asarray((idx[:, None] - idx[None, :]).astype(np.float32))
    same = jnp.asarray(idx[:, None] // CHUNK == idx[None, :] // CHUNK)
    earlier = jnp.asarray(idx[None, :] // CHUNK < idx[:, None] // CHUNK)
    decay = jnp.where(same[None], jnp.exp(jnp.abs(diff)[None] * lg[:, None, None]),
                      jnp.where(earlier[None], jnp.exp(diff[None] * lg[:, None, None]), 0.0))
    lg_lane = lg[jnp.asarray(khead)]
    fidx = jnp.asarray(idx.astype(np.float32))
    cross = jnp.exp((fidx + 1.0)[:, None] * lg_lane[None, :])
    tail = jnp.exp((L - 1.0 - fidx)[:, None] * lg_lane[None, :])
    sdec = jnp.broadcast_to(jnp.exp(float(L) * lg_lane)[:, None], (RET_W, RET_W))
    bdr = jnp.asarray((khead[:, None] == vhead[None, :]).astype(np.float32))
    avg = jnp.asarray((vhead[:, None] == vhead[None, :]).astype(np.float32) / RET_DK).astype(BF16)
    sidx = np.arange(SUPER)
    tri = jnp.asarray(np.logical_and(sidx[:, None] // CHUNK == sidx[None, :] // CHUNK,
                                     sidx[:, None] >= sidx[None, :]).astype(np.float32)).astype(BF16)
    e_gc = np.zeros((LANES, GDN_HEADS * LANES), np.float32)
    e_beta = np.zeros((LANES, GDN_W), np.float32)
    for h in range(GDN_HEADS):
        e_gc[h, h * LANES:(h + 1) * LANES] = 1.0
        e_beta[GDN_HEADS + h, h * GDN_DK:(h + 1) * GDN_DK] = 1.0
    l2 = np.arange(LANES) // GDN_DK
    bd2 = (l2[:, None] == l2[None, :]).astype(np.float32)
    return dict(mk=mk, mv=mv, decay=decay, cross=cross, tail=tail, sdec=sdec, bdr=bdr, avg=avg, tri=tri,
                e_gc=jnp.asarray(e_gc).astype(BF16), e_beta=jnp.asarray(e_beta).astype(BF16),
                bd2=jnp.asarray(bd2), avg2=jnp.asarray(bd2 / GDN_DK).astype(BF16))


def _mixer(x, cos, sin, w, consts):
    B, S, _ = x.shape
    L = min(MIX_BLOCK, S)
    tok = lambda n: pl.BlockSpec((None, L, n), lambda b, j: (b, j, 0))
    c = consts
    operands = [
        (x, tok(D_MODEL)), (cos, tok(LANES)), (sin, tok(LANES)),
        (w["w_in"], _layer_spec(w["w_in"].shape, w["layer"])), (w["w_out"], _layer_spec(w["w_out"].shape, w["layer"])),
        (c["mk"], None), (c["mv"], None), (c["decay"], None), (c["cross"], None), (c["tail"], None),
        (c["sdec"], None), (c["bdr"], None), (c["avg"], None), (w["ret_norm_g"], None),
        (w["lru_conv_w"], None), (w["lru_conv_b"], None), (w["lru_wax"], None), (w["lru_b_a"], None),
        (w["lru_b_x"], None), (w["lru_lambda"], None),
        (w["gdn_conv_w"], None), (w["gdn_a_log"], None), (w["gdn_dt_bias"], None), (w["gdn_norm_g"], None),
        (c["tri"], None), (c["e_gc"], None), (c["e_beta"], None), (c["bd2"], None), (c["avg2"], None),
        (w["ln_g"], None), (w["ln_b"], None),
    ]
    args = [a for a, _ in operands]
    specs = [s if s is not None else _const_spec(a.shape) for a, s in operands]
    return pl.pallas_call(
        functools.partial(_mixer_body, L),
        grid=(B, S // L),
        in_specs=specs,
        out_specs=tok(D_MODEL),
        out_shape=jax.ShapeDtypeStruct((B, S, D_MODEL), F32),
        scratch_shapes=[
            pltpu.VMEM((RET_W, RET_W), F32),
            pltpu.VMEM((GDN_W // LANES, LANES, LANES), F32),
            pltpu.VMEM((SUBLANES, LRU_W), F32),
            pltpu.VMEM((L + SUBLANES, LRU_W), F32),
            pltpu.VMEM((L + SUBLANES, 3 * GDN_W), F32),
        ],
        compiler_params=pltpu.CompilerParams(dimension_semantics=("arbitrary", "arbitrary"),
                                             vmem_limit_bytes=VMEM_LIMIT_BYTES),
        name="mixer",
    )(*args)


def _relayout_w_in(w):
    n, d = w.shape[0], w.shape[1]
    half = RET_DK // 2

    def halves_apart(cols):
        return cols.reshape(n, d, RET_HEADS, 2, half).transpose(0, 1, 3, 2, 4).reshape(n, d, RET_W)

    parts = [halves_apart(w[..., 0:RET_W]), halves_apart(w[..., RET_W:2 * RET_W]), w[..., 2 * RET_W:],
             jnp.zeros((n, d, D_IN_PAD - D_IN), w.dtype)]
    return jnp.concatenate(parts, axis=2).astype(BF16)


def _pair_block_diag(w):
    z = jnp.zeros((LRU_BLOCK, LRU_BLOCK), w.dtype)
    slabs = []
    for p in range(LRU_W // LANES):
        top = jnp.concatenate([w[2 * p], z], axis=1)
        bot = jnp.concatenate([z, w[2 * p + 1]], axis=1)
        slabs.append(jnp.concatenate([top, bot], axis=0))
    return jnp.stack(slabs)


def _pad_rows(a, rows):
    return jnp.pad(a, ((0, rows - a.shape[0]), (0, 0)))


def _head_row(v):
    return jnp.pad(v, (0, LANES - v.shape[0])).reshape(1, LANES)


def kernel(x, p, positions, ln_ffn1_g, ln_ffn1_b, ffn1_w_gate, ffn1_w_up, ffn1_w_down, w_in, ret_norm_g,
           lru_conv_w, lru_conv_b, lru_w_a, lru_b_a, lru_w_x, lru_b_x, lru_lambda, gdn_conv_w, gdn_a_log,
           gdn_dt_bias, gdn_norm_g, w_out, ln_mix_g, ln_mix_b, ffn2_w_gate, ffn2_w_up, ffn2_w_down,
           ple_w_gate, ple_w_proj, ln_ffn2_g, ln_ffn2_b):
    B, S, _ = x.shape
    T = B * S
    row = lambda v: v.reshape(1, -1)
    cos, sin = _rope_tables(positions)
    consts = _mixer_constants()
    bf = lambda w: w.astype(BF16)
    ffn1_w, ffn2_w = (bf(ffn1_w_gate), bf(ffn1_w_up), bf(ffn1_w_down)), (bf(ffn2_w_gate), bf(ffn2_w_up), bf(ffn2_w_down))
    ple_w = (bf(ple_w_gate), bf(ple_w_proj))
    w_in_all, w_out_all = _relayout_w_in(w_in), bf(w_out)
    for i in range(DEPTH):
        x2 = _ffn(x.reshape(T, D_MODEL), i, *ffn1_w, row(ln_ffn1_g[i]), row(ln_ffn1_b[i]))
        mix_w = dict(
            layer=i, w_in=w_in_all, w_out=w_out_all,
            ret_norm_g=row(ret_norm_g[i]),
            lru_conv_w=_pad_rows(lru_conv_w[i], SUBLANES), lru_conv_b=row(lru_conv_b[i]),
            lru_wax=jnp.concatenate([_pair_block_diag(lru_w_a[i]), _pair_block_diag(lru_w_x[i])], axis=2).astype(BF16),
            lru_b_a=row(lru_b_a[i]), lru_b_x=row(lru_b_x[i]), lru_lambda=row(lru_lambda[i]),
            gdn_conv_w=_pad_rows(gdn_conv_w[i], SUBLANES),
            gdn_a_log=_head_row(gdn_a_log[i]), gdn_dt_bias=_head_row(gdn_dt_bias[i]),
            gdn_norm_g=row(jnp.tile(gdn_norm_g[i], GDN_HEADS)),
            ln_g=row(ln_mix_g[i]), ln_b=row(ln_mix_b[i]),
        )
        x3 = _mixer(x2.reshape(B, S, D_MODEL), cos, sin, mix_w, consts)
        x = _ffn(x3.reshape(T, D_MODEL), i, *ffn2_w, row(ln_ffn2_g[i]), row(ln_ffn2_b[i]),
                 ple=(p.reshape(DEPTH, T, PLE_DIM),) + ple_w)
        x = x.reshape(B, S, D_MODEL)
    return x
```

```python
import functools

import numpy as np
import jax
import jax.numpy as jnp
from jax import lax
from jax.experimental import pallas as pl
from jax.experimental.pallas import tpu as pltpu

F32 = jnp.float32
BF16 = jnp.bfloat16

D_MODEL = 1024
DEPTH = 2
CHUNK = 64
PLE_DIM = 256
D_FF = 2816
CONV_W = 4
RET_HEADS = 4
RET_DK = 64
RET_W = 256
LRU_W = 384
LRU_BLOCK = 64
LRU_C = 8.0
GDN_HEADS = 6
GDN_DK = 64
GDN_W = 384
IN_WIDTHS = (RET_W, RET_W, RET_W, RET_W, LRU_W, LRU_W, GDN_W, GDN_W, GDN_W, GDN_W, GDN_HEADS, GDN_HEADS)
D_IN = sum(IN_WIDTHS)
ROPE_THETA = 10000.0
ALPHA = (2 * DEPTH) ** 0.25
LN_EPS = 1e-5
HEAD_NORM_EPS = 1e-5
L2_EPS = 1e-6
RMS_EPS = 1e-6

LANES = 128
SUBLANES = 8
VMEM_LIMIT_BYTES = 56 * 1024 * 1024

MIX_BLOCK = 512
RET_BLOCK = 256
SUPER = 128
FFN_ROWS = 1024
FFN_COLS = 256
FFN_SLABS = 4
ROPE_ROWS = 1024

C_RQ, C_RK, C_RV, C_RG = 0, 256, 512, 768
C_LX, C_LG = 1024, 1408
C_GQ, C_GK, C_GV, C_GZ = 1792, 2176, 2560, 2944
C_AB = 3328
D_IN_PAD = 3456


def _mm(a, b):
    return jnp.dot(a.astype(BF16), b.astype(BF16), preferred_element_type=F32)


def _mm_nt(a, b):
    return lax.dot_general(a.astype(BF16), b.astype(BF16), (((1,), (1,)), ((), ())),
                           preferred_element_type=F32)


def _mm_tn(a, b):
    return lax.dot_general(a.astype(BF16), b.astype(BF16), (((0,), (0,)), ((), ())),
                           preferred_element_type=F32)


def _bmm(a, b):
    return lax.dot_general(a.astype(BF16), b.astype(BF16), (((2,), (1,)), ((0,), (0,))),
                           preferred_element_type=F32)


def _split(x, parts):
    out = []
    r = x
    for _ in range(parts):
        t = r.astype(BF16)
        out.append(t)
        r = r - t.astype(F32)
    return out


def _mm_sel(sel, x, parts):
    acc = None
    for t in _split(x, parts):
        y = jnp.dot(sel, t, preferred_element_type=F32)
        acc = y if acc is None else acc + y
    return acc


def _mm_xsel(x, sel, parts):
    acc = None
    for t in _split(x, parts):
        y = jnp.dot(t, sel, preferred_element_type=F32)
        acc = y if acc is None else acc + y
    return acc


LOG2_E = 1.4426950408889634


def _sigmoid(x):
    return 1.0 / (1.0 + jnp.exp2(x * -LOG2_E))


def _silu(x):
    return x * _sigmoid(x)


def _softplus(x):
    return jnp.maximum(x, 0.0) + jnp.log(1.0 + jnp.exp(-jnp.abs(x)))


def _sqrt_nonneg(y):
    return jnp.where(y > 0.0, y * lax.rsqrt(y), 0.0)


def _gelu_tanh(x):
    c = np.float32(np.sqrt(2.0 / np.pi))
    return 0.5 * x * (1.0 + jnp.tanh(c * (x + 0.044715 * (x * x * x))))


def _layer_norm(r, g, b):
    mu = jnp.mean(r, axis=-1, keepdims=True)
    d = r - mu
    var = jnp.mean(d * d, axis=-1, keepdims=True)
    return d * lax.rsqrt(var + LN_EPS) * g + b


def _rope_body(pos_ref, invf_ref, cos_ref, sin_ref):
    half = RET_DK // 2
    per_row = LANES // half
    ang = pos_ref[...] * invf_ref[...]
    rows = ang.shape[0]
    group = jnp.right_shift(lax.broadcasted_iota(jnp.int32, (rows, LANES), 1), half.bit_length() - 1)
    for table, out_ref in ((jnp.cos(ang), cos_ref), (jnp.sin(ang), sin_ref)):
        for j in range(per_row):
            own = jnp.where(group == j, table, 0.0)
            spread = own
            for k in range(1, per_row):
                spread = spread + pltpu.roll(own, k * half, 1)
            out_ref[pl.ds(j, rows, stride=per_row), :] = spread


def _rope_tables(positions):
    B, S = positions.shape
    half = RET_DK // 2
    per_row = LANES // half
    T = B * S
    rows = T // per_row
    inv_freq = ROPE_THETA ** (-jnp.arange(half, dtype=F32) / half)
    pos_rep = jnp.broadcast_to(positions.astype(F32).reshape(rows, per_row, 1), (rows, per_row, half))
    invf = jnp.tile(inv_freq, per_row).reshape(1, LANES)
    blk = min(rows, ROPE_ROWS)
    cos, sin = pl.pallas_call(
        _rope_body,
        grid=(rows // blk,),
        in_specs=[pl.BlockSpec((blk, LANES), lambda i: (i, 0)),
                  pl.BlockSpec((1, LANES), lambda i: (0, 0))],
        out_specs=[pl.BlockSpec((per_row * blk, LANES), lambda i: (i, 0)),
                   pl.BlockSpec((per_row * blk, LANES), lambda i: (i, 0))],
        out_shape=[jax.ShapeDtypeStruct((T, LANES), F32)] * 2,
        name="rope_tables",
    )(pos_rep.reshape(rows, LANES), invf)
    return cos.reshape(B, S, LANES), sin.reshape(B, S, LANES)


def _ffn_body(has_ple, *refs):
    if has_ple:
        (x_ref, wg_ref, wu_ref, wd_ref, g_ref, b_ref, p_ref, pwg_ref, pwp_ref, o_ref, h_scr) = refs
    else:
        (x_ref, wg_ref, wu_ref, wd_ref, g_ref, b_ref, o_ref, h_scr) = refs
    x = x_ref[...]
    xb = x.astype(BF16)
    for c in range(0, D_FF, FFN_COLS):
        gate = jnp.dot(xb, wg_ref[:, c:c + FFN_COLS], preferred_element_type=F32)
        up = jnp.dot(xb, wu_ref[:, c:c + FFN_COLS], preferred_element_type=F32)
        h_scr[:, c:c + FFN_COLS] = (_silu(gate) * up).astype(BF16)
    r = ALPHA * x
    if has_ple:
        gate = _sigmoid(jnp.dot(xb, pwg_ref[...], preferred_element_type=F32))
        proj = jnp.dot(p_ref[...].astype(BF16), pwp_ref[...], preferred_element_type=F32)
        r = r + gate * proj
    n = x.shape[0]
    slabs = [slice(s, s + n // FFN_SLABS) for s in range(0, n, n // FFN_SLABS)]
    ys = [jnp.dot(h_scr[rows, :], wd_ref[...], preferred_element_type=F32) for rows in slabs]
    for rows, y in zip(slabs, ys):
        o_ref[rows, :] = _layer_norm(r[rows] + 0.5 * y, g_ref[...], b_ref[...])


def _const_spec(shape):
    nd = len(shape)
    return pl.BlockSpec(shape, lambda *_: (0,) * nd, pipeline_mode=pl.Buffered(1))


def _layer_spec(shape, layer):
    nd = len(shape)
    return pl.BlockSpec((None,) + tuple(shape[1:]), lambda *_: (layer,) + (0,) * (nd - 1),
                        pipeline_mode=pl.Buffered(1))


def _ffn(x2, layer, wg, wu, wd, ln_g, ln_b, ple=None):
    T = x2.shape[0]
    rows = min(FFN_ROWS, T)
    tok = lambda n: pl.BlockSpec((rows, n), lambda i: (i, 0))
    args = [x2, wg, wu, wd, ln_g, ln_b]
    specs = [tok(D_MODEL), _layer_spec(wg.shape, layer), _layer_spec(wu.shape, layer), _layer_spec(wd.shape, layer),
             _const_spec(ln_g.shape), _const_spec(ln_b.shape)]
    if ple is not None:
        p_all, pwg, pwp = ple
        args += [p_all, pwg, pwp]
        specs += [pl.BlockSpec((None, rows, PLE_DIM), lambda i: (layer, i, 0)),
                  _layer_spec(pwg.shape, layer), _layer_spec(pwp.shape, layer)]
    return pl.pallas_call(
        functools.partial(_ffn_body, ple is not None),
        grid=(T // rows,),
        in_specs=specs,
        out_specs=tok(D_MODEL),
        out_shape=jax.ShapeDtypeStruct((T, D_MODEL), F32),
        scratch_shapes=[pltpu.VMEM((rows, D_FF), BF16)],
        compiler_params=pltpu.CompilerParams(dimension_semantics=("arbitrary",),
                                             vmem_limit_bytes=VMEM_LIMIT_BYTES),
        name="ffn_ple" if ple is not None else "ffn",
    )(*args)


def _iota2(shape, dim):
    return lax.broadcasted_iota(jnp.int32, shape, dim)


def _drain(gen):
    for _ in gen:
        pass


def _retention_steps(proj, cos, sin, consts, norm_g, state_ref, out):
    mk_ref, mv_ref, decay_ref, cross_ref, tail_ref, sdec_ref, bd_ref, avg_ref = consts

    def rot(t):
        t1, t2 = t[:, :LANES], t[:, LANES:]
        return jnp.concatenate([t1 * cos - t2 * sin, t2 * cos + t1 * sin], axis=1)

    q_raw = proj(C_RQ, C_RK)
    k_raw = proj(C_RK, C_RV)
    yield
    v_all = proj(C_RV, C_RG)
    gate = proj(C_RG, C_LX)
    yield
    q_all = rot(q_raw) * (RET_DK ** -0.5)
    k_all = rot(k_raw)
    os = []
    for sb in range(q_all.shape[0] // RET_BLOCK):
        rows = slice(sb * RET_BLOCK, (sb + 1) * RET_BLOCK)
        q, k, v = q_all[rows], k_all[rows], v_all[rows]
        state = state_ref[...]
        o = _mm(q * cross_ref[...], state)
        state_ref[...] = state * sdec_ref[...] + _mm_tn(k * tail_ref[...], v) * bd_ref[...]
        yield
        qb, kb, vb = q.astype(BF16), k.astype(BF16), v.astype(BF16)
        for h in range(RET_HEADS):
            s = _mm_nt(qb * mk_ref[h:h + 1, :].astype(BF16), kb) * decay_ref[h]
            o = o + _mm(s, vb * mv_ref[h:h + 1, :].astype(BF16))
            yield
        os.append(o)
    o = jnp.concatenate(os, axis=0)
    avg = avg_ref[...]
    d = o - _mm(o, avg)
    var = _mm(d * d, avg)
    out["ret"] = d * lax.rsqrt(var + HEAD_NORM_EPS) * norm_g * _silu(gate)
    yield


def _conv_taps(buf_ref, w_ref, r0, n, c0, c1):
    base = SUBLANES + r0
    y = buf_ref[base:base + n, c0:c1] * w_ref[CONV_W - 1:CONV_W, c0:c1]
    for k in range(CONV_W - 1):
        off = base - (CONV_W - 1) + k
        y = y + buf_ref[off:off + n, c0:c1] * w_ref[k:k + 1, c0:c1]
    return y


def _rg_lru_conv(buf_ref, conv_w_ref, conv_b, wax_ref, L):
    coeffs = []
    for p in range(LRU_W // LANES):
        c0, c1 = p * LANES, (p + 1) * LANES
        xc = _conv_taps(buf_ref, conv_w_ref, 0, L, c0, c1) + conv_b[:, c0:c1]
        y = _mm(xc, wax_ref[p])
        coeffs.append((xc, y, c0, c1))
    return coeffs


def _rg_lru_gates(coeffs, b_a, b_x, lam):
    out = []
    for xc, y, c0, c1 in coeffs:
        r = _sigmoid(y[:, :LANES] + b_a[:, c0:c1])
        i = _sigmoid(y[:, LANES:] + b_x[:, c0:c1])
        log_a = (LRU_C * (-_softplus(-lam[:, c0:c1]))) * r
        t = jnp.tanh(log_a)
        out.append((jnp.exp(log_a), _sqrt_nonneg(-2.0 * t / (1.0 - t)) * (i * xc)))
    return out


def _rg_lru_scan(a, b, h_ref, c0, c1, L):
    G = L // SUBLANES
    a3 = a.reshape(G, SUBLANES, LANES)
    b3 = b.reshape(G, SUBLANES, LANES)
    sub = _iota2((G, SUBLANES, LANES), 1)
    for d in (1, 2, 4):
        ok = sub >= d
        a_s = pltpu.roll(a3, d, 1)
        b_s = pltpu.roll(b3, d, 1)
        b3 = b3 + jnp.where(ok, a3 * b_s, 0.0)
        a3 = jnp.where(ok, a3 * a_s, a3)
    a_last = jnp.broadcast_to(a3[:, SUBLANES - 1:SUBLANES, :], a3.shape)
    b_last = jnp.broadcast_to(b3[:, SUBLANES - 1:SUBLANES, :], b3.shape)
    carry = h_ref[:, c0:c1]
    hs = []
    for g in range(G):
        hs.append(b3[g] + a3[g] * carry)
        carry = b_last[g] + a_last[g] * carry
    h_ref[:, c0:c1] = carry
    return jnp.concatenate(hs, axis=0)


def _gdn_gates(ab, a_log_row, dt_row, tri_ref, e_gc_ref, e_beta_ref, L):
    g_n = -jnp.exp(a_log_row) * _softplus(ab + dt_row)
    tri = tri_ref[...]
    gc_n = jnp.concatenate([_mm_sel(tri, g_n[s * SUPER:(s + 1) * SUPER], 2) for s in range(L // SUPER)], axis=0)
    gc_h = _mm_xsel(gc_n, e_gc_ref[...], 2)
    beta_pair = _mm_xsel(_sigmoid(ab), e_beta_ref[...], 2)
    return gc_h, beta_pair


def _gdn_masks():
    lane = _iota2((SUPER, LANES), 1)
    row = _iota2((SUPER, LANES), 0)
    chunk_bits = CHUNK.bit_length() - 1
    same_chunk = jnp.right_shift(row, chunk_bits) == jnp.right_shift(lane, chunk_bits)
    first_half = lane < CHUNK

    def block(t, size):
        return jnp.right_shift(t, size.bit_length() - 1)

    def merge_mask(m):
        lower_left = jnp.logical_and(jnp.bitwise_and(block(row, m), 1) == 1, jnp.bitwise_and(block(lane, m), 1) == 0)
        return jnp.where(jnp.logical_and(block(row, 2 * m) == block(lane, 2 * m), lower_left), 1.0, 0.0).astype(F32)

    return dict(
        incl=jnp.logical_and(same_chunk, row >= lane), strict=jnp.logical_and(same_chunk, row > lane),
        eye=row == lane, first_half=first_half,
        diag8=jnp.where(block(row, SUBLANES) == block(lane, SUBLANES), 1.0, 0.0).astype(F32),
        merge_masks=tuple(merge_mask(SUBLANES << i) for i in range((CHUNK // SUBLANES).bit_length() - 1)),
        masks=(jnp.where(first_half, 1.0, 0.0).astype(F32)[0:1, :], jnp.where(first_half, 0.0, 1.0).astype(F32)[0:1, :]))


def _gdn_prep(ctx, sc, buf_ref, conv_w_ref):
    W = GDN_W
    qkv = _silu(_conv_taps(buf_ref, conv_w_ref, sc * SUPER, SUPER, 0, 3 * W))
    bd2b = ctx["bd2"].astype(BF16)
    q, k, v = [], [], []
    for p in range(W // LANES):
        qk = jnp.concatenate([qkv[:, p * LANES:(p + 1) * LANES], qkv[:, W + p * LANES:W + (p + 1) * LANES]], axis=0)
        qk = qk * lax.rsqrt(_mm(qk * qk, bd2b) + L2_EPS)
        q.append(qk[:SUPER] * (GDN_DK ** -0.5))
        k.append(qk[SUPER:])
        v.append(qkv[:, 2 * W + p * LANES:2 * W + (p + 1) * LANES])
    ctx["qkv"][sc] = (q, k, v)


def _gdn_parallel_steps(ctx, scs, out):
    n_pairs = GDN_W // LANES
    m_a, m_b = ctx["masks"]
    eye, strict, incl = ctx["eye"], ctx["strict"], ctx["incl"]
    mb_a, mb_b = m_a.astype(BF16), m_b.astype(BF16)
    kk, qk, rhs, gcs_all, per_sc = [], [], [], [], []
    for sc in scs:
        rs = slice(sc * SUPER, (sc + 1) * SUPER)
        q_p, k_p, v_p = ctx["qkv"][sc]
        gcs = jnp.stack([ctx["gc_h"][rs, h * LANES:(h + 1) * LANES] for h in range(GDN_HEADS)])
        gcs_all.append(gcs)
        gc_pair, q_dec = [], []
        for p in range(n_pairs):
            kp, qp, vp = k_p[p], q_p[p], v_p[p]
            bp = ctx["beta"][rs, p * LANES:(p + 1) * LANES]
            gcp = jnp.where(ctx["first_half"], gcs[2 * p], gcs[2 * p + 1])
            eg = jnp.exp(gcp)
            kbb = kp * bp
            kbb_b, qp_b = kbb.astype(BF16), qp.astype(BF16)
            y = _mm_nt(jnp.concatenate([kbb_b * mb_a, kbb_b * mb_b, qp_b * mb_a, qp_b * mb_b], axis=0), kp)
            kk += [y[0:SUPER], y[SUPER:2 * SUPER]]
            qk += [y[2 * SUPER:3 * SUPER], y[3 * SUPER:4 * SUPER]]
            r_p = jnp.concatenate([vp * bp, kbb * eg], axis=1).astype(BF16)
            rhs += [r_p * jnp.concatenate([mb_a, mb_a], axis=1), r_p * jnp.concatenate([mb_b, mb_b], axis=1)]
            gc_pair.append(gcp)
            q_dec.append(qp * eg)
        per_sc.append(dict(gc_pair=gc_pair, q_dec=q_dec, k=k_p))
    gcs = jnp.concatenate(gcs_all, axis=0)
    rowterm = jnp.sum(jnp.where(eye[None], gcs, 0.0), axis=1, keepdims=True)
    dec = jnp.exp(jnp.minimum(gcs - rowterm, 0.0))
    yield
    n_mat = jnp.where(strict[None], jnp.stack(kk) * dec, 0.0)
    qkd = jnp.where(incl[None], jnp.stack(qk) * dec, 0.0)
    n8 = n_mat * ctx["diag8"]
    x_inv = jnp.where(eye, 1.0, 0.0)[None] - n8
    pw = _bmm(n8, n8)
    yield
    pw = pw.astype(BF16)
    y = _bmm(jnp.concatenate([pw, x_inv.astype(BF16)], axis=1), pw)
    x_inv = x_inv + y[:, SUPER:]
    yield
    x_inv = x_inv + _bmm(x_inv, y[:, :SUPER])
    yield
    for off in ctx["merge_masks"]:
        t_a = _bmm(x_inv, n_mat * off)
        yield
        x_inv = x_inv - _bmm(t_a, x_inv)
        yield
    uw = _bmm(x_inv, jnp.stack(rhs))
    for i, sc in enumerate(scs):
        o = GDN_HEADS * i
        out[sc] = dict(
            u=[uw[o + 2 * p, :, :LANES] + uw[o + 2 * p + 1, :, :LANES] for p in range(n_pairs)],
            w=[uw[o + 2 * p, :, LANES:] + uw[o + 2 * p + 1, :, LANES:] for p in range(n_pairs)],
            qkd=qkd[o:o + GDN_HEADS], **per_sc[i])
    yield


def _gdn_recurrence_steps(ctx, par, state_ref, o_rows):
    n_pairs = GDN_W // LANES
    mb_a, mb_b = (m.astype(BF16) for m in ctx["masks"])
    bd2 = ctx["bd2"]
    zeros = jnp.zeros((CHUNK, LANES), BF16)
    for c in range(SUPER // CHUNK):
        cs = slice(c * CHUNK, (c + 1) * CHUNK)
        last = c * CHUNK + CHUNK - 1
        ys = [_mm(jnp.concatenate([par["w"][p][cs], par["q_dec"][p][cs]], axis=0), state_ref[p])
              for p in range(n_pairs)]
        yield
        for p in range(n_pairs):
            gcp = par["gc_pair"][p]
            gl_row = gcp[last:last + 1, :]
            v_new = par["u"][p][cs] - ys[p][:CHUNK]
            k_tail = par["k"][p][cs] * jnp.exp(gl_row - gcp[cs])
            v_new_b = v_new.astype(BF16)
            vm_a, vm_b = v_new_b * mb_a, v_new_b * mb_b
            v_cat = jnp.concatenate([vm_a, zeros, vm_b, zeros] if c == 0 else [zeros, vm_a, zeros, vm_b], axis=0)
            q_cat = jnp.concatenate([par["qkd"][2 * p][cs], par["qkd"][2 * p + 1][cs]], axis=1)
            o_rows[p].append(ys[p][CHUNK:] + _mm(q_cat, v_cat))
            state_ref[p] = state_ref[p] * jnp.exp(gl_row) + _mm_tn(k_tail, v_new_b) * bd2
        yield


def _mixer_body(L, x_ref, cos_ref, sin_ref, win_ref, wout_ref,
                mk_ref, mv_ref, decay_ref, cross_ref, tail_ref, sdec_ref, bdr_ref, avg_ref, rng_ref,
                lcw_ref, lcb_ref, wax_ref, lba_ref, lbx_ref, lam_ref,
                gcw_ref, galog_ref, gdt_ref, gng_ref, tri_ref, egc_ref, ebeta_ref, bd2_ref, avg2_ref,
                lng_ref, lnb_ref,
                o_ref, ret_state, gdn_state, lru_h, lru_buf, gdn_buf):
    @pl.when(pl.program_id(1) == 0)
    def _reset():
        ret_state[...] = jnp.zeros_like(ret_state)
        gdn_state[...] = jnp.zeros_like(gdn_state)
        lru_h[...] = jnp.zeros_like(lru_h)
        lru_buf[...] = jnp.zeros_like(lru_buf)
        gdn_buf[...] = jnp.zeros_like(gdn_buf)

    x = x_ref[...]
    xb = x.astype(BF16)

    def proj(lo, hi):
        return jnp.dot(xb, win_ref[:, lo:hi], preferred_element_type=F32)

    n_pairs = GDN_W // LANES
    n_lru = LRU_W // LANES
    n_super = L // SUPER
    gdn_buf[SUBLANES:SUBLANES + L, :] = proj(C_GQ, C_GZ)
    ab = proj(C_AB, D_IN_PAD)
    lru_buf[SUBLANES:SUBLANES + L, :] = proj(C_LX, C_LG)
    lru_gate = proj(C_LG, C_GQ)
    gc_h, beta_pair = _gdn_gates(ab, galog_ref[...], gdt_ref[...], tri_ref, egc_ref, ebeta_ref, L)
    ctx = dict(gc_h=gc_h, beta=beta_pair, bd2=bd2_ref[...], qkv={}, **_gdn_masks())
    _gdn_prep(ctx, 0, gdn_buf, gcw_ref)
    lru_pre = _rg_lru_conv(lru_buf, lcw_ref, lcb_ref[...], wax_ref, L)
    heads = {}
    ret_gen = _retention_steps(proj, cos_ref[...], sin_ref[...],
                               (mk_ref, mv_ref, decay_ref, cross_ref, tail_ref, sdec_ref, bdr_ref, avg_ref),
                               rng_ref[...], ret_state, heads)
    next(ret_gen)
    lru_ab = _rg_lru_gates(lru_pre, lba_ref[...], lbx_ref[...], lam_ref[...])
    next(ret_gen)
    if n_super > 1:
        _gdn_prep(ctx, 1, gdn_buf, gcw_ref)
    par = {}
    o_rows = [[] for _ in range(n_pairs)]
    lru_cols = []
    mix_l = []

    def lru_out_steps():
        o_l = (jnp.concatenate(lru_cols, axis=1) * _gelu_tanh(lru_gate)).astype(BF16)
        for c in range(0, D_MODEL, RET_W):
            mix_l.append(jnp.dot(o_l, wout_ref[RET_W:RET_W + LRU_W, c:c + RET_W], preferred_element_type=F32))
            yield

    pairs = [tuple(range(s, min(s + 2, n_super))) for s in range(0, n_super, 2)]
    _drain(_gdn_parallel_steps(ctx, pairs[0], par))
    for b, scs in enumerate(pairs):
        last = b + 1 == len(pairs)
        if not last:
            for sc in pairs[b + 1]:
                _gdn_prep(ctx, sc, gdn_buf, gcw_ref)
        while len(lru_cols) < (n_lru if last else min(n_lru, 2 * (b + 1))):
            t = len(lru_cols)
            lru_cols.append(_rg_lru_scan(lru_ab[t][0], lru_ab[t][1], lru_h, t * LANES, (t + 1) * LANES, L))
        nxt = iter(()) if last else _gdn_parallel_steps(ctx, pairs[b + 1], par)
        lru_out = lru_out_steps() if last else iter(())
        step = 0
        for sc in scs:
            for _ in _gdn_recurrence_steps(ctx, par[sc], gdn_state, o_rows):
                next(nxt, None)
                next(nxt, None)
                if last:
                    next(ret_gen, None)
                    if step % 2 == 0:
                        next(ret_gen, None)
                        next(lru_out, None)
                step += 1
        _drain(nxt)
    _drain(ret_gen)
    _drain(lru_out)
    gdn_buf[0:SUBLANES, :] = gdn_buf[L:L + SUBLANES, :]
    lru_buf[0:SUBLANES, :] = lru_buf[L:L + SUBLANES, :]

    mix = _mm(heads["ret"], wout_ref[0:RET_W, :]) + jnp.concatenate(mix_l, axis=1)
    avg2 = avg2_ref[...]
    z = proj(C_GZ, C_AB)
    outs = []
    for p in range(n_pairs):
        op = jnp.concatenate(o_rows[p], axis=0)
        outs.append(op * lax.rsqrt(_mm(op * op, avg2) + RMS_EPS))
    o_g = jnp.concatenate(outs, axis=1) * gng_ref[...] * _silu(z)
    w_g = wout_ref[RET_W + LRU_W:, :]
    slabs = [slice(r, r + RET_BLOCK) for r in range(0, L, RET_BLOCK)]
    mix_g = [_mm(o_g[rows], w_g) for rows in slabs]
    for rows, m_g in zip(slabs, mix_g):
        o_ref[rows, :] = _layer_norm(ALPHA * x[rows] + mix[rows] + m_g, lng_ref[...], lnb_ref[...])


def _mixer_constants():
    L = RET_BLOCK
    lg = jnp.log1p(-jnp.exp2(-5.0 - jnp.arange(RET_HEADS, dtype=F32)))
    lane = np.arange(RET_W)
    khead = (lane % LANES) // (RET_DK // 2)
    vhead = lane // RET_DK
    mk = jnp.asarray(np.pad((khead[None, :] == np.arange(RET_HEADS)[:, None]).astype(np.float32), ((0, 4), (0, 0))))
    mv = jnp.asarray(np.pad((vhead[None, :] == np.arange(RET_HEADS)[:, None]).astype(np.float32), ((0, 4), (0, 0))))
    idx = np.arange(L)
    diff = jnp.asarray((idx[:, None] - idx[None, :]).astype(np.float32))
    same = jnp.asarray(idx[:, None] // CHUNK == idx[None, :] // CHUNK)
    earlier = jnp.asarray(idx[None, :] // CHUNK < idx[:, None] // CHUNK)
    decay = jnp.where(same[None], jnp.exp(jnp.abs(diff)[None] * lg[:, None, None]),
                      jnp.where(earlier[None], jnp.exp(diff[None] * lg[:, None, None]), 0.0))
    lg_lane = lg[jnp.asarray(khead)]
    fidx = jnp.asarray(idx.astype(np.float32))
    cross = jnp.exp((fidx + 1.0)[:, None] * lg_lane[None, :])
    tail = jnp.exp((L - 1.0 - fidx)[:, None] * lg_lane[None, :])
    sdec = jnp.broadcast_to(jnp.exp(float(L) * lg_lane)[:, None], (RET_W, RET_W))
    bdr = jnp.asarray((khead[:, None] == vhead[None, :]).astype(np.float32))
    avg = jnp.asarray((vhead[:, None] == vhead[None, :]).astype(np.float32) / RET_DK).astype(BF16)
    sidx = np.arange(SUPER)
    tri = jnp.asarray(np.logical_and(sidx[:, None] // CHUNK == sidx[None, :] // CHUNK,
                                     sidx[:, None] >= sidx[None, :]).astype(np.float32)).astype(BF16)
    e_gc = np.zeros((LANES, GDN_HEADS * LANES), np.float32)
    e_beta = np.zeros((LANES, GDN_W), np.float32)
    for h in range(GDN_HEADS):
        e_gc[h, h * LANES:(h + 1) * LANES] = 1.0
        e_beta[GDN_HEADS + h, h * GDN_DK:(h + 1) * GDN_DK] = 1.0
    l2 = np.arange(LANES) // GDN_DK
    bd2 = (l2[:, None] == l2[None, :]).astype(np.float32)
    return dict(mk=mk, mv=mv, decay=decay, cross=cross, tail=tail, sdec=sdec, bdr=bdr, avg=avg, tri=tri,
                e_gc=jnp.asarray(e_gc).astype(BF16), e_beta=jnp.asarray(e_beta).astype(BF16),
                bd2=jnp.asarray(bd2), avg2=jnp.asarray(bd2 / GDN_DK).astype(BF16))


def _mixer(x, cos, sin, w, consts):
    B, S, _ = x.shape
    L = min(MIX_BLOCK, S)
    tok = lambda n: pl.BlockSpec((None, L, n), lambda b, j: (b, j, 0))
    c = consts
    operands = [
        (x, tok(D_MODEL)), (cos, tok(LANES)), (sin, tok(LANES)),
        (w["w_in"], _layer_spec(w["w_in"].shape, w["layer"])), (w["w_out"], _layer_spec(w["w_out"].shape, w["layer"])),
        (c["mk"], None), (c["mv"], None), (c["decay"], None), (c["cross"], None), (c["tail"], None),
        (c["sdec"], None), (c["bdr"], None), (c["avg"], None), (w["ret_norm_g"], None),
        (w["lru_conv_w"], None), (w["lru_conv_b"], None), (w["lru_wax"], None), (w["lru_b_a"], None),
        (w["lru_b_x"], None), (w["lru_lambda"], None),
        (w["gdn_conv_w"], None), (w["gdn_a_log"], None), (w["gdn_dt_bias"], None), (w["gdn_norm_g"], None),
        (c["tri"], None), (c["e_gc"], None), (c["e_beta"], None), (c["bd2"], None), (c["avg2"], None),
        (w["ln_g"], None), (w["ln_b"], None),
    ]
    args = [a for a, _ in operands]
    specs = [s if s is not None else _const_spec(a.shape) for a, s in operands]
    return pl.pallas_call(
        functools.partial(_mixer_body, L),
        grid=(B, S // L),
        in_specs=specs,
        out_specs=tok(D_MODEL),
        out_shape=jax.ShapeDtypeStruct((B, S, D_MODEL), F32),
        scratch_shapes=[
            pltpu.VMEM((RET_W, RET_W), F32),
            pltpu.VMEM((GDN_W // LANES, LANES, LANES), F32),
            pltpu.VMEM((SUBLANES, LRU_W), F32),
            pltpu.VMEM((L + SUBLANES, LRU_W), F32),
            pltpu.VMEM((L + SUBLANES, 3 * GDN_W), F32),
        ],
        compiler_params=pltpu.CompilerParams(dimension_semantics=("arbitrary", "arbitrary"),
                                             vmem_limit_bytes=VMEM_LIMIT_BYTES),
        name="mixer",
    )(*args)


def _relayout_w_in(w):
    n, d = w.shape[0], w.shape[1]
    half = RET_DK // 2

    def halves_apart(cols):
        return cols.reshape(n, d, RET_HEADS, 2, half).transpose(0, 1, 3, 2, 4).reshape(n, d, RET_W)

    parts = [halves_apart(w[..., 0:RET_W]), halves_apart(w[..., RET_W:2 * RET_W]), w[..., 2 * RET_W:],
             jnp.zeros((n, d, D_IN_PAD - D_IN), w.dtype)]
    return jnp.concatenate(parts, axis=2).astype(BF16)


def _pair_block_diag(w):
    z = jnp.zeros((LRU_BLOCK, LRU_BLOCK), w.dtype)
    slabs = []
    for p in range(LRU_W // LANES):
        top = jnp.concatenate([w[2 * p], z], axis=1)
        bot = jnp.concatenate([z, w[2 * p + 1]], axis=1)
        slabs.append(jnp.concatenate([top, bot], axis=0))
    return jnp.stack(slabs)


def _pad_rows(a, rows):
    return jnp.pad(a, ((0, rows - a.shape[0]), (0, 0)))


def _head_row(v):
    return jnp.pad(v, (0, LANES - v.shape[0])).reshape(1, LANES)


def kernel(x, p, positions, ln_ffn1_g, ln_ffn1_b, ffn1_w_gate, ffn1_w_up, ffn1_w_down, w_in, ret_norm_g,
           lru_conv_w, lru_conv_b, lru_w_a, lru_b_a, lru_w_x, lru_b_x, lru_lambda, gdn_conv_w, gdn_a_log,
           gdn_dt_bias, gdn_norm_g, w_out, ln_mix_g, ln_mix_b, ffn2_w_gate, ffn2_w_up, ffn2_w_down,
           ple_w_gate, ple_w_proj, ln_ffn2_g, ln_ffn2_b):
    B, S, _ = x.shape
    T = B * S
    row = lambda v: v.reshape(1, -1)
    cos, sin = _rope_tables(positions)
    consts = _mixer_constants()
    bf = lambda w: w.astype(BF16)
    ffn1_w, ffn2_w = (bf(ffn1_w_gate), bf(ffn1_w_up), bf(ffn1_w_down)), (bf(ffn2_w_gate), bf(ffn2_w_up), bf(ffn2_w_down))
    ple_w = (bf(ple_w_gate), bf(ple_w_proj))
    w_in_all, w_out_all = _relayout_w_in(w_in), bf(w_out)
    for i in range(DEPTH):
        x2 = _ffn(x.reshape(T, D_MODEL), i, *ffn1_w, row(ln_ffn1_g[i]), row(ln_ffn1_b[i]))
        mix_w = dict(
            layer=i, w_in=w_in_all, w_out=w_out_all,
            ret_norm_g=row(ret_norm_g[i]),
            lru_conv_w=_pad_rows(lru_conv_w[i], SUBLANES), lru_conv_b=row(lru_conv_b[i]),
            lru_wax=jnp.concatenate([_pair_block_diag(lru_w_a[i]), _pair_block_diag(lru_w_x[i])], axis=2).astype(BF16),
            lru_b_a=row(lru_b_a[i]), lru_b_x=row(lru_b_x[i]), lru_lambda=row(lru_lambda[i]),
            gdn_conv_w=_pad_rows(gdn_conv_w[i], SUBLANES),
            gdn_a_log=_head_row(gdn_a_log[i]), gdn_dt_bias=_head_row(gdn_dt_bias[i]),
            gdn_norm_g=row(jnp.tile(gdn_norm_g[i], GDN_HEADS)),
            ln_g=row(ln_mix_g[i]), ln_b=row(ln_mix_b[i]),
        )
        x3 = _mixer(x2.reshape(B, S, D_MODEL), cos, sin, mix_w, consts)
        x = _ffn(x3.reshape(T, D_MODEL), i, *ffn2_w, row(ln_ffn2_g[i]), row(ln_ffn2_b[i]),
                 ple=(p.reshape(DEPTH, T, PLE_DIM),) + ple_w)
        x = x.reshape(B, S, D_MODEL)
    return x
```

```python
import functools

import numpy as np
import jax
import jax.numpy as jnp
from jax import lax
from jax.experimental import pallas as pl
from jax.experimental.pallas import tpu as pltpu

F32 = jnp.float32
BF16 = jnp.bfloat16

D_MODEL = 1024
DEPTH = 2
CHUNK = 64
PLE_DIM = 256
D_FF = 2816
CONV_W = 4
RET_HEADS = 4
RET_DK = 64
RET_W = 256
LRU_W = 384
LRU_BLOCK = 64
LRU_C = 8.0
GDN_HEADS = 6
GDN_DK = 64
GDN_W = 384
IN_WIDTHS = (RET_W, RET_W, RET_W, RET_W, LRU_W, LRU_W, GDN_W, GDN_W, GDN_W, GDN_W, GDN_HEADS, GDN_HEADS)
D_IN = sum(IN_WIDTHS)
ROPE_THETA = 10000.0
ALPHA = (2 * DEPTH) ** 0.25
LN_EPS = 1e-5
HEAD_NORM_EPS = 1e-5
L2_EPS = 1e-6
RMS_EPS = 1e-6

LANES = 128
SUBLANES = 8
VMEM_LIMIT_BYTES = 56 * 1024 * 1024

MIX_BLOCK = 512
RET_BLOCK = 256
SUPER = 128
FFN_ROWS = 1024
FFN_COLS = 256
FFN_SLABS = 4
ROPE_ROWS = 1024

C_RQ, C_RK, C_RV, C_RG = 0, 256, 512, 768
C_LX, C_LG = 1024, 1408
C_GQ, C_GK, C_GV, C_GZ = 1792, 2176, 2560, 2944
C_AB = 3328
D_IN_PAD = 3456


def _mm(a, b):
    return jnp.dot(a.astype(BF16), b.astype(BF16), preferred_element_type=F32)


def _mm_nt(a, b):
    return lax.dot_general(a.astype(BF16), b.astype(BF16), (((1,), (1,)), ((), ())),
                           preferred_element_type=F32)


def _mm_tn(a, b):
    return lax.dot_general(a.astype(BF16), b.astype(BF16), (((0,), (0,)), ((), ())),
                           preferred_element_type=F32)


def _bmm(a, b):
    return lax.dot_general(a.astype(BF16), b.astype(BF16), (((2,), (1,)), ((0,), (0,))),
                           preferred_element_type=F32)


def _split(x, parts):
    out = []
    r = x
    for _ in range(parts):
        t = r.astype(BF16)
        out.append(t)
        r = r - t.astype(F32)
    return out


def _mm_sel(sel, x, parts):
    acc = None
    for t in _split(x, parts):
        y = jnp.dot(sel, t, preferred_element_type=F32)
        acc = y if acc is None else acc + y
    return acc


def _mm_xsel(x, sel, parts):
    acc = None
    for t in _split(x, parts):
        y = jnp.dot(t, sel, preferred_element_type=F32)
        acc = y if acc is None else acc + y
    return acc


LOG2_E = 1.4426950408889634


def _sigmoid(x):
    return 1.0 / (1.0 + jnp.exp2(x * -LOG2_E))


def _silu(x):
    return x * _sigmoid(x)


def _softplus(x):
    return jnp.maximum(x, 0.0) + jnp.log(1.0 + jnp.exp(-jnp.abs(x)))


def _sqrt_nonneg(y):
    return jnp.where(y > 0.0, y * lax.rsqrt(y), 0.0)


def _gelu_tanh(x):
    c = np.float32(np.sqrt(2.0 / np.pi))
    return 0.5 * x * (1.0 + jnp.tanh(c * (x + 0.044715 * (x * x * x))))


def _layer_norm(r, g, b):
    mu = jnp.mean(r, axis=-1, keepdims=True)
    d = r - mu
    var = jnp.mean(d * d, axis=-1, keepdims=True)
    return d * lax.rsqrt(var + LN_EPS) * g + b


def _rope_body(pos_ref, invf_ref, cos_ref, sin_ref):
    half = RET_DK // 2
    per_row = LANES // half
    ang = pos_ref[...] * invf_ref[...]
    rows = ang.shape[0]
    group = jnp.right_shift(lax.broadcasted_iota(jnp.int32, (rows, LANES), 1), half.bit_length() - 1)
    for table, out_ref in ((jnp.cos(ang), cos_ref), (jnp.sin(ang), sin_ref)):
        for j in range(per_row):
            own = jnp.where(group == j, table, 0.0)
            spread = own
            for k in range(1, per_row):
                spread = spread + pltpu.roll(own, k * half, 1)
            out_ref[pl.ds(j, rows, stride=per_row), :] = spread


def _rope_tables(positions):
    B, S = positions.shape
    half = RET_DK // 2
    per_row = LANES // half
    T = B * S
    rows = T // per_row
    inv_freq = ROPE_THETA ** (-jnp.arange(half, dtype=F32) / half)
    pos_rep = jnp.broadcast_to(positions.astype(F32).reshape(rows, per_row, 1), (rows, per_row, half))
    invf = jnp.tile(inv_freq, per_row).reshape(1, LANES)
    blk = min(rows, ROPE_ROWS)
    cos, sin = pl.pallas_call(
        _rope_body,
        grid=(rows // blk,),
        in_specs=[pl.BlockSpec((blk, LANES), lambda i: (i, 0)),
                  pl.BlockSpec((1, LANES), lambda i: (0, 0))],
        out_specs=[pl.BlockSpec((per_row * blk, LANES), lambda i: (i, 0)),
                   pl.BlockSpec((per_row * blk, LANES), lambda i: (i, 0))],
        out_shape=[jax.ShapeDtypeStruct((T, LANES), F32)] * 2,
        name="rope_tables",
    )(pos_rep.reshape(rows, LANES), invf)
    return cos.reshape(B, S, LANES), sin.reshape(B, S, LANES)


def _ffn_body(has_ple, *refs):
    if has_ple:
        (x_ref, wg_ref, wu_ref, wd_ref, g_ref, b_ref, p_ref, pwg_ref, pwp_ref, o_ref, h_scr) = refs
    else:
        (x_ref, wg_ref, wu_ref, wd_ref, g_ref, b_ref, o_ref, h_scr) = refs
    x = x_ref[...]
    xb = x.astype(BF16)
    for c in range(0, D_FF, FFN_COLS):
        gate = jnp.dot(xb, wg_ref[:, c:c + FFN_COLS], preferred_element_type=F32)
        up = jnp.dot(xb, wu_ref[:, c:c + FFN_COLS], preferred_element_type=F32)
        h_scr[:, c:c + FFN_COLS] = (_silu(gate) * up).astype(BF16)
    r = ALPHA * x
    if has_ple:
        gate = _sigmoid(jnp.dot(xb, pwg_ref[...], preferred_element_type=F32))
        proj = jnp.dot(p_ref[...].astype(BF16), pwp_ref[...], preferred_element_type=F32)
        r = r + gate * proj
    n = x.shape[0]
    slabs = [slice(s, s + n // FFN_SLABS) for s in range(0, n, n // FFN_SLABS)]
    ys = [jnp.dot(h_scr[rows, :], wd_ref[...], preferred_element_type=F32) for rows in slabs]
    for rows, y in zip(slabs, ys):
        o_ref[rows, :] = _layer_norm(r[rows] + 0.5 * y, g_ref[...], b_ref[...])


def _const_spec(shape):
    nd = len(shape)
    return pl.BlockSpec(shape, lambda *_: (0,) * nd, pipeline_mode=pl.Buffered(1))


def _layer_spec(shape, layer):
    nd = len(shape)
    return pl.BlockSpec((None,) + tuple(shape[1:]), lambda *_: (layer,) + (0,) * (nd - 1),
                        pipeline_mode=pl.Buffered(1))


def _ffn(x2, layer, wg, wu, wd, ln_g, ln_b, ple=None):
    T = x2.shape[0]
    rows = min(FFN_ROWS, T)
    tok = lambda n: pl.BlockSpec((rows, n), lambda i: (i, 0))
    args = [x2, wg, wu, wd, ln_g, ln_b]
    specs = [tok(D_MODEL), _layer_spec(wg.shape, layer), _layer_spec(wu.shape, layer), _layer_spec(wd.shape, layer),
             _const_spec(ln_g.shape), _const_spec(ln_b.shape)]
    if ple is not None:
        p_all, pwg, pwp = ple
        args += [p_all, pwg, pwp]
        specs += [pl.BlockSpec((None, rows, PLE_DIM), lambda i: (layer, i, 0)),
                  _layer_spec(pwg.shape, layer), _layer_spec(pwp.shape, layer)]
    return pl.pallas_call(
        functools.partial(_ffn_body, ple is not None),
        grid=(T // rows,),
        in_specs=specs,
        out_specs=tok(D_MODEL),
        out_shape=jax.ShapeDtypeStruct((T, D_MODEL), F32),
        scratch_shapes=[pltpu.VMEM((rows, D_FF), BF16)],
        compiler_params=pltpu.CompilerParams(dimension_semantics=("arbitrary",),
                                             vmem_limit_bytes=VMEM_LIMIT_BYTES),
        name="ffn_ple" if ple is not None else "ffn",
    )(*args)


def _iota2(shape, dim):
    return lax.broadcasted_iota(jnp.int32, shape, dim)


def _drain(gen):
    for _ in gen:
        pass


def _retention_steps(proj, cos, sin, consts, norm_g, state_ref, out):
    mk_ref, mv_ref, decay_ref, cross_ref, tail_ref, sdec_ref, bd_ref, avg_ref = consts

    def rot(t):
        t1, t2 = t[:, :LANES], t[:, LANES:]
        return jnp.concatenate([t1 * cos - t2 * sin, t2 * cos + t1 * sin], axis=1)

    q_raw = proj(C_RQ, C_RK)
    k_raw = proj(C_RK, C_RV)
    yield
    v_all = proj(C_RV, C_RG)
    gate = proj(C_RG, C_LX)
    yield
    q_all = rot(q_raw) * (RET_DK ** -0.5)
    k_all = rot(k_raw)
    os = []
    for sb in range(q_all.shape[0] // RET_BLOCK):
        rows = slice(sb * RET_BLOCK, (sb + 1) * RET_BLOCK)
        q, k, v = q_all[rows], k_all[rows], v_all[rows]
        state = state_ref[...]
        o = _mm(q * cross_ref[...], state)
        state_ref[...] = state * sdec_ref[...] + _mm_tn(k * tail_ref[...], v) * bd_ref[...]
        yield
        qb, kb, vb = q.astype(BF16), k.astype(BF16), v.astype(BF16)
        for h in range(RET_HEADS):
            s = _mm_nt(qb * mk_ref[h:h + 1, :].astype(BF16), kb) * decay_ref[h]
            o = o + _mm(s, vb * mv_ref[h:h + 1, :].astype(BF16))
            yield
        os.append(o)
    o = jnp.concatenate(os, axis=0)
    avg = avg_ref[...]
    d = o - _mm(o, avg)
    var = _mm(d * d, avg)
    out["ret"] = d * lax.rsqrt(var + HEAD_NORM_EPS) * norm_g * _silu(gate)
    yield


def _rg_lru_conv(x_raw, tail_ref, conv_w_ref, conv_b, wax_ref):
    coeffs = []
    for p in range(LRU_W // LANES):
        c0, c1 = p * LANES, (p + 1) * LANES
        xc = _conv_rolled(x_raw[:, c0:c1], tail_ref, conv_w_ref, c0, c1) + conv_b[:, c0:c1]
        y = _mm(xc, wax_ref[p])
        coeffs.append((xc, y, c0, c1))
    return coeffs


def _rg_lru_gates(coeffs, b_a, b_x, lam):
    out = []
    for xc, y, c0, c1 in coeffs:
        r = _sigmoid(y[:, :LANES] + b_a[:, c0:c1])
        i = _sigmoid(y[:, LANES:] + b_x[:, c0:c1])
        log_a = (LRU_C * (-_softplus(-lam[:, c0:c1]))) * r
        t = jnp.tanh(log_a)
        out.append((jnp.exp(log_a), _sqrt_nonneg(-2.0 * t / (1.0 - t)) * (i * xc)))
    return out


def _rg_lru_scan(a, b, h_ref, c0, c1, L):
    G = L // SUBLANES
    a3 = a.reshape(G, SUBLANES, LANES)
    b3 = b.reshape(G, SUBLANES, LANES)
    sub = _iota2((G, SUBLANES, LANES), 1)
    for d in (1, 2, 4):
        ok = sub >= d
        a_s = pltpu.roll(a3, d, 1)
        b_s = pltpu.roll(b3, d, 1)
        b3 = b3 + jnp.where(ok, a3 * b_s, 0.0)
        a3 = jnp.where(ok, a3 * a_s, a3)
    a_last = jnp.broadcast_to(a3[:, SUBLANES - 1:SUBLANES, :], a3.shape)
    b_last = jnp.broadcast_to(b3[:, SUBLANES - 1:SUBLANES, :], b3.shape)
    carry = h_ref[:, c0:c1]
    hs = []
    for g in range(G):
        hs.append(b3[g] + a3[g] * carry)
        carry = b_last[g] + a_last[g] * carry
    h_ref[:, c0:c1] = carry
    return jnp.concatenate(hs, axis=0)


def _gdn_gates(ab, a_log_row, dt_row, tri_ref, e_gc_ref, e_beta_ref, L):
    g_n = -jnp.exp(a_log_row) * _softplus(ab + dt_row)
    tri = tri_ref[...]
    gc_n = jnp.concatenate([_mm_sel(tri, g_n[s * SUPER:(s + 1) * SUPER], 2) for s in range(L // SUPER)], axis=0)
    gc_h = _mm_xsel(gc_n, e_gc_ref[...], 2)
    beta_pair = _mm_xsel(_sigmoid(ab), e_beta_ref[...], 2)
    return gc_h, beta_pair


def _gdn_masks():
    lane = _iota2((SUPER, LANES), 1)
    row = _iota2((SUPER, LANES), 0)
    chunk_bits = CHUNK.bit_length() - 1
    same_chunk = jnp.right_shift(row, chunk_bits) == jnp.right_shift(lane, chunk_bits)
    first_half = lane < CHUNK

    def block(t, size):
        return jnp.right_shift(t, size.bit_length() - 1)

    def merge_mask(m):
        lower_left = jnp.logical_and(jnp.bitwise_and(block(row, m), 1) == 1, jnp.bitwise_and(block(lane, m), 1) == 0)
        return jnp.where(jnp.logical_and(block(row, 2 * m) == block(lane, 2 * m), lower_left), 1.0, 0.0).astype(F32)

    return dict(
        incl=jnp.logical_and(same_chunk, row >= lane), strict=jnp.logical_and(same_chunk, row > lane),
        eye=row == lane, first_half=first_half,
        diag8=jnp.where(block(row, SUBLANES) == block(lane, SUBLANES), 1.0, 0.0).astype(F32),
        merge_masks=tuple(merge_mask(SUBLANES << i) for i in range((CHUNK // SUBLANES).bit_length() - 1)),
        masks=(jnp.where(first_half, 1.0, 0.0).astype(F32)[0:1, :], jnp.where(first_half, 0.0, 1.0).astype(F32)[0:1, :]))


def _conv_rolled(cur, tail_ref, w_ref, c0, c1):
    L = cur.shape[0]
    tail = tail_ref[0:SUBLANES, c0:c1]
    sub = _iota2(tail.shape, 0)
    y = cur * w_ref[CONV_W - 1:CONV_W, c0:c1]
    for j in range(1, CONV_W):
        shifted = pltpu.roll(cur, j, 0)
        head = jnp.where(sub < j, pltpu.roll(tail, j, 0), shifted[0:SUBLANES])
        shifted = jnp.concatenate([head, shifted[SUBLANES:]], axis=0)
        y = y + shifted * w_ref[CONV_W - 1 - j:CONV_W - j, c0:c1]
    tail_ref[0:SUBLANES, c0:c1] = cur[L - SUBLANES:L]
    return y


def _gdn_conv(proj, tail_ref, conv_w_ref):
    raw = proj(C_GQ, C_GZ)
    return [_silu(_conv_rolled(raw[:, c:c + GDN_W], tail_ref, conv_w_ref, c, c + GDN_W))
            for c in range(0, 3 * GDN_W, GDN_W)]


def _gdn_prep(ctx, sc, acts):
    rs = slice(sc * SUPER, (sc + 1) * SUPER)
    q_act, k_act, v_act = acts
    bd2b = ctx["bd2"].astype(BF16)
    q, k, v = [], [], []
    for p in range(GDN_W // LANES):
        lanes = slice(p * LANES, (p + 1) * LANES)
        qk = jnp.concatenate([q_act[rs, lanes], k_act[rs, lanes]], axis=0)
        qk = qk * lax.rsqrt(_mm(qk * qk, bd2b) + L2_EPS)
        q.append(qk[:SUPER] * (GDN_DK ** -0.5))
        k.append(qk[SUPER:])
        v.append(v_act[rs, lanes])
    ctx["qkv"][sc] = (q, k, v)


def _gdn_parallel_steps(ctx, scs, out):
    n_pairs = GDN_W // LANES
    m_a, m_b = ctx["masks"]
    eye, strict, incl = ctx["eye"], ctx["strict"], ctx["incl"]
    mb_a, mb_b = m_a.astype(BF16), m_b.astype(BF16)
    kk, qk, rhs, gcs_all, per_sc = [], [], [], [], []
    for sc in scs:
        rs = slice(sc * SUPER, (sc + 1) * SUPER)
        q_p, k_p, v_p = ctx["qkv"][sc]
        gcs = jnp.stack([ctx["gc_h"][rs, h * LANES:(h + 1) * LANES] for h in range(GDN_HEADS)])
        gcs_all.append(gcs)
        gc_pair, q_dec = [], []
        for p in range(n_pairs):
            kp, qp, vp = k_p[p], q_p[p], v_p[p]
            bp = ctx["beta"][rs, p * LANES:(p + 1) * LANES]
            gcp = jnp.where(ctx["first_half"], gcs[2 * p], gcs[2 * p + 1])
            eg = jnp.exp(gcp)
            kbb = kp * bp
            kbb_b, qp_b = kbb.astype(BF16), qp.astype(BF16)
            y = _mm_nt(jnp.concatenate([kbb_b * mb_a, kbb_b * mb_b, qp_b * mb_a, qp_b * mb_b], axis=0), kp)
            kk += [y[0:SUPER], y[SUPER:2 * SUPER]]
            qk += [y[2 * SUPER:3 * SUPER], y[3 * SUPER:4 * SUPER]]
            r_p = jnp.concatenate([vp * bp, kbb * eg], axis=1).astype(BF16)
            rhs += [r_p * jnp.concatenate([mb_a, mb_a], axis=1), r_p * jnp.concatenate([mb_b, mb_b], axis=1)]
            gc_pair.append(gcp)
            q_dec.append(qp * eg)
        per_sc.append(dict(gc_pair=gc_pair, q_dec=q_dec, k=k_p))
    gcs = jnp.concatenate(gcs_all, axis=0)
    rowterm = jnp.sum(jnp.where(eye[None], gcs, 0.0), axis=1, keepdims=True)
    dec = jnp.exp(jnp.minimum(gcs - rowterm, 0.0))
    yield
    n_mat = jnp.where(strict[None], jnp.stack(kk) * dec, 0.0)
    qkd = jnp.where(incl[None], jnp.stack(qk) * dec, 0.0)
    n8 = n_mat * ctx["diag8"]
    x_inv = jnp.where(eye, 1.0, 0.0)[None] - n8
    pw = _bmm(n8, n8)
    yield
    pw = pw.astype(BF16)
    y = _bmm(jnp.concatenate([pw, x_inv.astype(BF16)], axis=1), pw)
    x_inv = x_inv + y[:, SUPER:]
    yield
    x_inv = x_inv + _bmm(x_inv, y[:, :SUPER])
    yield
    for off in ctx["merge_masks"]:
        t_a = _bmm(x_inv, n_mat * off)
        yield
        x_inv = x_inv - _bmm(t_a, x_inv)
        yield
    uw = _bmm(x_inv, jnp.stack(rhs))
    for i, sc in enumerate(scs):
        o = GDN_HEADS * i
        out[sc] = dict(
            u=[uw[o + 2 * p, :, :LANES] + uw[o + 2 * p + 1, :, :LANES] for p in range(n_pairs)],
            w=[uw[o + 2 * p, :, LANES:] + uw[o + 2 * p + 1, :, LANES:] for p in range(n_pairs)],
            qkd=qkd[o:o + GDN_HEADS], **per_sc[i])
    yield


def _gdn_recurrence_steps(ctx, par, state_ref, o_rows):
    n_pairs = GDN_W // LANES
    mb_a, mb_b = (m.astype(BF16) for m in ctx["masks"])
    bd2 = ctx["bd2"]
    zeros = jnp.zeros((CHUNK, LANES), BF16)
    for c in range(SUPER // CHUNK):
        cs = slice(c * CHUNK, (c + 1) * CHUNK)
        last = c * CHUNK + CHUNK - 1
        ys = [_mm(jnp.concatenate([par["w"][p][cs], par["q_dec"][p][cs]], axis=0), state_ref[p])
              for p in range(n_pairs)]
        yield
        for p in range(n_pairs):
            gcp = par["gc_pair"][p]
            gl_row = gcp[last:last + 1, :]
            v_new = par["u"][p][cs] - ys[p][:CHUNK]
            k_tail = par["k"][p][cs] * jnp.exp(gl_row - gcp[cs])
            v_new_b = v_new.astype(BF16)
            vm_a, vm_b = v_new_b * mb_a, v_new_b * mb_b
            v_cat = jnp.concatenate([vm_a, zeros, vm_b, zeros] if c == 0 else [zeros, vm_a, zeros, vm_b], axis=0)
            q_cat = jnp.concatenate([par["qkd"][2 * p][cs], par["qkd"][2 * p + 1][cs]], axis=1)
            o_rows[p].append(ys[p][CHUNK:] + _mm(q_cat, v_cat))
            state_ref[p] = state_ref[p] * jnp.exp(gl_row) + _mm_tn(k_tail, v_new_b) * bd2
        yield


def _mixer_body(L, x_ref, cos_ref, sin_ref, win_ref, wout_ref,
                mk_ref, mv_ref, decay_ref, cross_ref, tail_ref, sdec_ref, bdr_ref, avg_ref, rng_ref,
                lcw_ref, lcb_ref, wax_ref, lba_ref, lbx_ref, lam_ref,
                gcw_ref, galog_ref, gdt_ref, gng_ref, tri_ref, egc_ref, ebeta_ref, bd2_ref, avg2_ref,
                lng_ref, lnb_ref,
                o_ref, ret_state, gdn_state, lru_h, lru_buf, gdn_buf):
    @pl.when(pl.program_id(1) == 0)
    def _reset():
        ret_state[...] = jnp.zeros_like(ret_state)
        gdn_state[...] = jnp.zeros_like(gdn_state)
        lru_h[...] = jnp.zeros_like(lru_h)
        lru_buf[...] = jnp.zeros_like(lru_buf)
        gdn_buf[...] = jnp.zeros_like(gdn_buf)

    x = x_ref[...]
    xb = x.astype(BF16)

    def proj(lo, hi):
        return jnp.dot(xb, win_ref[:, lo:hi], preferred_element_type=F32)

    n_pairs = GDN_W // LANES
    n_lru = LRU_W // LANES
    n_super = L // SUPER
    gdn_acts = _gdn_conv(proj, gdn_buf, gcw_ref)
    ab = proj(C_AB, D_IN_PAD)
    lru_x = proj(C_LX, C_LG)
    lru_gate = proj(C_LG, C_GQ)
    gc_h, beta_pair = _gdn_gates(ab, galog_ref[...], gdt_ref[...], tri_ref, egc_ref, ebeta_ref, L)
    ctx = dict(gc_h=gc_h, beta=beta_pair, bd2=bd2_ref[...], qkv={}, **_gdn_masks())
    _gdn_prep(ctx, 0, gdn_acts)
    lru_pre = _rg_lru_conv(lru_x, lru_buf, lcw_ref, lcb_ref[...], wax_ref)
    heads = {}
    ret_gen = _retention_steps(proj, cos_ref[...], sin_ref[...],
                               (mk_ref, mv_ref, decay_ref, cross_ref, tail_ref, sdec_ref, bdr_ref, avg_ref),
                               rng_ref[...], ret_state, heads)
    next(ret_gen)
    lru_ab = _rg_lru_gates(lru_pre, lba_ref[...], lbx_ref[...], lam_ref[...])
    next(ret_gen)
    if n_super > 1:
        _gdn_prep(ctx, 1, gdn_acts)
    par = {}
    o_rows = [[] for _ in range(n_pairs)]
    lru_cols = []
    mix_l = []

    def lru_out_steps():
        o_l = (jnp.concatenate(lru_cols, axis=1) * _gelu_tanh(lru_gate)).astype(BF16)
        for c in range(0, D_MODEL, RET_W):
            mix_l.append(jnp.dot(o_l, wout_ref[RET_W:RET_W + LRU_W, c:c + RET_W], preferred_element_type=F32))
            yield

    pairs = [tuple(range(s, min(s + 2, n_super))) for s in range(0, n_super, 2)]
    _drain(_gdn_parallel_steps(ctx, pairs[0], par))
    for b, scs in enumerate(pairs):
        last = b + 1 == len(pairs)
        if not last:
            for sc in pairs[b + 1]:
                _gdn_prep(ctx, sc, gdn_acts)
        while len(lru_cols) < (n_lru if last else min(n_lru, 2 * (b + 1))):
            t = len(lru_cols)
            lru_cols.append(_rg_lru_scan(lru_ab[t][0], lru_ab[t][1], lru_h, t * LANES, (t + 1) * LANES, L))
        nxt = iter(()) if last else _gdn_parallel_steps(ctx, pairs[b + 1], par)
        lru_out = lru_out_steps() if last else iter(())
        step = 0
        for sc in scs:
            for _ in _gdn_recurrence_steps(ctx, par[sc], gdn_state, o_rows):
                next(nxt, None)
                next(nxt, None)
                if last:
                    next(ret_gen, None)
                    if step % 2 == 0:
                        next(ret_gen, None)
                        next(lru_out, None)
                step += 1
        _drain(nxt)
    _drain(ret_gen)
    _drain(lru_out)

    mix = _mm(heads["ret"], wout_ref[0:RET_W, :]) + jnp.concatenate(mix_l, axis=1)
    avg2 = avg2_ref[...]
    z = proj(C_GZ, C_AB)
    outs = []
    for p in range(n_pairs):
        op = jnp.concatenate(o_rows[p], axis=0)
        outs.append(op * lax.rsqrt(_mm(op * op, avg2) + RMS_EPS))
    o_g = jnp.concatenate(outs, axis=1) * gng_ref[...] * _silu(z)
    w_g = wout_ref[RET_W + LRU_W:, :]
    slabs = [slice(r, r + RET_BLOCK) for r in range(0, L, RET_BLOCK)]
    mix_g = [_mm(o_g[rows], w_g) for rows in slabs]
    for rows, m_g in zip(slabs, mix_g):
        o_ref[rows, :] = _layer_norm(ALPHA * x[rows] + mix[rows] + m_g, lng_ref[...], lnb_ref[...])


def _mixer_constants():
    L = RET_BLOCK
    lg = jnp.log1p(-jnp.exp2(-5.0 - jnp.arange(RET_HEADS, dtype=F32)))
    lane = np.arange(RET_W)
    khead = (lane % LANES) // (RET_DK // 2)
    vhead = lane // RET_DK
    mk = jnp.asarray(np.pad((khead[None, :] == np.arange(RET_HEADS)[:, None]).astype(np.float32), ((0, 4), (0, 0))))
    mv = jnp.asarray(np.pad((vhead[None, :] == np.arange(RET_HEADS)[:, None]).astype(np.float32), ((0, 4), (0, 0))))
    idx = np.arange(L)
    diff = jnp.asarray((idx[:, None] - idx[None, :]).astype(np.float32))
    same = jnp.asarray(idx[:, None] // CHUNK == idx[None, :] // CHUNK)
    earlier = jnp.asarray(idx[None, :] // CHUNK < idx[:, None] // CHUNK)
    decay = jnp.where(same[None], jnp.exp(jnp.abs(diff)[None] * lg[:, None, None]),
                      jnp.where(earlier[None], jnp.exp(diff[None] * lg[:, None, None]), 0.0))
    lg_lane = lg[jnp.asarray(khead)]
    fidx = jnp.asarray(idx.astype(np.float32))
    cross = jnp.exp((fidx + 1.0)[:, None] * lg_lane[None, :])
    tail = jnp.exp((L - 1.0 - fidx)[:, None] * lg_lane[None, :])
    sdec = jnp.broadcast_to(jnp.exp(float(L) * lg_lane)[:, None], (RET_W, RET_W))
    bdr = jnp.asarray((khead[:, None] == vhead[None, :]).astype(np.float32))
    avg = jnp.asarray((vhead[:, None] == vhead[None, :]).astype(np.float32) / RET_DK).astype(BF16)
    sidx = np.arange(SUPER)
    tri = jnp.asarray(np.logical_and(sidx[:, None] // CHUNK == sidx[None, :] // CHUNK,
                                     sidx[:, None] >= sidx[None, :]).astype(np.float32)).astype(BF16)
    e_gc = np.zeros((LANES, GDN_HEADS * LANES), np.float32)
    e_beta = np.zeros((LANES, GDN_W), np.float32)
    for h in range(GDN_HEADS):
        e_gc[h, h * LANES:(h + 1) * LANES] = 1.0
        e_beta[GDN_HEADS + h, h * GDN_DK:(h + 1) * GDN_DK] = 1.0
    l2 = np.arange(LANES) // GDN_DK
    bd2 = (l2[:, None] == l2[None, :]).astype(np.float32)
    return dict(mk=mk, mv=mv, decay=decay, cross=cross, tail=tail, sdec=sdec, bdr=bdr, avg=avg, tri=tri,
                e_gc=jnp.asarray(e_gc).astype(BF16), e_beta=jnp.asarray(e_beta).astype(BF16),
                bd2=jnp.asarray(bd2), avg2=jnp.asarray(bd2 / GDN_DK).astype(BF16))


def _mixer(x, cos, sin, w, consts):
    B, S, _ = x.shape
    L = min(MIX_BLOCK, S)
    tok = lambda n: pl.BlockSpec((None, L, n), lambda b, j: (b, j, 0))
    c = consts
    operands = [
        (x, tok(D_MODEL)), (cos, tok(LANES)), (sin, tok(LANES)),
        (w["w_in"], _layer_spec(w["w_in"].shape, w["layer"])), (w["w_out"], _layer_spec(w["w_out"].shape, w["layer"])),
        (c["mk"], None), (c["mv"], None), (c["decay"], None), (c["cross"], None), (c["tail"], None),
        (c["sdec"], None), (c["bdr"], None), (c["avg"], None), (w["ret_norm_g"], None),
        (w["lru_conv_w"], None), (w["lru_conv_b"], None), (w["lru_wax"], None), (w["lru_b_a"], None),
        (w["lru_b_x"], None), (w["lru_lambda"], None),
        (w["gdn_conv_w"], None), (w["gdn_a_log"], None), (w["gdn_dt_bias"], None), (w["gdn_norm_g"], None),
        (c["tri"], None), (c["e_gc"], None), (c["e_beta"], None), (c["bd2"], None), (c["avg2"], None),
        (w["ln_g"], None), (w["ln_b"], None),
    ]
    args = [a for a, _ in operands]
    specs = [s if s is not None else _const_spec(a.shape) for a, s in operands]
    return pl.pallas_call(
        functools.partial(_mixer_body, L),
        grid=(B, S // L),
        in_specs=specs,
        out_specs=tok(D_MODEL),
        out_shape=jax.ShapeDtypeStruct((B, S, D_MODEL), F32),
        scratch_shapes=[
            pltpu.VMEM((RET_W, RET_W), F32),
            pltpu.VMEM((GDN_W // LANES, LANES, LANES), F32),
            pltpu.VMEM((SUBLANES, LRU_W), F32),
            pltpu.VMEM((SUBLANES, LRU_W), F32),
            pltpu.VMEM((SUBLANES, 3 * GDN_W), F32),
        ],
        compiler_params=pltpu.CompilerParams(dimension_semantics=("arbitrary", "arbitrary"),
                                             vmem_limit_bytes=VMEM_LIMIT_BYTES),
        name="mixer",
    )(*args)


def _relayout_w_in(w):
    n, d = w.shape[0], w.shape[1]
    half = RET_DK // 2

    def halves_apart(cols):
        return cols.reshape(n, d, RET_HEADS, 2, half).transpose(0, 1, 3, 2, 4).reshape(n, d, RET_W)

    parts = [halves_apart(w[..., 0:RET_W]), halves_apart(w[..., RET_W:2 * RET_W]), w[..., 2 * RET_W:],
             jnp.zeros((n, d, D_IN_PAD - D_IN), w.dtype)]
    return jnp.concatenate(parts, axis=2).astype(BF16)


def _pair_block_diag(w):
    z = jnp.zeros((LRU_BLOCK, LRU_BLOCK), w.dtype)
    slabs = []
    for p in range(LRU_W // LANES):
        top = jnp.concatenate([w[2 * p], z], axis=1)
        bot = jnp.concatenate([z, w[2 * p + 1]], axis=1)
        slabs.append(jnp.concatenate([top, bot], axis=0))
    return jnp.stack(slabs)


def _pad_rows(a, rows):
    return jnp.pad(a, ((0, rows - a.shape[0]), (0, 0)))


def _head_row(v):
    return jnp.pad(v, (0, LANES - v.shape[0])).reshape(1, LANES)


def kernel(x, p, positions, ln_ffn1_g, ln_ffn1_b, ffn1_w_gate, ffn1_w_up, ffn1_w_down, w_in, ret_norm_g,
           lru_conv_w, lru_conv_b, lru_w_a, lru_b_a, lru_w_x, lru_b_x, lru_lambda, gdn_conv_w, gdn_a_log,
           gdn_dt_bias, gdn_norm_g, w_out, ln_mix_g, ln_mix_b, ffn2_w_gate, ffn2_w_up, ffn2_w_down,
           ple_w_gate, ple_w_proj, ln_ffn2_g, ln_ffn2_b):
    B, S, _ = x.shape
    T = B * S
    row = lambda v: v.reshape(1, -1)
    cos, sin = _rope_tables(positions)
    consts = _mixer_constants()
    bf = lambda w: w.astype(BF16)
    ffn1_w, ffn2_w = (bf(ffn1_w_gate), bf(ffn1_w_up), bf(ffn1_w_down)), (bf(ffn2_w_gate), bf(ffn2_w_up), bf(ffn2_w_down))
    ple_w = (bf(ple_w_gate), bf(ple_w_proj))
    w_in_all, w_out_all = _relayout_w_in(w_in), bf(w_out)
    for i in range(DEPTH):
        x2 = _ffn(x.reshape(T, D_MODEL), i, *ffn1_w, row(ln_ffn1_g[i]), row(ln_ffn1_b[i]))
        mix_w = dict(
            layer=i, w_in=w_in_all, w_out=w_out_all,
            ret_norm_g=row(ret_norm_g[i]),
            lru_conv_w=_pad_rows(lru_conv_w[i], SUBLANES), lru_conv_b=row(lru_conv_b[i]),
            lru_wax=jnp.concatenate([_pair_block_diag(lru_w_a[i]), _pair_block_diag(lru_w_x[i])], axis=2).astype(BF16),
            lru_b_a=row(lru_b_a[i]), lru_b_x=row(lru_b_x[i]), lru_lambda=row(lru_lambda[i]),
            gdn_conv_w=_pad_rows(gdn_conv_w[i], SUBLANES),
            gdn_a_log=_head_row(gdn_a_log[i]), gdn_dt_bias=_head_row(gdn_dt_bias[i]),
            gdn_norm_g=row(jnp.tile(gdn_norm_g[i], GDN_HEADS)),
            ln_g=row(ln_mix_g[i]), ln_b=row(ln_mix_b[i]),
        )
        x3 = _mixer(x2.reshape(B, S, D_MODEL), cos, sin, mix_w, consts)
        x = _ffn(x3.reshape(T, D_MODEL), i, *ffn2_w, row(ln_ffn2_g[i]), row(ln_ffn2_b[i]),
                 ple=(p.reshape(DEPTH, T, PLE_DIM),) + ple_w)
        x = x.reshape(B, S, D_MODEL)
    return x
```

```python
import functools

import numpy as np
import jax
import jax.numpy as jnp
from jax import lax
from jax.experimental import pallas as pl
from jax.experimental.pallas import tpu as pltpu

F32 = jnp.float32
BF16 = jnp.bfloat16

D_MODEL = 1024
DEPTH = 2
CHUNK = 64
PLE_DIM = 256
D_FF = 2816
CONV_W = 4
RET_HEADS = 4
RET_DK = 64
RET_W = 256
LRU_W = 384
LRU_BLOCK = 64
LRU_C = 8.0
GDN_HEADS = 6
GDN_DK = 64
GDN_W = 384
IN_WIDTHS = (RET_W, RET_W, RET_W, RET_W, LRU_W, LRU_W, GDN_W, GDN_W, GDN_W, GDN_W, GDN_HEADS, GDN_HEADS)
D_IN = sum(IN_WIDTHS)
ROPE_THETA = 10000.0
ALPHA = (2 * DEPTH) ** 0.25
LN_EPS = 1e-5
HEAD_NORM_EPS = 1e-5
L2_EPS = 1e-6
RMS_EPS = 1e-6

LANES = 128
SUBLANES = 8
VMEM_LIMIT_BYTES = 56 * 1024 * 1024

MIX_BLOCK = 512
RET_BLOCK = 256
SUPER = 128
FFN_ROWS = 1024
FFN_COLS = 256
FFN_SLABS = 4
ROPE_ROWS = 1024

C_RQ, C_RK, C_RV, C_RG = 0, 256, 512, 768
C_LX, C_LG = 1024, 1408
C_GQ, C_GK, C_GV, C_GZ = 1792, 2176, 2560, 2944
C_AB = 3328
D_IN_PAD = 3456


def _mm(a, b):
    return jnp.dot(a.astype(BF16), b.astype(BF16), preferred_element_type=F32)


def _mm_nt(a, b):
    return lax.dot_general(a.astype(BF16), b.astype(BF16), (((1,), (1,)), ((), ())),
                           preferred_element_type=F32)


def _mm_tn(a, b):
    return lax.dot_general(a.astype(BF16), b.astype(BF16), (((0,), (0,)), ((), ())),
                           preferred_element_type=F32)


def _bmm(a, b):
    return lax.dot_general(a.astype(BF16), b.astype(BF16), (((2,), (1,)), ((0,), (0,))),
                           preferred_element_type=F32)


def _split(x, parts):
    out = []
    r = x
    for _ in range(parts):
        t = r.astype(BF16)
        out.append(t)
        r = r - t.astype(F32)
    return out


def _mm_sel(sel, x, parts):
    acc = None
    for t in _split(x, parts):
        y = jnp.dot(sel, t, preferred_element_type=F32)
        acc = y if acc is None else acc + y
    return acc


def _mm_xsel(x, sel, parts):
    acc = None
    for t in _split(x, parts):
        y = jnp.dot(t, sel, preferred_element_type=F32)
        acc = y if acc is None else acc + y
    return acc


LOG2_E = 1.4426950408889634


def _sigmoid(x):
    return 1.0 / (1.0 + jnp.exp2(x * -LOG2_E))


def _silu(x):
    return x * _sigmoid(x)


def _softplus(x):
    return jnp.maximum(x, 0.0) + jnp.log(1.0 + jnp.exp(-jnp.abs(x)))


def _sqrt_nonneg(y):
    return jnp.where(y > 0.0, y * lax.rsqrt(y), 0.0)


def _gelu_tanh(x):
    c = np.float32(np.sqrt(2.0 / np.pi))
    return 0.5 * x * (1.0 + jnp.tanh(c * (x + 0.044715 * (x * x * x))))


def _layer_norm(r, g, b):
    mu = jnp.mean(r, axis=-1, keepdims=True)
    d = r - mu
    var = jnp.mean(d * d, axis=-1, keepdims=True)
    return d * lax.rsqrt(var + LN_EPS) * g + b


def _rope_body(pos_ref, invf_ref, cos_ref, sin_ref):
    half = RET_DK // 2
    per_row = LANES // half
    ang = pos_ref[...] * invf_ref[...]
    rows = ang.shape[0]
    group = jnp.right_shift(lax.broadcasted_iota(jnp.int32, (rows, LANES), 1), half.bit_length() - 1)
    for table, out_ref in ((jnp.cos(ang), cos_ref), (jnp.sin(ang), sin_ref)):
        for j in range(per_row):
            own = jnp.where(group == j, table, 0.0)
            spread = own
            for k in range(1, per_row):
                spread = spread + pltpu.roll(own, k * half, 1)
            out_ref[pl.ds(j, rows, stride=per_row), :] = spread


def _rope_tables(positions):
    B, S = positions.shape
    half = RET_DK // 2
    per_row = LANES // half
    T = B * S
    rows = T // per_row
    inv_freq = ROPE_THETA ** (-jnp.arange(half, dtype=F32) / half)
    pos_rep = jnp.broadcast_to(positions.astype(F32).reshape(rows, per_row, 1), (rows, per_row, half))
    invf = jnp.tile(inv_freq, per_row).reshape(1, LANES)
    blk = min(rows, ROPE_ROWS)
    cos, sin = pl.pallas_call(
        _rope_body,
        grid=(rows // blk,),
        in_specs=[pl.BlockSpec((blk, LANES), lambda i: (i, 0)),
                  pl.BlockSpec((1, LANES), lambda i: (0, 0))],
        out_specs=[pl.BlockSpec((per_row * blk, LANES), lambda i: (i, 0)),
                   pl.BlockSpec((per_row * blk, LANES), lambda i: (i, 0))],
        out_shape=[jax.ShapeDtypeStruct((T, LANES), F32)] * 2,
        name="rope_tables",
    )(pos_rep.reshape(rows, LANES), invf)
    return cos.reshape(B, S, LANES), sin.reshape(B, S, LANES)


def _ffn_body(has_ple, *refs):
    if has_ple:
        (x_ref, wgu_ref, wd_ref, g_ref, b_ref, p_ref, pwg_ref, pwp_ref, o_ref, h_scr) = refs
    else:
        (x_ref, wgu_ref, wd_ref, g_ref, b_ref, o_ref, h_scr) = refs
    x = x_ref[...]
    xb = x.astype(BF16)
    for c in range(0, D_FF, FFN_COLS):
        gu = jnp.dot(xb, wgu_ref[:, 2 * c:2 * (c + FFN_COLS)], preferred_element_type=F32)
        h_scr[:, c:c + FFN_COLS] = (_silu(gu[:, :FFN_COLS]) * gu[:, FFN_COLS:]).astype(BF16)
    r = ALPHA * x
    if has_ple:
        gate = _sigmoid(jnp.dot(xb, pwg_ref[...], preferred_element_type=F32))
        proj = jnp.dot(p_ref[...].astype(BF16), pwp_ref[...], preferred_element_type=F32)
        r = r + gate * proj
    n = x.shape[0]
    slabs = [slice(s, s + n // FFN_SLABS) for s in range(0, n, n // FFN_SLABS)]
    ys = [jnp.dot(h_scr[rows, :], wd_ref[...], preferred_element_type=F32) for rows in slabs]
    for rows, y in zip(slabs, ys):
        o_ref[rows, :] = _layer_norm(r[rows] + 0.5 * y, g_ref[...], b_ref[...])


def _const_spec(shape):
    nd = len(shape)
    return pl.BlockSpec(shape, lambda *_: (0,) * nd, pipeline_mode=pl.Buffered(1))


def _layer_spec(shape, layer):
    nd = len(shape)
    return pl.BlockSpec((None,) + tuple(shape[1:]), lambda *_: (layer,) + (0,) * (nd - 1),
                        pipeline_mode=pl.Buffered(1))


def _interleave_gate_up(wg, wu):
    n, d = wg.shape[0], wg.shape[1]
    parts = jnp.stack([wg.reshape(n, d, D_FF // FFN_COLS, FFN_COLS), wu.reshape(n, d, D_FF // FFN_COLS, FFN_COLS)], axis=3)
    return parts.reshape(n, d, 2 * D_FF).astype(BF16)


def _ffn(x2, layer, wgu, wd, ln_g, ln_b, ple=None):
    T = x2.shape[0]
    rows = min(FFN_ROWS, T)
    tok = lambda n: pl.BlockSpec((rows, n), lambda i: (i, 0))
    args = [x2, wgu, wd, ln_g, ln_b]
    specs = [tok(D_MODEL), _layer_spec(wgu.shape, layer), _layer_spec(wd.shape, layer),
             _const_spec(ln_g.shape), _const_spec(ln_b.shape)]
    if ple is not None:
        p_all, pwg, pwp = ple
        args += [p_all, pwg, pwp]
        specs += [pl.BlockSpec((None, rows, PLE_DIM), lambda i: (layer, i, 0)),
                  _layer_spec(pwg.shape, layer), _layer_spec(pwp.shape, layer)]
    return pl.pallas_call(
        functools.partial(_ffn_body, ple is not None),
        grid=(T // rows,),
        in_specs=specs,
        out_specs=tok(D_MODEL),
        out_shape=jax.ShapeDtypeStruct((T, D_MODEL), F32),
        scratch_shapes=[pltpu.VMEM((rows, D_FF), BF16)],
        compiler_params=pltpu.CompilerParams(dimension_semantics=("arbitrary",),
                                             vmem_limit_bytes=VMEM_LIMIT_BYTES),
        name="ffn_ple" if ple is not None else "ffn",
    )(*args)


def _iota2(shape, dim):
    return lax.broadcasted_iota(jnp.int32, shape, dim)


def _drain(gen):
    for _ in gen:
        pass


def _retention_steps(proj, cos, sin, consts, norm_g, state_ref, out):
    mk_ref, mv_ref, decay_ref, cross_ref, tail_ref, sdec_ref, bd_ref, avg_ref = consts

    def rot(t):
        t1, t2 = t[:, :LANES], t[:, LANES:]
        return jnp.concatenate([t1 * cos - t2 * sin, t2 * cos + t1 * sin], axis=1)

    q_raw = proj(C_RQ, C_RK)
    k_raw = proj(C_RK, C_RV)
    yield
    v_all = proj(C_RV, C_RG)
    gate = proj(C_RG, C_LX)
    yield
    q_all = rot(q_raw) * (RET_DK ** -0.5)
    k_all = rot(k_raw)
    os = []
    for sb in range(q_all.shape[0] // RET_BLOCK):
        rows = slice(sb * RET_BLOCK, (sb + 1) * RET_BLOCK)
        q, k, v = q_all[rows], k_all[rows], v_all[rows]
        state = state_ref[...]
        o = _mm(q * cross_ref[...], state)
        state_ref[...] = state * sdec_ref[...] + _mm_tn(k * tail_ref[...], v) * bd_ref[...]
        yield
        qb, kb, vb = q.astype(BF16), k.astype(BF16), v.astype(BF16)
        for h in range(RET_HEADS):
            s = _mm_nt(qb * mk_ref[h:h + 1, :].astype(BF16), kb) * decay_ref[h]
            o = o + _mm(s, vb * mv_ref[h:h + 1, :].astype(BF16))
            yield
        os.append(o)
    o = jnp.concatenate(os, axis=0)
    avg = avg_ref[...]
    d = o - _mm(o, avg)
    var = _mm(d * d, avg)
    out["ret"] = d * lax.rsqrt(var + HEAD_NORM_EPS) * norm_g * _silu(gate)
    yield


def _rg_lru_conv(x_raw, tail_ref, conv_w_ref, conv_b, wax_ref):
    coeffs = []
    for p in range(LRU_W // LANES):
        c0, c1 = p * LANES, (p + 1) * LANES
        xc = _conv_rolled(x_raw[:, c0:c1], tail_ref, conv_w_ref, c0, c1) + conv_b[:, c0:c1]
        y = _mm(xc, wax_ref[p])
        coeffs.append((xc, y, c0, c1))
    return coeffs


def _rg_lru_gates(coeffs, b_a, b_x, lam):
    out = []
    for xc, y, c0, c1 in coeffs:
        r = _sigmoid(y[:, :LANES] + b_a[:, c0:c1])
        i = _sigmoid(y[:, LANES:] + b_x[:, c0:c1])
        log_a = (LRU_C * (-_softplus(-lam[:, c0:c1]))) * r
        t = jnp.tanh(log_a)
        out.append((jnp.exp(log_a), _sqrt_nonneg(-2.0 * t / (1.0 - t)) * (i * xc)))
    return out


def _rg_lru_scan(a, b, h_ref, c0, c1, L):
    G = L // SUBLANES
    a3 = a.reshape(G, SUBLANES, LANES)
    b3 = b.reshape(G, SUBLANES, LANES)
    sub = _iota2((G, SUBLANES, LANES), 1)
    for d in (1, 2, 4):
        ok = sub >= d
        a_s = pltpu.roll(a3, d, 1)
        b_s = pltpu.roll(b3, d, 1)
        b3 = b3 + jnp.where(ok, a3 * b_s, 0.0)
        a3 = jnp.where(ok, a3 * a_s, a3)
    a_last = jnp.broadcast_to(a3[:, SUBLANES - 1:SUBLANES, :], a3.shape)
    b_last = jnp.broadcast_to(b3[:, SUBLANES - 1:SUBLANES, :], b3.shape)
    carry = h_ref[:, c0:c1]
    hs = []
    for g in range(G):
        hs.append(b3[g] + a3[g] * carry)
        carry = b_last[g] + a_last[g] * carry
    h_ref[:, c0:c1] = carry
    return jnp.concatenate(hs, axis=0)


def _gdn_gates(ab, a_log_row, dt_row, tri_ref, e_gc_ref, e_beta_ref, L):
    g_n = -jnp.exp(a_log_row) * _softplus(ab + dt_row)
    tri = tri_ref[...]
    gc_n = jnp.concatenate([_mm_sel(tri, g_n[s * SUPER:(s + 1) * SUPER], 2) for s in range(L // SUPER)], axis=0)
    gc_h = _mm_xsel(gc_n, e_gc_ref[...], 2)
    beta_pair = _mm_xsel(_sigmoid(ab), e_beta_ref[...], 2)
    return gc_h, beta_pair


def _gdn_masks():
    lane = _iota2((SUPER, LANES), 1)
    row = _iota2((SUPER, LANES), 0)
    chunk_bits = CHUNK.bit_length() - 1
    same_chunk = jnp.right_shift(row, chunk_bits) == jnp.right_shift(lane, chunk_bits)
    first_half = lane < CHUNK

    def block(t, size):
        return jnp.right_shift(t, size.bit_length() - 1)

    def merge_mask(m):
        lower_left = jnp.logical_and(jnp.bitwise_and(block(row, m), 1) == 1, jnp.bitwise_and(block(lane, m), 1) == 0)
        return jnp.where(jnp.logical_and(block(row, 2 * m) == block(lane, 2 * m), lower_left), 1.0, 0.0).astype(F32)

    return dict(
        incl=jnp.logical_and(same_chunk, row >= lane), strict=jnp.logical_and(same_chunk, row > lane),
        eye=row == lane, first_half=first_half,
        diag8=jnp.where(block(row, SUBLANES) == block(lane, SUBLANES), 1.0, 0.0).astype(F32),
        merge_masks=tuple(merge_mask(SUBLANES << i) for i in range((CHUNK // SUBLANES).bit_length() - 1)),
        masks=(jnp.where(first_half, 1.0, 0.0).astype(F32)[0:1, :], jnp.where(first_half, 0.0, 1.0).astype(F32)[0:1, :]))


def _conv_rolled(cur, tail_ref, w_ref, c0, c1):
    L = cur.shape[0]
    tail = tail_ref[0:SUBLANES, c0:c1]
    sub = _iota2(tail.shape, 0)
    y = cur * w_ref[CONV_W - 1:CONV_W, c0:c1]
    for j in range(1, CONV_W):
        shifted = pltpu.roll(cur, j, 0)
        head = jnp.where(sub < j, pltpu.roll(tail, j, 0), shifted[0:SUBLANES])
        shifted = jnp.concatenate([head, shifted[SUBLANES:]], axis=0)
        y = y + shifted * w_ref[CONV_W - 1 - j:CONV_W - j, c0:c1]
    tail_ref[0:SUBLANES, c0:c1] = cur[L - SUBLANES:L]
    return y


def _gdn_conv(proj, tail_ref, conv_w_ref):
    raw = proj(C_GQ, C_GZ)
    return [_silu(_conv_rolled(raw[:, c:c + GDN_W], tail_ref, conv_w_ref, c, c + GDN_W))
            for c in range(0, 3 * GDN_W, GDN_W)]


def _gdn_prep(ctx, sc, acts):
    rs = slice(sc * SUPER, (sc + 1) * SUPER)
    q_act, k_act, v_act = acts
    bd2b = ctx["bd2"].astype(BF16)
    q, k, v = [], [], []
    for p in range(GDN_W // LANES):
        lanes = slice(p * LANES, (p + 1) * LANES)
        qk = jnp.concatenate([q_act[rs, lanes], k_act[rs, lanes]], axis=0)
        qk = qk * lax.rsqrt(_mm(qk * qk, bd2b) + L2_EPS)
        q.append(qk[:SUPER] * (GDN_DK ** -0.5))
        k.append(qk[SUPER:])
        v.append(v_act[rs, lanes])
    ctx["qkv"][sc] = (q, k, v)


def _gdn_parallel_steps(ctx, scs, out):
    n_pairs = GDN_W // LANES
    m_a, m_b = ctx["masks"]
    eye, strict, incl = ctx["eye"], ctx["strict"], ctx["incl"]
    mb_a, mb_b = m_a.astype(BF16), m_b.astype(BF16)
    kk, qk, rhs, gcs_all, per_sc = [], [], [], [], []
    for sc in scs:
        rs = slice(sc * SUPER, (sc + 1) * SUPER)
        q_p, k_p, v_p = ctx["qkv"][sc]
        gcs = jnp.stack([ctx["gc_h"][rs, h * LANES:(h + 1) * LANES] for h in range(GDN_HEADS)])
        gcs_all.append(gcs)
        gc_pair, q_dec = [], []
        for p in range(n_pairs):
            kp, qp, vp = k_p[p], q_p[p], v_p[p]
            bp = ctx["beta"][rs, p * LANES:(p + 1) * LANES]
            gcp = jnp.where(ctx["first_half"], gcs[2 * p], gcs[2 * p + 1])
            eg = jnp.exp(gcp)
            kbb = kp * bp
            kbb_b, qp_b = kbb.astype(BF16), qp.astype(BF16)
            y = _mm_nt(jnp.concatenate([kbb_b * mb_a, kbb_b * mb_b, qp_b * mb_a, qp_b * mb_b], axis=0), kp)
            kk += [y[0:SUPER], y[SUPER:2 * SUPER]]
            qk += [y[2 * SUPER:3 * SUPER], y[3 * SUPER:4 * SUPER]]
            r_p = jnp.concatenate([vp * bp, kbb * eg], axis=1).astype(BF16)
            rhs += [r_p * jnp.concatenate([mb_a, mb_a], axis=1), r_p * jnp.concatenate([mb_b, mb_b], axis=1)]
            gc_pair.append(gcp)
            q_dec.append(qp * eg)
        per_sc.append(dict(gc_pair=gc_pair, q_dec=q_dec, k=k_p))
    gcs = jnp.concatenate(gcs_all, axis=0)
    rowterm = jnp.sum(jnp.where(eye[None], gcs, 0.0), axis=1, keepdims=True)
    dec = jnp.exp(jnp.minimum(gcs - rowterm, 0.0))
    yield
    n_mat = jnp.where(strict[None], jnp.stack(kk) * dec, 0.0)
    qkd = jnp.where(incl[None], jnp.stack(qk) * dec, 0.0)
    n8 = n_mat * ctx["diag8"]
    x_inv = jnp.where(eye, 1.0, 0.0)[None] - n8
    pw = _bmm(n8, n8)
    yield
    pw = pw.astype(BF16)
    y = _bmm(jnp.concatenate([pw, x_inv.astype(BF16)], axis=1), pw)
    x_inv = x_inv + y[:, SUPER:]
    yield
    x_inv = x_inv + _bmm(x_inv, y[:, :SUPER])
    yield
    for off in ctx["merge_masks"]:
        t_a = _bmm(x_inv, n_mat * off)
        yield
        x_inv = x_inv - _bmm(t_a, x_inv)
        yield
    uw = _bmm(x_inv, jnp.stack(rhs))
    for i, sc in enumerate(scs):
        o = GDN_HEADS * i
        out[sc] = dict(
            u=[uw[o + 2 * p, :, :LANES] + uw[o + 2 * p + 1, :, :LANES] for p in range(n_pairs)],
            w=[uw[o + 2 * p, :, LANES:] + uw[o + 2 * p + 1, :, LANES:] for p in range(n_pairs)],
            qkd=qkd[o:o + GDN_HEADS], **per_sc[i])
    yield


def _gdn_recurrence_steps(ctx, par, state_ref, o_rows):
    n_pairs = GDN_W // LANES
    mb_a, mb_b = (m.astype(BF16) for m in ctx["masks"])
    bd2 = ctx["bd2"]
    zeros = jnp.zeros((CHUNK, LANES), BF16)
    for c in range(SUPER // CHUNK):
        cs = slice(c * CHUNK, (c + 1) * CHUNK)
        last = c * CHUNK + CHUNK - 1
        ys = [_mm(jnp.concatenate([par["w"][p][cs], par["q_dec"][p][cs]], axis=0), state_ref[p])
              for p in range(n_pairs)]
        yield
        for p in range(n_pairs):
            gcp = par["gc_pair"][p]
            gl_row = gcp[last:last + 1, :]
            v_new = par["u"][p][cs] - ys[p][:CHUNK]
            k_tail = par["k"][p][cs] * jnp.exp(gl_row - gcp[cs])
            v_new_b = v_new.astype(BF16)
            vm_a, vm_b = v_new_b * mb_a, v_new_b * mb_b
            v_cat = jnp.concatenate([vm_a, zeros, vm_b, zeros] if c == 0 else [zeros, vm_a, zeros, vm_b], axis=0)
            q_cat = jnp.concatenate([par["qkd"][2 * p][cs], par["qkd"][2 * p + 1][cs]], axis=1)
            o_rows[p].append(ys[p][CHUNK:] + _mm(q_cat, v_cat))
            state_ref[p] = state_ref[p] * jnp.exp(gl_row) + _mm_tn(k_tail, v_new_b) * bd2
        yield


def _mixer_body(L, x_ref, cos_ref, sin_ref, win_ref, wout_ref,
                mk_ref, mv_ref, decay_ref, cross_ref, tail_ref, sdec_ref, bdr_ref, avg_ref, rng_ref,
                lcw_ref, lcb_ref, wax_ref, lba_ref, lbx_ref, lam_ref,
                gcw_ref, galog_ref, gdt_ref, gng_ref, tri_ref, egc_ref, ebeta_ref, bd2_ref, avg2_ref,
                lng_ref, lnb_ref,
                o_ref, ret_state, gdn_state, lru_h, lru_buf, gdn_buf):
    @pl.when(pl.program_id(1) == 0)
    def _reset():
        ret_state[...] = jnp.zeros_like(ret_state)
        gdn_state[...] = jnp.zeros_like(gdn_state)
        lru_h[...] = jnp.zeros_like(lru_h)
        lru_buf[...] = jnp.zeros_like(lru_buf)
        gdn_buf[...] = jnp.zeros_like(gdn_buf)

    x = x_ref[...]
    xb = x.astype(BF16)

    def proj(lo, hi):
        return jnp.dot(xb, win_ref[:, lo:hi], preferred_element_type=F32)

    n_pairs = GDN_W // LANES
    n_lru = LRU_W // LANES
    n_super = L // SUPER
    gdn_acts = _gdn_conv(proj, gdn_buf, gcw_ref)
    ab = proj(C_AB, D_IN_PAD)
    lru_x = proj(C_LX, C_LG)
    lru_gate = proj(C_LG, C_GQ)
    gc_h, beta_pair = _gdn_gates(ab, galog_ref[...], gdt_ref[...], tri_ref, egc_ref, ebeta_ref, L)
    ctx = dict(gc_h=gc_h, beta=beta_pair, bd2=bd2_ref[...], qkv={}, **_gdn_masks())
    _gdn_prep(ctx, 0, gdn_acts)
    lru_pre = _rg_lru_conv(lru_x, lru_buf, lcw_ref, lcb_ref[...], wax_ref)
    heads = {}
    ret_gen = _retention_steps(proj, cos_ref[...], sin_ref[...],
                               (mk_ref, mv_ref, decay_ref, cross_ref, tail_ref, sdec_ref, bdr_ref, avg_ref),
                               rng_ref[...], ret_state, heads)
    next(ret_gen)
    lru_ab = _rg_lru_gates(lru_pre, lba_ref[...], lbx_ref[...], lam_ref[...])
    next(ret_gen)
    if n_super > 1:
        _gdn_prep(ctx, 1, gdn_acts)
    par = {}
    o_rows = [[] for _ in range(n_pairs)]
    lru_cols = []
    mix_l = []

    def lru_out_steps():
        o_l = (jnp.concatenate(lru_cols, axis=1) * _gelu_tanh(lru_gate)).astype(BF16)
        for c in range(0, D_MODEL, RET_W):
            mix_l.append(jnp.dot(o_l, wout_ref[RET_W:RET_W + LRU_W, c:c + RET_W], preferred_element_type=F32))
            yield

    pairs = [tuple(range(s, min(s + 2, n_super))) for s in range(0, n_super, 2)]
    _drain(_gdn_parallel_steps(ctx, pairs[0], par))
    for b, scs in enumerate(pairs):
        last = b + 1 == len(pairs)
        if not last:
            for sc in pairs[b + 1]:
                _gdn_prep(ctx, sc, gdn_acts)
        while len(lru_cols) < (n_lru if last else min(n_lru, 2 * (b + 1))):
            t = len(lru_cols)
            lru_cols.append(_rg_lru_scan(lru_ab[t][0], lru_ab[t][1], lru_h, t * LANES, (t + 1) * LANES, L))
        nxt = iter(()) if last else _gdn_parallel_steps(ctx, pairs[b + 1], par)
        lru_out = lru_out_steps() if last else iter(())
        step = 0
        for sc in scs:
            for _ in _gdn_recurrence_steps(ctx, par[sc], gdn_state, o_rows):
                next(nxt, None)
                next(nxt, None)
                if last:
                    next(ret_gen, None)
                    if step % 2 == 0:
                        next(ret_gen, None)
                        next(lru_out, None)
                step += 1
        _drain(nxt)
    _drain(ret_gen)
    _drain(lru_out)

    mix = _mm(heads["ret"], wout_ref[0:RET_W, :]) + jnp.concatenate(mix_l, axis=1)
    avg2 = avg2_ref[...]
    z = proj(C_GZ, C_AB)
    outs = []
    for p in range(n_pairs):
        op = jnp.concatenate(o_rows[p], axis=0)
        outs.append(op * lax.rsqrt(_mm(op * op, avg2) + RMS_EPS))
    o_g = jnp.concatenate(outs, axis=1) * gng_ref[...] * _silu(z)
    w_g = wout_ref[RET_W + LRU_W:, :]
    slabs = [slice(r, r + RET_BLOCK) for r in range(0, L, RET_BLOCK)]
    mix_g = [_mm(o_g[rows], w_g) for rows in slabs]
    for rows, m_g in zip(slabs, mix_g):
        o_ref[rows, :] = _layer_norm(ALPHA * x[rows] + mix[rows] + m_g, lng_ref[...], lnb_ref[...])


def _mixer_constants():
    L = RET_BLOCK
    lg = jnp.log1p(-jnp.exp2(-5.0 - jnp.arange(RET_HEADS, dtype=F32)))
    lane = np.arange(RET_W)
    khead = (lane % LANES) // (RET_DK // 2)
    vhead = lane // RET_DK
    mk = jnp.asarray(np.pad((khead[None, :] == np.arange(RET_HEADS)[:, None]).astype(np.float32), ((0, 4), (0, 0))))
    mv = jnp.asarray(np.pad((vhead[None, :] == np.arange(RET_HEADS)[:, None]).astype(np.float32), ((0, 4), (0, 0))))
    idx = np.arange(L)
    diff = jnp.asarray((idx[:, None] - idx[None, :]).astype(np.float32))
    same = jnp.asarray(idx[:, None] // CHUNK == idx[None, :] // CHUNK)
    earlier = jnp.asarray(idx[None, :] // CHUNK < idx[:, None] // CHUNK)
    decay = jnp.where(same[None], jnp.exp(jnp.abs(diff)[None] * lg[:, None, None]),
                      jnp.where(earlier[None], jnp.exp(diff[None] * lg[:, None, None]), 0.0))
    lg_lane = lg[jnp.asarray(khead)]
    fidx = jnp.asarray(idx.astype(np.float32))
    cross = jnp.exp((fidx + 1.0)[:, None] * lg_lane[None, :])
    tail = jnp.exp((L - 1.0 - fidx)[:, None] * lg_lane[None, :])
    sdec = jnp.broadcast_to(jnp.exp(float(L) * lg_lane)[:, None], (RET_W, RET_W))
    bdr = jnp.asarray((khead[:, None] == vhead[None, :]).astype(np.float32))
    avg = jnp.asarray((vhead[:, None] == vhead[None, :]).astype(np.float32) / RET_DK).astype(BF16)
    sidx = np.arange(SUPER)
    tri = jnp.asarray(np.logical_and(sidx[:, None] // CHUNK == sidx[None, :] // CHUNK,
                                     sidx[:, None] >= sidx[None, :]).astype(np.float32)).astype(BF16)
    e_gc = np.zeros((LANES, GDN_HEADS * LANES), np.float32)
    e_beta = np.zeros((LANES, GDN_W), np.float32)
    for h in range(GDN_HEADS):
        e_gc[h, h * LANES:(h + 1) * LANES] = 1.0
        e_beta[GDN_HEADS + h, h * GDN_DK:(h + 1) * GDN_DK] = 1.0
    l2 = np.arange(LANES) // GDN_DK
    bd2 = (l2[:, None] == l2[None, :]).astype(np.float32)
    return dict(mk=mk, mv=mv, decay=decay, cross=cross, tail=tail, sdec=sdec, bdr=bdr, avg=avg, tri=tri,
                e_gc=jnp.asarray(e_gc).astype(BF16), e_beta=jnp.asarray(e_beta).astype(BF16),
                bd2=jnp.asarray(bd2), avg2=jnp.asarray(bd2 / GDN_DK).astype(BF16))


def _mixer(x, cos, sin, w, consts):
    B, S, _ = x.shape
    L = min(MIX_BLOCK, S)
    tok = lambda n: pl.BlockSpec((None, L, n), lambda b, j: (b, j, 0))
    c = consts
    operands = [
        (x, tok(D_MODEL)), (cos, tok(LANES)), (sin, tok(LANES)),
        (w["w_in"], _layer_spec(w["w_in"].shape, w["layer"])), (w["w_out"], _layer_spec(w["w_out"].shape, w["layer"])),
        (c["mk"], None), (c["mv"], None), (c["decay"], None), (c["cross"], None), (c["tail"], None),
        (c["sdec"], None), (c["bdr"], None), (c["avg"], None), (w["ret_norm_g"], None),
        (w["lru_conv_w"], None), (w["lru_conv_b"], None), (w["lru_wax"], None), (w["lru_b_a"], None),
        (w["lru_b_x"], None), (w["lru_lambda"], None),
        (w["gdn_conv_w"], None), (w["gdn_a_log"], None), (w["gdn_dt_bias"], None), (w["gdn_norm_g"], None),
        (c["tri"], None), (c["e_gc"], None), (c["e_beta"], None), (c["bd2"], None), (c["avg2"], None),
        (w["ln_g"], None), (w["ln_b"], None),
    ]
    args = [a for a, _ in operands]
    specs = [s if s is not None else _const_spec(a.shape) for a, s in operands]
    return pl.pallas_call(
        functools.partial(_mixer_body, L),
        grid=(B, S // L),
        in_specs=specs,
        out_specs=tok(D_MODEL),
        out_shape=jax.ShapeDtypeStruct((B, S, D_MODEL), F32),
        scratch_shapes=[
            pltpu.VMEM((RET_W, RET_W), F32),
            pltpu.VMEM((GDN_W // LANES, LANES, LANES), F32),
            pltpu.VMEM((SUBLANES, LRU_W), F32),
            pltpu.VMEM((SUBLANES, LRU_W), F32),
            pltpu.VMEM((SUBLANES, 3 * GDN_W), F32),
        ],
        compiler_params=pltpu.CompilerParams(dimension_semantics=("arbitrary", "arbitrary"),
                                             vmem_limit_bytes=VMEM_LIMIT_BYTES),
        name="mixer",
    )(*args)


def _relayout_w_in(w):
    n, d = w.shape[0], w.shape[1]
    half = RET_DK // 2

    def halves_apart(cols):
        return cols.reshape(n, d, RET_HEADS, 2, half).transpose(0, 1, 3, 2, 4).reshape(n, d, RET_W)

    parts = [halves_apart(w[..., 0:RET_W]), halves_apart(w[..., RET_W:2 * RET_W]), w[..., 2 * RET_W:],
             jnp.zeros((n, d, D_IN_PAD - D_IN), w.dtype)]
    return jnp.concatenate(parts, axis=2).astype(BF16)


def _pair_block_diag(w):
    z = jnp.zeros((LRU_BLOCK, LRU_BLOCK), w.dtype)
    slabs = []
    for p in range(LRU_W // LANES):
        top = jnp.concatenate([w[2 * p], z], axis=1)
        bot = jnp.concatenate([z, w[2 * p + 1]], axis=1)
        slabs.append(jnp.concatenate([top, bot], axis=0))
    return jnp.stack(slabs)


def _pad_rows(a, rows):
    return jnp.pad(a, ((0, rows - a.shape[0]), (0, 0)))


def _head_row(v):
    return jnp.pad(v, (0, LANES - v.shape[0])).reshape(1, LANES)


def kernel(x, p, positions, ln_ffn1_g, ln_ffn1_b, ffn1_w_gate, ffn1_w_up, ffn1_w_down, w_in, ret_norm_g,
           lru_conv_w, lru_conv_b, lru_w_a, lru_b_a, lru_w_x, lru_b_x, lru_lambda, gdn_conv_w, gdn_a_log,
           gdn_dt_bias, gdn_norm_g, w_out, ln_mix_g, ln_mix_b, ffn2_w_gate, ffn2_w_up, ffn2_w_down,
           ple_w_gate, ple_w_proj, ln_ffn2_g, ln_ffn2_b):
    B, S, _ = x.shape
    T = B * S
    row = lambda v: v.reshape(1, -1)
    cos, sin = _rope_tables(positions)
    consts = _mixer_constants()
    bf = lambda w: w.astype(BF16)
    ffn1_w = (_interleave_gate_up(ffn1_w_gate, ffn1_w_up), bf(ffn1_w_down))
    ffn2_w = (_interleave_gate_up(ffn2_w_gate, ffn2_w_up), bf(ffn2_w_down))
    ple_w = (bf(ple_w_gate), bf(ple_w_proj))
    w_in_all, w_out_all = _relayout_w_in(w_in), bf(w_out)
    for i in range(DEPTH):
        x2 = _ffn(x.reshape(T, D_MODEL), i, *ffn1_w, row(ln_ffn1_g[i]), row(ln_ffn1_b[i]))
        mix_w = dict(
            layer=i, w_in=w_in_all, w_out=w_out_all,
            ret_norm_g=row(ret_norm_g[i]),
            lru_conv_w=_pad_rows(lru_conv_w[i], SUBLANES), lru_conv_b=row(lru_conv_b[i]),
            lru_wax=jnp.concatenate([_pair_block_diag(lru_w_a[i]), _pair_block_diag(lru_w_x[i])], axis=2).astype(BF16),
            lru_b_a=row(lru_b_a[i]), lru_b_x=row(lru_b_x[i]), lru_lambda=row(lru_lambda[i]),
            gdn_conv_w=_pad_rows(gdn_conv_w[i], SUBLANES),
            gdn_a_log=_head_row(gdn_a_log[i]), gdn_dt_bias=_head_row(gdn_dt_bias[i]),
            gdn_norm_g=row(jnp.tile(gdn_norm_g[i], GDN_HEADS)),
            ln_g=row(ln_mix_g[i]), ln_b=row(ln_mix_b[i]),
        )
        x3 = _mixer(x2.reshape(B, S, D_MODEL), cos, sin, mix_w, consts)
        x = _ffn(x3.reshape(T, D_MODEL), i, *ffn2_w, row(ln_ffn2_g[i]), row(ln_ffn2_b[i]),
                 ple=(p.reshape(DEPTH, T, PLE_DIM),) + ple_w)
        x = x.reshape(B, S, D_MODEL)
    return x
```
